```python
import math
import jax, jax.numpy as jnp
from jax import lax
import numpy as np

D_MODEL = 1024
BATCH = 2
SEQ = 16384
DEPTH = 2
DEC_BATCH = 1
DEC_SEQ = 16384
PAST_LEN = 128

HEAD_DIM = 64
A_HEADS = 8
A_KV_HEADS = 2
A_GROUP = A_HEADS // A_KV_HEADS
B_HEADS = 4
B_V_DIM = 2 * HEAD_DIM
A_WIDTH = A_HEADS * HEAD_DIM
B_WIDTH = B_HEADS * B_V_DIM
MIX_WIDTH = A_WIDTH + B_WIDTH
A_Q = A_HEADS * HEAD_DIM
A_KV = A_KV_HEADS * HEAD_DIM
B_QK = B_HEADS * 2 * HEAD_DIM
B_V = B_HEADS * B_V_DIM
IN_WIDTH = A_Q + 2 * A_KV + 2 * B_QK + B_V
SPLIT_POINTS = (A_Q, A_Q + A_KV, A_Q + 2 * A_KV, A_Q + 2 * A_KV + B_QK, A_Q + 2 * A_KV + 2 * B_QK)
D_FF = -(-8 * D_MODEL // (3 * 256)) * 256
GRID_W = 64
AXIAL_THETA = 10000.0
ROPE_THETA = 500000.0
PARTIAL_ROT = HEAD_DIM // 4
Q_BLOCK = 128
NORM_EPS = 1e-6

kernel_name = "hybrid_axial_gqa_diffattn_encoder"


def rms_norm(x, g):
    xf = x.astype(jnp.float32)
    y = xf * lax.rsqrt(jnp.mean(xf * xf, axis=-1, keepdims=True) + NORM_EPS)
    return (y * g.astype(jnp.float32)).astype(x.dtype)


def rope_tables(pos, dim, theta):
    inv = 1.0 / (theta ** (jnp.arange(0, dim, 2, dtype=jnp.float32) / dim))
    ang = pos.astype(jnp.float32)[:, None] * inv[None, :]
    return (jnp.cos(ang), jnp.sin(ang))


def rotate(x, cos, sin):
    half = x.shape[-1] // 2
    xf = x.astype(jnp.float32)
    x1, x2 = xf[..., :half], xf[..., half:]
    c = cos[:, None, :]
    s = sin[:, None, :]
    return jnp.concatenate([x1 * c - x2 * s, x1 * s + x2 * c], axis=-1).astype(x.dtype)


def axial_rope(x, tabs):
    cos_r, sin_r, cos_c, sin_c = tabs
    half = HEAD_DIM // 2
    return jnp.concatenate([rotate(x[..., :half], cos_r, sin_r), rotate(x[..., half:], cos_c, sin_c)], axis=-1)


def partial_rope(x, cos, sin):
    return jnp.concatenate([rotate(x[..., :PARTIAL_ROT], cos, sin), x[..., PARTIAL_ROT:]], axis=-1)


def sweep_query_blocks(fn, qs):
    b, s = qs[0].shape[:2]
    nb = s // Q_BLOCK
    blocks = tuple(jnp.moveaxis(q.reshape((b, nb, Q_BLOCK) + q.shape[2:]), 1, 0) for q in qs)
    out = lax.map(fn, blocks)
    out = jnp.moveaxis(out, 0, 1)
    return out.reshape((b, s) + out.shape[3:])


def gqa_attention(q, k, v):
    scale = HEAD_DIM ** -0.5

    def block(qs):
        (qb,) = qs
        sc = jnp.einsum("bqkgd,bskd->bkgqs", qb, k).astype(jnp.float32) * scale
        p = jax.nn.softmax(sc, axis=-1).astype(v.dtype)
        return jnp.einsum("bkgqs,bskd->bqkgd", p, v)

    return sweep_query_blocks(block, (q,))


def diff_attention(q1, q2, k1, k2, v, lam):
    scale = HEAD_DIM ** -0.5

    def block(qs):
        q1b, q2b = qs
        s1 = jnp.einsum("bqhd,bshd->bhqs", q1b, k1).astype(jnp.float32) * scale
        s2 = jnp.einsum("bqhd,bshd->bhqs", q2b, k2).astype(jnp.float32) * scale
        p = (jax.nn.softmax(s1, axis=-1) - lam * jax.nn.softmax(s2, axis=-1)).astype(v.dtype)
        return jnp.einsum("bhqs,bshe->bqhe", p, v)

    return sweep_query_blocks(block, (q1, q2))


def encoder_layer(x, c_act, tabs_axial, tabs_partial, layer_idx, w_ada, b_ada, g_mix, w_in, g_q_a, g_k_a,
                  lam_q1, lam_k1, lam_q2, lam_k2, g_subln, w_out, g_ffn, w_gate_up, w_down):
    b, s, _ = x.shape
    mod = (c_act @ w_ada + b_ada)[:, None, :]
    sh1, sc1, gt1, sh2, sc2, gt2 = jnp.split(mod, 6, axis=-1)

    h = rms_norm(x, g_mix) * (1.0 + sc1) + sh1
    proj = h @ w_in
    qa, ka, va, qb, kb, vb = jnp.split(proj, SPLIT_POINTS, axis=-1)

    qa = axial_rope(rms_norm(qa.reshape(b, s, A_HEADS, HEAD_DIM), g_q_a), tabs_axial)
    qa = qa.reshape(b, s, A_KV_HEADS, A_GROUP, HEAD_DIM)
    ka = axial_rope(rms_norm(ka.reshape(b, s, A_KV_HEADS, HEAD_DIM), g_k_a), tabs_axial)
    va = va.reshape(b, s, A_KV_HEADS, HEAD_DIM)
    o_a = gqa_attention(qa, ka, va).reshape(b, s, A_WIDTH)

    cos_p, sin_p = tabs_partial
    qb = partial_rope(qb.reshape(b, s, B_HEADS * 2, HEAD_DIM), cos_p, sin_p).reshape(b, s, B_HEADS, 2, HEAD_DIM)
    kb = partial_rope(kb.reshape(b, s, B_HEADS * 2, HEAD_DIM), cos_p, sin_p).reshape(b, s, B_HEADS, 2, HEAD_DIM)
    vb = vb.reshape(b, s, B_HEADS, B_V_DIM)
    lam_init = 0.8 - 0.6 * math.exp(-0.3 * layer_idx)
    lam = (jnp.exp(jnp.sum(lam_q1.astype(jnp.float32) * lam_k1.astype(jnp.float32)))
           - jnp.exp(jnp.sum(lam_q2.astype(jnp.float32) * lam_k2.astype(jnp.float32))) + lam_init)
    o_b = diff_attention(qb[..., 0, :], qb[..., 1, :], kb[..., 0, :], kb[..., 1, :], vb, lam)
    o_b = (rms_norm(o_b, g_subln) * (1.0 - lam_init)).reshape(b, s, B_WIDTH)

    mixed = jnp.concatenate([o_a, o_b], axis=-1) @ w_out
    x = x + gt1 * mixed

    h = rms_norm(x, g_ffn) * (1.0 + sc2) + sh2
    gate, up = jnp.split(h @ w_gate_up, 2, axis=-1)
    x = x + gt2 * ((jax.nn.silu(gate) * up) @ w_down)
    return x


def run_trunk(x, c, w_ada, b_ada, g_mix, w_in, g_q_a, g_k_a, lam_q1, lam_k1, lam_q2, lam_k2, g_subln,
              w_out, g_ffn, w_gate_up, w_down, w_ada_final, b_ada_final, g_final):
    n = x.shape[1]
    rows = n // GRID_W
    row_idx = jnp.repeat(jnp.arange(rows), GRID_W)
    col_idx = jnp.tile(jnp.arange(GRID_W), rows)
    tabs_axial = rope_tables(row_idx, HEAD_DIM // 2, AXIAL_THETA) + rope_tables(col_idx, HEAD_DIM // 2, AXIAL_THETA)
    tabs_partial = rope_tables(jnp.arange(n), PARTIAL_ROT, ROPE_THETA)
    c_act = jax.nn.silu(c)
    for l in range(DEPTH):
        x = encoder_layer(x, c_act, tabs_axial, tabs_partial, l, w_ada[l], b_ada[l], g_mix[l], w_in[l],
                          g_q_a[l], g_k_a[l], lam_q1[l], lam_k1[l], lam_q2[l], lam_k2[l], g_subln[l],
                          w_out[l], g_ffn[l], w_gate_up[l], w_down[l])
    mod = (c_act @ w_ada_final + b_ada_final)[:, None, :]
    shift, scale = jnp.split(mod, 2, axis=-1)
    return rms_norm(x, g_final) * (1.0 + scale) + shift


def setup_inputs(seed: int = 0) -> dict:
    key = jax.random.key(seed)
    ks = jax.random.split(key, 24)
    f32 = jnp.float32
    nrm = lambda k, shape, s: jax.random.normal(k, shape, f32) * s
    return {
        "x_prompt": nrm(ks[0], (BATCH, SEQ, D_MODEL), 1.0),
        "x_sample": nrm(ks[1], (DEC_BATCH, DEC_SEQ, D_MODEL), 1.0),
        "c_prompt": nrm(ks[2], (BATCH, D_MODEL), 1.0),
        "c_sample": nrm(ks[3], (DEC_BATCH, D_MODEL), 1.0),
        "w_ada": nrm(ks[4], (DEPTH, D_MODEL, 6 * D_MODEL), 0.5 * D_MODEL ** -0.5),
        "b_ada": nrm(ks[5], (DEPTH, 6 * D_MODEL), 0.02),
        "g_mix": 1.0 + nrm(ks[6], (DEPTH, D_MODEL), 0.05),
        "w_in": nrm(ks[7], (DEPTH, D_MODEL, IN_WIDTH), D_MODEL ** -0.5),
        "g_q_a": 1.0 + nrm(ks[8], (DEPTH, HEAD_DIM), 0.05),
        "g_k_a": 1.0 + nrm(ks[9], (DEPTH, HEAD_DIM), 0.05),
        "lam_q1": nrm(ks[10], (DEPTH, HEAD_DIM), 0.1),
        "lam_k1": nrm(ks[11], (DEPTH, HEAD_DIM), 0.1),
        "lam_q2": nrm(ks[12], (DEPTH, HEAD_DIM), 0.1),
        "lam_k2": nrm(ks[13], (DEPTH, HEAD_DIM), 0.1),
        "g_subln": 1.0 + nrm(ks[14], (DEPTH, B_V_DIM), 0.05),
        "w_out": nrm(ks[15], (DEPTH, MIX_WIDTH, D_MODEL), MIX_WIDTH ** -0.5),
        "g_ffn": 1.0 + nrm(ks[16], (DEPTH, D_MODEL), 0.05),
        "w_gate_up": nrm(ks[17], (DEPTH, D_MODEL, 2 * D_FF), D_MODEL ** -0.5),
        "w_down": nrm(ks[18], (DEPTH, D_FF, D_MODEL), D_FF ** -0.5),
        "w_ada_final": nrm(ks[19], (D_MODEL, 2 * D_MODEL), 0.5 * D_MODEL ** -0.5),
        "b_ada_final": nrm(ks[20], (2 * D_MODEL,), 0.02),
        "g_final": 1.0 + nrm(ks[21], (D_MODEL,), 0.05),
    }


def reference(x_prompt, x_sample, c_prompt, c_sample, w_ada, b_ada, g_mix, w_in, g_q_a, g_k_a,
              lam_q1, lam_k1, lam_q2, lam_k2, g_subln, w_out, g_ffn, w_gate_up, w_down,
              w_ada_final, b_ada_final, g_final):
    y_prompt = run_trunk(x_prompt, c_prompt, w_ada, b_ada, g_mix, w_in, g_q_a, g_k_a, lam_q1, lam_k1,
                         lam_q2, lam_k2, g_subln, w_out, g_ffn, w_gate_up, w_down,
                         w_ada_final, b_ada_final, g_final)
    y_sample = run_trunk(x_sample, c_sample, w_ada, b_ada, g_mix, w_in, g_q_a, g_k_a, lam_q1, lam_k1,
                         lam_q2, lam_k2, g_subln, w_out, g_ffn, w_gate_up, w_down,
                         w_ada_final, b_ada_final, g_final)
    return (y_prompt, y_sample)
```

```python
import functools
import math

import jax
import jax.numpy as jnp
from jax import lax
from jax.experimental import pallas as pl
from jax.experimental.pallas import tpu as pltpu

HEAD_DIM = 64
A_HEADS = 8
A_KV_HEADS = 2
A_GROUP = A_HEADS // A_KV_HEADS
B_HEADS = 4
B_V_DIM = 2 * HEAD_DIM
A_Q = A_HEADS * HEAD_DIM
A_KV = A_KV_HEADS * HEAD_DIM
B_QK = B_HEADS * 2 * HEAD_DIM
B_V = B_HEADS * B_V_DIM
IN_WIDTH = A_Q + 2 * A_KV + 2 * B_QK + B_V
GRID_W = 64
AXIAL_THETA = 10000.0
ROPE_THETA = 500000.0
PARTIAL_ROT = HEAD_DIM // 4
NORM_EPS = 1e-6
QK_SCALE = HEAD_DIM ** -0.5

_QA0, _KA0, _VA0 = 0, A_Q, A_Q + A_KV
_QB0 = A_Q + 2 * A_KV
_KB0 = _QB0 + B_QK
_VB0 = _KB0 + B_QK

FF_CHUNK = 256
V7X_VMEM_LIMIT_BYTES = 56 * 1024 * 1024

_BF16 = jnp.bfloat16
_F32 = jnp.float32


def _params(semantics):
    return pltpu.CompilerParams(dimension_semantics=semantics, vmem_limit_bytes=V7X_VMEM_LIMIT_BYTES)


def _dot(a, b):
    return jnp.dot(a, b, preferred_element_type=_F32)


def _dot_nt(a, b):
    return lax.dot_general(a, b, (((1,), (1,)), ((), ())), preferred_element_type=_F32)


def _dot_tn(a, b):
    return lax.dot_general(a, b, (((0,), (0,)), ((), ())), preferred_element_type=_F32)


def _split_bf16(x):
    hi = x.astype(_BF16)
    lo = (x - hi.astype(_F32)).astype(_BF16)
    return hi, lo


def _mod_kernel(c_ref, w_ref, b_ref, o_ref):
    c = c_ref[...]
    c_act = c * jax.nn.sigmoid(c)
    c_hi, c_lo = _split_bf16(c_act)
    w_hi, w_lo = _split_bf16(w_ref[...])
    acc = _dot(c_hi, w_hi) + (_dot(c_lo, w_hi) + _dot(c_hi, w_lo))
    o_ref[...] = acc + b_ref[...]


def _modulation(c_pad, w, b, tn=1024):
    nl, d, n = w.shape
    return pl.pallas_call(
        _mod_kernel,
        grid=(nl, n // tn),
        in_specs=[
            pl.BlockSpec((8, d), lambda l, j: (0, 0)),
            pl.BlockSpec((None, d, tn), lambda l, j: (l, 0, j)),
            pl.BlockSpec((None, 1, tn), lambda l, j: (l, 0, j)),
        ],
        out_specs=pl.BlockSpec((None, 8, tn), lambda l, j: (l, 0, j)),
        out_shape=jax.ShapeDtypeStruct((nl, 8, n), _F32),
        compiler_params=_params(("parallel", "parallel")),
        name="adaln_modulation",
    )(c_pad, w, b)


def _rms_rows(x):
    return lax.rsqrt(jnp.mean(x * x, axis=1, keepdims=True) + NORM_EPS)


def _inproj_kernel(x_ref, mod_ref, gmix_ref, wT_ref, gq_ref, gk_ref, tab_ref,
                   qTa_ref, kA_ref, vTa_ref, qTb_ref, kB_ref, vTb_ref, *, tk):
    ts = x_ref.shape[0]
    x = x_ref[...]
    sh1 = mod_ref[0:1, :]
    sc1 = mod_ref[1:2, :]
    h = ((x * _rms_rows(x)) * gmix_ref[...]) * (1.0 + sc1) + sh1
    hb = h.astype(_BF16)

    def proj_t(lo, n):
        return _dot_nt(wT_ref[lo:lo + n, :], hb)

    tab = tab_ref[...]
    cos_r, sin_r, cos_c, sin_c = tab[0:16], tab[16:32], tab[32:48], tab[48:64]
    cos_p, sin_p = tab[64:72], tab[72:80]

    def head_norm(x3, g_ref):
        ms = jnp.mean(x3 * x3, axis=1, keepdims=True)
        return (x3 * lax.rsqrt(ms + NORM_EPS)) * g_ref[...][None]

    def axial(x3):
        r1, r2, c1, c2 = x3[:, 0:16], x3[:, 16:32], x3[:, 32:48], x3[:, 48:64]
        return jnp.concatenate([r1 * cos_r - r2 * sin_r, r1 * sin_r + r2 * cos_r,
                                c1 * cos_c - c2 * sin_c, c1 * sin_c + c2 * cos_c], axis=1)

    def partial(x3):
        half = PARTIAL_ROT // 2
        x1, x2, rest = x3[:, 0:half], x3[:, half:PARTIAL_ROT], x3[:, PARTIAL_ROT:]
        return jnp.concatenate([x1 * cos_p - x2 * sin_p, x1 * sin_p + x2 * cos_p, rest], axis=1)

    qa = proj_t(_QA0, A_Q).reshape(A_HEADS, HEAD_DIM, ts)
    qa = axial(head_norm(qa, gq_ref)) * QK_SCALE
    qTa_ref[...] = qa.reshape(A_Q, ts).astype(_BF16)

    ka = proj_t(_KA0, A_KV).reshape(A_KV_HEADS, HEAD_DIM, ts)
    ka = axial(head_norm(ka, gk_ref)).reshape(A_KV, ts)
    kA_ref[...] = ka.T.astype(_BF16)

    va = proj_t(_VA0, A_KV).astype(_BF16)
    for g in range(A_KV_HEADS):
        for c in range(ts // tk):
            vTa_ref[g, c] = va[g * HEAD_DIM:(g + 1) * HEAD_DIM, c * tk:(c + 1) * tk]

    qb = proj_t(_QB0, B_QK).reshape(2 * B_HEADS, HEAD_DIM, ts)
    qTb_ref[...] = (partial(qb) * QK_SCALE).reshape(B_QK, ts).astype(_BF16)

    kb = proj_t(_KB0, B_QK).reshape(2 * B_HEADS, HEAD_DIM, ts)
    kB_ref[...] = partial(kb).reshape(B_QK, ts).T.astype(_BF16)

    vb = proj_t(_VB0, B_V).astype(_BF16)
    for hd in range(B_HEADS):
        for c in range(ts // tk):
            vTb_ref[hd, c] = vb[hd * B_V_DIM:(hd + 1) * B_V_DIM, c * tk:(c + 1) * tk]


def _inproj(x, mod, gmix, w_inT, gq, gk, tab, *, ts, tk):
    n, s, d = x.shape
    nck = s // tk
    cpt = ts // tk
    out_shape = (
        jax.ShapeDtypeStruct((n, A_Q, s), _BF16),
        jax.ShapeDtypeStruct((n, s, A_KV), _BF16),
        jax.ShapeDtypeStruct((n, A_KV_HEADS, nck, HEAD_DIM, tk), _BF16),
        jax.ShapeDtypeStruct((n, B_QK, s), _BF16),
        jax.ShapeDtypeStruct((n, s, B_QK), _BF16),
        jax.ShapeDtypeStruct((n, B_HEADS, nck, B_V_DIM, tk), _BF16),
    )
    return pl.pallas_call(
        functools.partial(_inproj_kernel, tk=tk),
        grid=(n, s // ts),
        in_specs=[
            pl.BlockSpec((None, ts, d), lambda b, t: (b, t, 0)),
            pl.BlockSpec((None, 6, d), lambda b, t: (b, 0, 0)),
            pl.BlockSpec((1, d), lambda b, t: (0, 0)),
            pl.BlockSpec((IN_WIDTH, d), lambda b, t: (0, 0)),
            pl.BlockSpec((HEAD_DIM, 1), lambda b, t: (0, 0)),
            pl.BlockSpec((HEAD_DIM, 1), lambda b, t: (0, 0)),
            pl.BlockSpec((80, ts), lambda b, t: (0, t)),
        ],
        out_specs=(
            pl.BlockSpec((None, A_Q, ts), lambda b, t: (b, 0, t)),
            pl.BlockSpec((None, ts, A_KV), lambda b, t: (b, t, 0)),
            pl.BlockSpec((None, A_KV_HEADS, cpt, HEAD_DIM, tk), lambda b, t: (b, 0, t, 0, 0)),
            pl.BlockSpec((None, B_QK, ts), lambda b, t: (b, 0, t)),
            pl.BlockSpec((None, ts, B_QK), lambda b, t: (b, t, 0)),
            pl.BlockSpec((None, B_HEADS, cpt, B_V_DIM, tk), lambda b, t: (b, 0, t, 0, 0)),
        ),
        out_shape=out_shape,
        compiler_params=_params(("parallel", "parallel")),
        name="mix_in_projection",
    )(x, mod, gmix, w_inT, gq, gk, tab)


def _online_softmax_step(s, vc, m_ref, l_ref, acc_ref, idx):
    m_prev = m_ref[idx]
    m_new = jnp.maximum(m_prev, jnp.max(s, axis=0, keepdims=True))
    alpha = jnp.exp(m_prev - m_new)
    p = jnp.exp(s - m_new)
    l_ref[idx] = alpha * l_ref[idx] + jnp.sum(p, axis=0, keepdims=True)
    acc_ref[idx] = alpha * acc_ref[idx] + _dot(vc, p.astype(_BF16))
    m_ref[idx] = m_new


def _padded_queries(qT_ref, w_ref, nchains, rows_of_chain):
    tq = qT_ref.shape[-1]
    row = lax.broadcasted_iota(jnp.int32, (2 * HEAD_DIM, tq), 0)
    for c in range(nchains):
        q = qT_ref[c * HEAD_DIM:(c + 1) * HEAD_DIM, :].astype(_F32)
        qq = jnp.concatenate([q, q], axis=0)
        lo = rows_of_chain(c)
        w_ref[c] = jnp.where((row >= lo) & (row < lo + HEAD_DIM), qq, 0.0).astype(_BF16)


def _attn_a_kernel(qT_ref, k_ref, vT_ref, o_ref, w_ref, m_ref, l_ref, acc_ref, *, tk):
    g = pl.program_id(1)
    _padded_queries(qT_ref, w_ref, A_GROUP, lambda c: g * HEAD_DIM)
    m_ref[...] = jnp.full(m_ref.shape, -jnp.inf, _F32)
    l_ref[...] = jnp.zeros(l_ref.shape, _F32)
    acc_ref[...] = jnp.zeros(acc_ref.shape, _F32)

    def body(j, carry):
        kc = k_ref[pl.ds(pl.multiple_of(j * tk, tk), tk), :]
        vc = vT_ref[j]
        for c in range(A_GROUP):
            _online_softmax_step(_dot(kc, w_ref[c]), vc, m_ref, l_ref, acc_ref, c)
        return carry

    lax.fori_loop(0, vT_ref.shape[0], body, 0)
    for c in range(A_GROUP):
        o_ref[c * HEAD_DIM:(c + 1) * HEAD_DIM, :] = (acc_ref[c] / l_ref[c]).astype(_BF16)


def _attn_a(qTa, kA, vTa, *, tq):
    n, _, s = qTa.shape
    _, _, nck, _, tk = vTa.shape
    return pl.pallas_call(
        functools.partial(_attn_a_kernel, tk=tk),
        grid=(n, A_KV_HEADS, s // tq),
        in_specs=[
            pl.BlockSpec((None, A_GROUP * HEAD_DIM, tq), lambda b, g, i: (b, g, i)),
            pl.BlockSpec((None, s, A_KV), lambda b, g, i: (b, 0, 0)),
            pl.BlockSpec((None, None, nck, HEAD_DIM, tk), lambda b, g, i: (b, g, 0, 0, 0)),
        ],
        out_specs=pl.BlockSpec((None, A_GROUP * HEAD_DIM, tq), lambda b, g, i: (b, g, i)),
        out_shape=jax.ShapeDtypeStruct((n, A_Q, s), _BF16),
        scratch_shapes=[
            pltpu.VMEM((A_GROUP, 2 * HEAD_DIM, tq), _BF16),
            pltpu.VMEM((A_GROUP, 1, tq), _F32),
            pltpu.VMEM((A_GROUP, 1, tq), _F32),
            pltpu.VMEM((A_GROUP, HEAD_DIM, tq), _F32),
        ],
        compiler_params=_params(("parallel", "parallel", "parallel")),
        name="gqa_attention",
    )(qTa, kA, vTa)


def _attn_b_kernel(qT_ref, k_ref, vT_ref, lam_ref, gsub_ref, o_ref, w_ref, m_ref, l_ref, acc_ref,
                   *, tk, lam_init):
    _padded_queries(qT_ref, w_ref, 2, lambda c: c * HEAD_DIM)
    m_ref[...] = jnp.full(m_ref.shape, -jnp.inf, _F32)
    l_ref[...] = jnp.zeros(l_ref.shape, _F32)
    acc_ref[...] = jnp.zeros(acc_ref.shape, _F32)

    def body(j, carry):
        kc = k_ref[pl.ds(pl.multiple_of(j * tk, tk), tk), :]
        vc = vT_ref[j]
        for c in range(2):
            _online_softmax_step(_dot(kc, w_ref[c]), vc, m_ref, l_ref, acc_ref, c)
        return carry

    lax.fori_loop(0, vT_ref.shape[0], body, 0)

    lp = lam_ref[...]
    lam = (jnp.exp(jnp.sum(lp[0:1] * lp[1:2], axis=1, keepdims=True))
           - jnp.exp(jnp.sum(lp[2:3] * lp[3:4], axis=1, keepdims=True)) + lam_init)
    o = acc_ref[0] / l_ref[0] - lam * (acc_ref[1] / l_ref[1])
    ms = jnp.mean(o * o, axis=0, keepdims=True)
    o = ((o * lax.rsqrt(ms + NORM_EPS)) * gsub_ref[...]) * (1.0 - lam_init)
    o_ref[...] = o.astype(_BF16)


def _attn_b(qTb, kB, vTb, lam_params, gsub, *, tq, lam_init):
    n, _, s = qTb.shape
    _, _, nck, _, tk = vTb.shape
    return pl.pallas_call(
        functools.partial(_attn_b_kernel, tk=tk, lam_init=lam_init),
        grid=(n, B_HEADS, s // tq),
        in_specs=[
            pl.BlockSpec((None, 2 * HEAD_DIM, tq), lambda b, hd, i: (b, hd, i)),
            pl.BlockSpec((None, s, 2 * HEAD_DIM), lambda b, hd, i: (b, 0, hd)),
            pl.BlockSpec((None, None, nck, B_V_DIM, tk), lambda b, hd, i: (b, hd, 0, 0, 0)),
            pl.BlockSpec((4, HEAD_DIM), lambda b, hd, i: (0, 0)),
            pl.BlockSpec((B_V_DIM, 1), lambda b, hd, i: (0, 0)),
        ],
        out_specs=pl.BlockSpec((None, B_V_DIM, tq), lambda b, hd, i: (b, hd, i)),
        out_shape=jax.ShapeDtypeStruct((n, B_V, s), _BF16),
        scratch_shapes=[
            pltpu.VMEM((2, 2 * HEAD_DIM, tq), _BF16),
            pltpu.VMEM((2, 1, tq), _F32),
            pltpu.VMEM((2, 1, tq), _F32),
            pltpu.VMEM((2, B_V_DIM, tq), _F32),
        ],
        compiler_params=_params(("parallel", "parallel", "parallel")),
        name="diff_attention",
    )(qTb, kB, vTb, lam_params, gsub)


def _ffn_kernel(x_ref, oTa_ref, oTb_ref, mod_ref, wout_ref, gffn_ref, wgu_ref, wdn_ref, y_ref, act_ref):
    mixed = _dot_tn(oTa_ref[...], wout_ref[0:A_Q, :]) + _dot_tn(oTb_ref[...], wout_ref[A_Q:, :])
    gt1, sh2, sc2, gt2 = mod_ref[2:3, :], mod_ref[3:4, :], mod_ref[4:5, :], mod_ref[5:6, :]
    x1 = x_ref[...] + gt1 * mixed
    h = ((x1 * _rms_rows(x1)) * gffn_ref[...]) * (1.0 + sc2) + sh2
    hb = h.astype(_BF16)
    for c in range(wgu_ref.shape[0]):
        gu = _dot(hb, wgu_ref[c])
        gate, up = gu[:, :FF_CHUNK], gu[:, FF_CHUNK:]
        act_ref[:, c * FF_CHUNK:(c + 1) * FF_CHUNK] = ((gate * jax.nn.sigmoid(gate)) * up).astype(_BF16)
    y_ref[...] = x1 + gt2 * _dot(act_ref[...], wdn_ref[...])


def _resident(shape):
    return pl.BlockSpec(shape, lambda b, t: (0,) * len(shape), pipeline_mode=pl.Buffered(1))


def _out_ffn(x, oTa, oTb, mod, w_out, gffn, w_gu, w_dn, *, ts):
    n, s, d = x.shape
    nff, _, _ = w_gu.shape
    dff = w_dn.shape[0]
    return pl.pallas_call(
        _ffn_kernel,
        grid=(n, s // ts),
        in_specs=[
            pl.BlockSpec((None, ts, d), lambda b, t: (b, t, 0)),
            pl.BlockSpec((None, A_Q, ts), lambda b, t: (b, 0, t)),
            pl.BlockSpec((None, B_V, ts), lambda b, t: (b, 0, t)),
            pl.BlockSpec((None, 6, d), lambda b, t: (b, 0, 0)),
            _resident((A_Q + B_V, d)),
            pl.BlockSpec((1, d), lambda b, t: (0, 0)),
            _resident((nff, d, 2 * FF_CHUNK)),
            _resident((dff, d)),
        ],
        out_specs=pl.BlockSpec((None, ts, d), lambda b, t: (b, t, 0)),
        out_shape=jax.ShapeDtypeStruct((n, s, d), _F32),
        scratch_shapes=[pltpu.VMEM((ts, dff), _BF16)],
        compiler_params=_params(("parallel", "parallel")),
        name="out_proj_swiglu",
    )(x, oTa, oTb, mod, w_out, gffn, w_gu, w_dn)


def _final_kernel(x_ref, mod_ref, g_ref, y_ref):
    x = x_ref[...]
    shift, scale = mod_ref[0:1, :], mod_ref[1:2, :]
    y_ref[...] = ((x * _rms_rows(x)) * g_ref[...]) * (1.0 + scale) + shift


def _final_norm(x, modf, g, *, ts):
    n, s, d = x.shape
    return pl.pallas_call(
        _final_kernel,
        grid=(n, s // ts),
        in_specs=[
            pl.BlockSpec((None, ts, d), lambda b, t: (b, t, 0)),
            pl.BlockSpec((None, 2, d), lambda b, t: (b, 0, 0)),
            pl.BlockSpec((1, d), lambda b, t: (0, 0)),
        ],
        out_specs=pl.BlockSpec((None, ts, d), lambda b, t: (b, t, 0)),
        out_shape=jax.ShapeDtypeStruct((n, s, d), _F32),
        compiler_params=_params(("parallel", "parallel")),
        name="final_norm",
    )(x, modf, g)


def _rope_table_t(s):
    def tables(pos, dim, theta):
        inv = 1.0 / (theta ** (jnp.arange(0, dim, 2, dtype=_F32) / dim))
        ang = pos.astype(_F32)[:, None] * inv[None, :]
        return jnp.cos(ang).T, jnp.sin(ang).T

    t = jnp.arange(s)
    cos_r, sin_r = tables(t // GRID_W, HEAD_DIM // 2, AXIAL_THETA)
    cos_c, sin_c = tables(t % GRID_W, HEAD_DIM // 2, AXIAL_THETA)
    cos_p, sin_p = tables(t, PARTIAL_ROT, ROPE_THETA)
    return jnp.concatenate([cos_r, sin_r, cos_c, sin_c, cos_p, sin_p], axis=0)


def _trunk(x, c, w_ada, b_ada, g_mix, w_in, g_q_a, g_k_a, lam_q1, lam_k1, lam_q2, lam_k2, g_subln,
           w_out, g_ffn, w_gate_up, w_down, w_ada_final, b_ada_final, g_final, *, ts, tq, tk):
    n, s, d = x.shape
    depth = w_ada.shape[0]
    dff = w_down.shape[1]
    nff = dff // FF_CHUNK

    c_pad = jnp.zeros((8, d), _F32).at[:n].set(c)
    mod = _modulation(c_pad, w_ada, b_ada[:, None, :])[:, :n].reshape(depth, n, 6, d)
    modf = _modulation(c_pad, w_ada_final[None], b_ada_final[None, None, :])[0, :n].reshape(n, 2, d)

    tab = _rope_table_t(s)
    w_inT = jnp.swapaxes(w_in, 1, 2).astype(_BF16)
    w_out_b = w_out.astype(_BF16)
    w_gu = jnp.concatenate([w_gate_up[:, :, :dff].reshape(depth, d, nff, FF_CHUNK),
                            w_gate_up[:, :, dff:].reshape(depth, d, nff, FF_CHUNK)], axis=-1)
    w_gu = jnp.transpose(w_gu, (0, 2, 1, 3)).astype(_BF16)
    w_dn = w_down.astype(_BF16)
    lam_params = jnp.stack([lam_q1, lam_k1, lam_q2, lam_k2], axis=1)

    for l in range(depth):
        qTa, kA, vTa, qTb, kB, vTb = _inproj(
            x, mod[l], g_mix[l][None, :], w_inT[l], g_q_a[l][:, None], g_k_a[l][:, None], tab, ts=ts, tk=tk)
        oTa = _attn_a(qTa, kA, vTa, tq=tq)
        lam_init = 0.8 - 0.6 * math.exp(-0.3 * l)
        oTb = _attn_b(qTb, kB, vTb, lam_params[l], g_subln[l][:, None], tq=tq, lam_init=lam_init)
        x = _out_ffn(x, oTa, oTb, mod[l], w_out_b[l], g_ffn[l][None, :], w_gu[l], w_dn[l], ts=ts)
    return _final_norm(x, modf, g_final[None, :], ts=ts)


def kernel(x_prompt, x_sample, c_prompt, c_sample, w_ada, b_ada, g_mix, w_in, g_q_a, g_k_a, lam_q1, lam_k1, lam_q2, lam_k2, g_subln, w_out, g_ffn, w_gate_up, w_down, w_ada_final, b_ada_final, g_final):
    nb = x_prompt.shape[0]
    x = jnp.concatenate([x_prompt, x_sample], axis=0)
    c = jnp.concatenate([c_prompt, c_sample], axis=0)
    y = _trunk(x, c, w_ada, b_ada, g_mix, w_in, g_q_a, g_k_a, lam_q1, lam_k1, lam_q2, lam_k2, g_subln,
               w_out, g_ffn, w_gate_up, w_down, w_ada_final, b_ada_final, g_final, ts=512, tq=256, tk=512)
    return (y[:nb], y[nb:])
```

```python
import functools
import math

import jax
import jax.numpy as jnp
from jax import lax
from jax.experimental import pallas as pl
from jax.experimental.pallas import tpu as pltpu

HEAD_DIM = 64
A_HEADS = 8
A_KV_HEADS = 2
A_GROUP = A_HEADS // A_KV_HEADS
B_HEADS = 4
B_V_DIM = 2 * HEAD_DIM
A_Q = A_HEADS * HEAD_DIM
A_KV = A_KV_HEADS * HEAD_DIM
B_QK = B_HEADS * 2 * HEAD_DIM
B_V = B_HEADS * B_V_DIM
IN_WIDTH = A_Q + 2 * A_KV + 2 * B_QK + B_V
GRID_W = 64
AXIAL_THETA = 10000.0
ROPE_THETA = 500000.0
PARTIAL_ROT = HEAD_DIM // 4
NORM_EPS = 1e-6
QK_SCALE = HEAD_DIM ** -0.5 * math.log2(math.e)

_QA0, _KA0, _VA0 = 0, A_Q, A_Q + A_KV
_QB0 = A_Q + 2 * A_KV
_KB0 = _QB0 + B_QK
_VB0 = _KB0 + B_QK

FF_CHUNK = 256
V7X_VMEM_LIMIT_BYTES = 56 * 1024 * 1024

_BF16 = jnp.bfloat16
_F32 = jnp.float32


def _params(semantics):
    return pltpu.CompilerParams(dimension_semantics=semantics, vmem_limit_bytes=V7X_VMEM_LIMIT_BYTES)


def _dot(a, b):
    return jnp.dot(a, b, preferred_element_type=_F32)


def _dot_nt(a, b):
    return lax.dot_general(a, b, (((1,), (1,)), ((), ())), preferred_element_type=_F32)


def _dot_tn(a, b):
    return lax.dot_general(a, b, (((0,), (0,)), ((), ())), preferred_element_type=_F32)


def _split_bf16(x):
    hi = x.astype(_BF16)
    lo = (x - hi.astype(_F32)).astype(_BF16)
    return hi, lo


def _mod_kernel(c_ref, w_ref, b_ref, o_ref):
    c = c_ref[...]
    c_act = c * jax.nn.sigmoid(c)
    c_hi, c_lo = _split_bf16(c_act)
    w_hi, w_lo = _split_bf16(w_ref[...])
    acc = _dot(c_hi, w_hi) + (_dot(c_lo, w_hi) + _dot(c_hi, w_lo))
    o_ref[...] = acc + b_ref[...]


def _modulation(c_pad, w, b, tn=1024):
    nl, d, n = w.shape
    return pl.pallas_call(
        _mod_kernel,
        grid=(nl, n // tn),
        in_specs=[
            pl.BlockSpec((8, d), lambda l, j: (0, 0)),
            pl.BlockSpec((None, d, tn), lambda l, j: (l, 0, j)),
            pl.BlockSpec((None, 1, tn), lambda l, j: (l, 0, j)),
        ],
        out_specs=pl.BlockSpec((None, 8, tn), lambda l, j: (l, 0, j)),
        out_shape=jax.ShapeDtypeStruct((nl, 8, n), _F32),
        compiler_params=_params(("parallel", "parallel")),
        name="adaln_modulation",
    )(c_pad, w, b)


def _rms_rows(x):
    return lax.rsqrt(jnp.mean(x * x, axis=1, keepdims=True) + NORM_EPS)


def _inproj_kernel(x_ref, mod_ref, gmix_ref, wT_ref, gq_ref, gk_ref, tab_ref,
                   qTa_ref, kA_ref, vTa_ref, qTb_ref, kB_ref, vTb_ref, *, tk):
    ts = x_ref.shape[0]
    x = x_ref[...]
    sh1 = mod_ref[0:1, :]
    sc1 = mod_ref[1:2, :]
    h = ((x * _rms_rows(x)) * gmix_ref[...]) * (1.0 + sc1) + sh1
    hb = h.astype(_BF16)

    def proj_t(lo, n):
        return _dot_nt(wT_ref[lo:lo + n, :], hb)

    tab = tab_ref[...]
    cos_r, sin_r, cos_c, sin_c = tab[0:16], tab[16:32], tab[32:48], tab[48:64]
    cos_p, sin_p = tab[64:72], tab[72:80]

    def head_norm(x3, g_ref):
        ms = jnp.mean(x3 * x3, axis=1, keepdims=True)
        return (x3 * lax.rsqrt(ms + NORM_EPS)) * g_ref[...][None]

    def axial(x3):
        r1, r2, c1, c2 = x3[:, 0:16], x3[:, 16:32], x3[:, 32:48], x3[:, 48:64]
        return jnp.concatenate([r1 * cos_r - r2 * sin_r, r1 * sin_r + r2 * cos_r,
                                c1 * cos_c - c2 * sin_c, c1 * sin_c + c2 * cos_c], axis=1)

    def partial(x3):
        half = PARTIAL_ROT // 2
        x1, x2, rest = x3[:, 0:half], x3[:, half:PARTIAL_ROT], x3[:, PARTIAL_ROT:]
        return jnp.concatenate([x1 * cos_p - x2 * sin_p, x1 * sin_p + x2 * cos_p, rest], axis=1)

    qa = proj_t(_QA0, A_Q).reshape(A_HEADS, HEAD_DIM, ts)
    qa = axial(head_norm(qa, gq_ref)) * QK_SCALE
    qTa_ref[...] = qa.reshape(A_Q, ts).astype(_BF16)

    ka = proj_t(_KA0, A_KV).reshape(A_KV_HEADS, HEAD_DIM, ts)
    ka = axial(head_norm(ka, gk_ref)).reshape(A_KV, ts)
    kA_ref[...] = ka.T.astype(_BF16)

    va = proj_t(_VA0, A_KV).astype(_BF16)
    for g in range(A_KV_HEADS):
        for c in range(ts // tk):
            vTa_ref[g, c] = va[g * HEAD_DIM:(g + 1) * HEAD_DIM, c * tk:(c + 1) * tk]

    qb = proj_t(_QB0, B_QK).reshape(2 * B_HEADS, HEAD_DIM, ts)
    qTb_ref[...] = (partial(qb) * QK_SCALE).reshape(B_QK, ts).astype(_BF16)

    kb = proj_t(_KB0, B_QK).reshape(2 * B_HEADS, HEAD_DIM, ts)
    kB_ref[...] = partial(kb).reshape(B_QK, ts).T.astype(_BF16)

    vb = proj_t(_VB0, B_V).astype(_BF16)
    for hd in range(B_HEADS):
        for c in range(ts // tk):
            vTb_ref[hd, c] = vb[hd * B_V_DIM:(hd + 1) * B_V_DIM, c * tk:(c + 1) * tk]


def _inproj(x, mod, gmix, w_inT, gq, gk, tab, *, ts, tk):
    n, s, d = x.shape
    nck = s // tk
    cpt = ts // tk
    out_shape = (
        jax.ShapeDtypeStruct((n, A_Q, s), _BF16),
        jax.ShapeDtypeStruct((n, s, A_KV), _BF16),
        jax.ShapeDtypeStruct((n, A_KV_HEADS, nck, HEAD_DIM, tk), _BF16),
        jax.ShapeDtypeStruct((n, B_QK, s), _BF16),
        jax.ShapeDtypeStruct((n, s, B_QK), _BF16),
        jax.ShapeDtypeStruct((n, B_HEADS, nck, B_V_DIM, tk), _BF16),
    )
    return pl.pallas_call(
        functools.partial(_inproj_kernel, tk=tk),
        grid=(n, s // ts),
        in_specs=[
            pl.BlockSpec((None, ts, d), lambda b, t: (b, t, 0)),
            pl.BlockSpec((None, 6, d), lambda b, t: (b, 0, 0)),
            pl.BlockSpec((1, d), lambda b, t: (0, 0)),
            pl.BlockSpec((IN_WIDTH, d), lambda b, t: (0, 0)),
            pl.BlockSpec((HEAD_DIM, 1), lambda b, t: (0, 0)),
            pl.BlockSpec((HEAD_DIM, 1), lambda b, t: (0, 0)),
            pl.BlockSpec((80, ts), lambda b, t: (0, t)),
        ],
        out_specs=(
            pl.BlockSpec((None, A_Q, ts), lambda b, t: (b, 0, t)),
            pl.BlockSpec((None, ts, A_KV), lambda b, t: (b, t, 0)),
            pl.BlockSpec((None, A_KV_HEADS, cpt, HEAD_DIM, tk), lambda b, t: (b, 0, t, 0, 0)),
            pl.BlockSpec((None, B_QK, ts), lambda b, t: (b, 0, t)),
            pl.BlockSpec((None, ts, B_QK), lambda b, t: (b, t, 0)),
            pl.BlockSpec((None, B_HEADS, cpt, B_V_DIM, tk), lambda b, t: (b, 0, t, 0, 0)),
        ),
        out_shape=out_shape,
        compiler_params=_params(("parallel", "parallel")),
        name="mix_in_projection",
    )(x, mod, gmix, w_inT, gq, gk, tab)


def _online_softmax_step(s, vc, m_ref, l_ref, acc_ref, idx):
    m_prev = m_ref[idx]
    m_new = jnp.maximum(m_prev, jnp.max(s, axis=0, keepdims=True))
    alpha = jnp.exp2(m_prev - m_new)
    p = jnp.exp2(s - m_new)
    l_ref[idx] = alpha * l_ref[idx] + jnp.sum(p, axis=0, keepdims=True)
    acc_ref[idx] = alpha * acc_ref[idx] + _dot(vc, p.astype(_BF16))
    m_ref[idx] = m_new


def _padded_queries(qT_ref, w_ref, nchains, rows_of_chain):
    tq = qT_ref.shape[-1]
    row = lax.broadcasted_iota(jnp.int32, (2 * HEAD_DIM, tq), 0)
    for c in range(nchains):
        q = qT_ref[c * HEAD_DIM:(c + 1) * HEAD_DIM, :].astype(_F32)
        qq = jnp.concatenate([q, q], axis=0)
        lo = rows_of_chain(c)
        w_ref[c] = jnp.where((row >= lo) & (row < lo + HEAD_DIM), qq, 0.0).astype(_BF16)


def _flash_loop(k_ref, vT_ref, w_ref, s_even_ref, s_odd_ref, m_ref, l_ref, acc_ref, *, nchains, tk):
    nck = vT_ref.shape[0]
    m_ref[...] = jnp.full(m_ref.shape, -jnp.inf, _F32)
    l_ref[...] = jnp.zeros(l_ref.shape, _F32)
    acc_ref[...] = jnp.zeros(acc_ref.shape, _F32)

    def k_chunk(j):
        return k_ref[pl.ds(pl.multiple_of(j * tk, tk), tk), :]

    def phase(j_cur, j_next, s_cur_ref, s_next_ref):
        kn = k_chunk(j_next)
        vc = vT_ref[j_cur]
        for c in range(nchains):
            s = s_cur_ref[c]
            s_next_ref[c] = _dot(kn, w_ref[c])
            _online_softmax_step(s, vc, m_ref, l_ref, acc_ref, c)

    k0 = k_chunk(0)
    for c in range(nchains):
        s_even_ref[c] = _dot(k0, w_ref[c])

    def body(i, carry):
        j = 2 * i
        phase(j, j + 1, s_even_ref, s_odd_ref)
        phase(j + 1, jnp.where(j + 2 == nck, 0, j + 2), s_odd_ref, s_even_ref)
        return carry

    lax.fori_loop(0, nck // 2, body, 0)


def _attn_a_kernel(qT_ref, k_ref, vT_ref, o_ref, w_ref, s_even_ref, s_odd_ref, m_ref, l_ref, acc_ref, *, tk):
    g = pl.program_id(1)
    _padded_queries(qT_ref, w_ref, A_GROUP, lambda c: g * HEAD_DIM)
    _flash_loop(k_ref, vT_ref, w_ref, s_even_ref, s_odd_ref, m_ref, l_ref, acc_ref, nchains=A_GROUP, tk=tk)
    for c in range(A_GROUP):
        o_ref[c * HEAD_DIM:(c + 1) * HEAD_DIM, :] = (acc_ref[c] / l_ref[c]).astype(_BF16)


def _attn_a(qTa, kA, vTa, *, tq):
    n, _, s = qTa.shape
    _, _, nck, _, tk = vTa.shape
    return pl.pallas_call(
        functools.partial(_attn_a_kernel, tk=tk),
        grid=(n, A_KV_HEADS, s // tq),
        in_specs=[
            pl.BlockSpec((None, A_GROUP * HEAD_DIM, tq), lambda b, g, i: (b, g, i)),
            pl.BlockSpec((None, s, A_KV), lambda b, g, i: (b, 0, 0)),
            pl.BlockSpec((None, None, nck, HEAD_DIM, tk), lambda b, g, i: (b, g, 0, 0, 0)),
        ],
        out_specs=pl.BlockSpec((None, A_GROUP * HEAD_DIM, tq), lambda b, g, i: (b, g, i)),
        out_shape=jax.ShapeDtypeStruct((n, A_Q, s), _BF16),
        scratch_shapes=[
            pltpu.VMEM((A_GROUP, 2 * HEAD_DIM, tq), _BF16),
            pltpu.VMEM((A_GROUP, tk, tq), _F32),
            pltpu.VMEM((A_GROUP, tk, tq), _F32),
            pltpu.VMEM((A_GROUP, 1, tq), _F32),
            pltpu.VMEM((A_GROUP, 1, tq), _F32),
            pltpu.VMEM((A_GROUP, HEAD_DIM, tq), _F32),
        ],
        compiler_params=_params(("parallel", "parallel", "parallel")),
        name="gqa_attention",
    )(qTa, kA, vTa)


def _attn_b_kernel(qT_ref, k_ref, vT_ref, lam_ref, gsub_ref, o_ref, w_ref, s_even_ref, s_odd_ref,
                   m_ref, l_ref, acc_ref, *, tk, lam_init):
    _padded_queries(qT_ref, w_ref, 2, lambda c: c * HEAD_DIM)
    _flash_loop(k_ref, vT_ref, w_ref, s_even_ref, s_odd_ref, m_ref, l_ref, acc_ref, nchains=2, tk=tk)

    lp = lam_ref[...]
    lam = (jnp.exp(jnp.sum(lp[0:1] * lp[1:2], axis=1, keepdims=True))
           - jnp.exp(jnp.sum(lp[2:3] * lp[3:4], axis=1, keepdims=True)) + lam_init)
    o = acc_ref[0] / l_ref[0] - lam * (acc_ref[1] / l_ref[1])
    ms = jnp.mean(o * o, axis=0, keepdims=True)
    o = ((o * lax.rsqrt(ms + NORM_EPS)) * gsub_ref[...]) * (1.0 - lam_init)
    o_ref[...] = o.astype(_BF16)


def _attn_b(qTb, kB, vTb, lam_params, gsub, *, tq, lam_init):
    n, _, s = qTb.shape
    _, _, nck, _, tk = vTb.shape
    return pl.pallas_call(
        functools.partial(_attn_b_kernel, tk=tk, lam_init=lam_init),
        grid=(n, B_HEADS, s // tq),
        in_specs=[
            pl.BlockSpec((None, 2 * HEAD_DIM, tq), lambda b, hd, i: (b, hd, i)),
            pl.BlockSpec((None, s, 2 * HEAD_DIM), lambda b, hd, i: (b, 0, hd)),
            pl.BlockSpec((None, None, nck, B_V_DIM, tk), lambda b, hd, i: (b, hd, 0, 0, 0)),
            pl.BlockSpec((4, HEAD_DIM), lambda b, hd, i: (0, 0)),
            pl.BlockSpec((B_V_DIM, 1), lambda b, hd, i: (0, 0)),
        ],
        out_specs=pl.BlockSpec((None, B_V_DIM, tq), lambda b, hd, i: (b, hd, i)),
        out_shape=jax.ShapeDtypeStruct((n, B_V, s), _BF16),
        scratch_shapes=[
            pltpu.VMEM((2, 2 * HEAD_DIM, tq), _BF16),
            pltpu.VMEM((2, tk, tq), _F32),
            pltpu.VMEM((2, tk, tq), _F32),
            pltpu.VMEM((2, 1, tq), _F32),
            pltpu.VMEM((2, 1, tq), _F32),
            pltpu.VMEM((2, B_V_DIM, tq), _F32),
        ],
        compiler_params=_params(("parallel", "parallel", "parallel")),
        name="diff_attention",
    )(qTb, kB, vTb, lam_params, gsub)


def _ffn_kernel(x_ref, oTa_ref, oTb_ref, mod_ref, wout_ref, gffn_ref, wgu_ref, wdn_ref, y_ref, act_ref):
    mixed = _dot_tn(oTa_ref[...], wout_ref[0:A_Q, :]) + _dot_tn(oTb_ref[...], wout_ref[A_Q:, :])
    gt1, sh2, sc2, gt2 = mod_ref[2:3, :], mod_ref[3:4, :], mod_ref[4:5, :], mod_ref[5:6, :]
    x1 = x_ref[...] + gt1 * mixed
    h = ((x1 * _rms_rows(x1)) * gffn_ref[...]) * (1.0 + sc2) + sh2
    hb = h.astype(_BF16)
    for c in range(wgu_ref.shape[0]):
        gu = _dot(hb, wgu_ref[c])
        gate, up = gu[:, :FF_CHUNK], gu[:, FF_CHUNK:]
        act_ref[:, c * FF_CHUNK:(c + 1) * FF_CHUNK] = ((gate * jax.nn.sigmoid(gate)) * up).astype(_BF16)
    y_ref[...] = x1 + gt2 * _dot(act_ref[...], wdn_ref[...])


def _resident(shape):
    return pl.BlockSpec(shape, lambda b, t: (0,) * len(shape), pipeline_mode=pl.Buffered(1))


def _out_ffn(x, oTa, oTb, mod, w_out, gffn, w_gu, w_dn, *, ts):
    n, s, d = x.shape
    nff, _, _ = w_gu.shape
    dff = w_dn.shape[0]
    return pl.pallas_call(
        _ffn_kernel,
        grid=(n, s // ts),
        in_specs=[
            pl.BlockSpec((None, ts, d), lambda b, t: (b, t, 0)),
            pl.BlockSpec((None, A_Q, ts), lambda b, t: (b, 0, t)),
            pl.BlockSpec((None, B_V, ts), lambda b, t: (b, 0, t)),
            pl.BlockSpec((None, 6, d), lambda b, t: (b, 0, 0)),
            _resident((A_Q + B_V, d)),
            pl.BlockSpec((1, d), lambda b, t: (0, 0)),
            _resident((nff, d, 2 * FF_CHUNK)),
            _resident((dff, d)),
        ],
        out_specs=pl.BlockSpec((None, ts, d), lambda b, t: (b, t, 0)),
        out_shape=jax.ShapeDtypeStruct((n, s, d), _F32),
        scratch_shapes=[pltpu.VMEM((ts, dff), _BF16)],
        compiler_params=_params(("parallel", "parallel")),
        name="out_proj_swiglu",
    )(x, oTa, oTb, mod, w_out, gffn, w_gu, w_dn)


def _final_kernel(x_ref, mod_ref, g_ref, y_ref):
    x = x_ref[...]
    shift, scale = mod_ref[0:1, :], mod_ref[1:2, :]
    y_ref[...] = ((x * _rms_rows(x)) * g_ref[...]) * (1.0 + scale) + shift


def _final_norm(x, modf, g, *, ts):
    n, s, d = x.shape
    return pl.pallas_call(
        _final_kernel,
        grid=(n, s // ts),
        in_specs=[
            pl.BlockSpec((None, ts, d), lambda b, t: (b, t, 0)),
            pl.BlockSpec((None, 2, d), lambda b, t: (b, 0, 0)),
            pl.BlockSpec((1, d), lambda b, t: (0, 0)),
        ],
        out_specs=pl.BlockSpec((None, ts, d), lambda b, t: (b, t, 0)),
        out_shape=jax.ShapeDtypeStruct((n, s, d), _F32),
        compiler_params=_params(("parallel", "parallel")),
        name="final_norm",
    )(x, modf, g)


def _rope_table_t(s):
    def tables(pos, dim, theta):
        inv = 1.0 / (theta ** (jnp.arange(0, dim, 2, dtype=_F32) / dim))
        ang = pos.astype(_F32)[:, None] * inv[None, :]
        return jnp.cos(ang).T, jnp.sin(ang).T

    t = jnp.arange(s)
    cos_r, sin_r = tables(t // GRID_W, HEAD_DIM // 2, AXIAL_THETA)
    cos_c, sin_c = tables(t % GRID_W, HEAD_DIM // 2, AXIAL_THETA)
    cos_p, sin_p = tables(t, PARTIAL_ROT, ROPE_THETA)
    return jnp.concatenate([cos_r, sin_r, cos_c, sin_c, cos_p, sin_p], axis=0)


def _trunk(x, c, w_ada, b_ada, g_mix, w_in, g_q_a, g_k_a, lam_q1, lam_k1, lam_q2, lam_k2, g_subln,
           w_out, g_ffn, w_gate_up, w_down, w_ada_final, b_ada_final, g_final, *, ts, tq, tk):
    n, s, d = x.shape
    depth = w_ada.shape[0]
    dff = w_down.shape[1]
    nff = dff // FF_CHUNK

    c_pad = jnp.zeros((8, d), _F32).at[:n].set(c)
    mod = _modulation(c_pad, w_ada, b_ada[:, None, :])[:, :n].reshape(depth, n, 6, d)
    modf = _modulation(c_pad, w_ada_final[None], b_ada_final[None, None, :])[0, :n].reshape(n, 2, d)

    tab = _rope_table_t(s)
    w_inT = jnp.swapaxes(w_in, 1, 2).astype(_BF16)
    w_out_b = w_out.astype(_BF16)
    w_gu = jnp.concatenate([w_gate_up[:, :, :dff].reshape(depth, d, nff, FF_CHUNK),
                            w_gate_up[:, :, dff:].reshape(depth, d, nff, FF_CHUNK)], axis=-1)
    w_gu = jnp.transpose(w_gu, (0, 2, 1, 3)).astype(_BF16)
    w_dn = w_down.astype(_BF16)
    lam_params = jnp.stack([lam_q1, lam_k1, lam_q2, lam_k2], axis=1)

    for l in range(depth):
        qTa, kA, vTa, qTb, kB, vTb = _inproj(
            x, mod[l], g_mix[l][None, :], w_inT[l], g_q_a[l][:, None], g_k_a[l][:, None], tab, ts=ts, tk=tk)
        oTa = _attn_a(qTa, kA, vTa, tq=tq)
        lam_init = 0.8 - 0.6 * math.exp(-0.3 * l)
        oTb = _attn_b(qTb, kB, vTb, lam_params[l], g_subln[l][:, None], tq=tq, lam_init=lam_init)
        x = _out_ffn(x, oTa, oTb, mod[l], w_out_b[l], g_ffn[l][None, :], w_gu[l], w_dn[l], ts=ts)
    return _final_norm(x, modf, g_final[None, :], ts=ts)


def kernel(x_prompt, x_sample, c_prompt, c_sample, w_ada, b_ada, g_mix, w_in, g_q_a, g_k_a, lam_q1, lam_k1, lam_q2, lam_k2, g_subln, w_out, g_ffn, w_gate_up, w_down, w_ada_final, b_ada_final, g_final):
    nb = x_prompt.shape[0]
    x = jnp.concatenate([x_prompt, x_sample], axis=0)
    c = jnp.concatenate([c_prompt, c_sample], axis=0)
    y = _trunk(x, c, w_ada, b_ada, g_mix, w_in, g_q_a, g_k_a, lam_q1, lam_k1, lam_q2, lam_k2, g_subln,
               w_out, g_ffn, w_gate_up, w_down, w_ada_final, b_ada_final, g_final, ts=512, tq=256, tk=512)
    return (y[:nb], y[nb:])
```

```python
import functools
import math

import jax
import jax.numpy as jnp
from jax import lax
from jax.experimental import pallas as pl
from jax.experimental.pallas import tpu as pltpu

HEAD_DIM = 64
A_HEADS = 8
A_KV_HEADS = 2
A_GROUP = A_HEADS // A_KV_HEADS
B_HEADS = 4
B_V_DIM = 2 * HEAD_DIM
A_Q = A_HEADS * HEAD_DIM
A_KV = A_KV_HEADS * HEAD_DIM
B_QK = B_HEADS * 2 * HEAD_DIM
B_V = B_HEADS * B_V_DIM
IN_WIDTH = A_Q + 2 * A_KV + 2 * B_QK + B_V
GRID_W = 64
AXIAL_THETA = 10000.0
ROPE_THETA = 500000.0
PARTIAL_ROT = HEAD_DIM // 4
NORM_EPS = 1e-6
QK_SCALE = HEAD_DIM ** -0.5 * math.log2(math.e)

_QA0, _KA0, _VA0 = 0, A_Q, A_Q + A_KV
_QB0 = A_Q + 2 * A_KV
_KB0 = _QB0 + B_QK
_VB0 = _KB0 + B_QK

ONES_ROWS = 16
_KNORM_B0, _KNORM_A0, _KNORM_ROWS = 0, 2 * B_HEADS, 16
FAST_SOFTMAX_MAX_BOUND = 48.0
FAST_UNROLL = 4
FAST_LOOKAHEAD = 4

FF_CHUNK = 256
V7X_VMEM_LIMIT_BYTES = 56 * 1024 * 1024

_BF16 = jnp.bfloat16
_F32 = jnp.float32


def _params(semantics, flags=None):
    return pltpu.CompilerParams(dimension_semantics=semantics, vmem_limit_bytes=V7X_VMEM_LIMIT_BYTES, flags=flags)


_ATTENTION_FLAGS = None


def _dot(a, b):
    return jnp.dot(a, b, preferred_element_type=_F32)


def _dot_nt(a, b):
    return lax.dot_general(a, b, (((1,), (1,)), ((), ())), preferred_element_type=_F32)


def _dot_tn(a, b):
    return lax.dot_general(a, b, (((0,), (0,)), ((), ())), preferred_element_type=_F32)


def _split_bf16(x):
    hi = x.astype(_BF16)
    lo = (x - hi.astype(_F32)).astype(_BF16)
    return hi, lo


def _mod_kernel(c_ref, w_ref, b_ref, o_ref):
    c = c_ref[...]
    c_act = c * jax.nn.sigmoid(c)
    c_hi, c_lo = _split_bf16(c_act)
    w_hi, w_lo = _split_bf16(w_ref[...])
    acc = _dot(c_hi, w_hi) + (_dot(c_lo, w_hi) + _dot(c_hi, w_lo))
    o_ref[...] = acc + b_ref[...]


def _modulation(c_pad, w, b, tn=1024):
    nl, d, n = w.shape
    return pl.pallas_call(
        _mod_kernel,
        grid=(nl, n // tn),
        in_specs=[
            pl.BlockSpec((8, d), lambda l, j: (0, 0)),
            pl.BlockSpec((None, d, tn), lambda l, j: (l, 0, j)),
            pl.BlockSpec((None, 1, tn), lambda l, j: (l, 0, j)),
        ],
        out_specs=pl.BlockSpec((None, 8, tn), lambda l, j: (l, 0, j)),
        out_shape=jax.ShapeDtypeStruct((nl, 8, n), _F32),
        compiler_params=_params(("parallel", "parallel")),
        name="adaln_modulation",
    )(c_pad, w, b)


def _rms_rows(x):
    return lax.rsqrt(jnp.mean(x * x, axis=1, keepdims=True) + NORM_EPS)


def _inproj_kernel(x_ref, mod_ref, gmix_ref, wT_ref, gq_ref, gk_ref, tab_ref,
                   qTa_ref, kA_ref, vTa_ref, qTb_ref, kB_ref, vTb_ref, knorm_ref, *, tk):
    ts = x_ref.shape[0]
    x = x_ref[...]
    sh1 = mod_ref[0:1, :]
    sc1 = mod_ref[1:2, :]
    h = ((x * _rms_rows(x)) * gmix_ref[...]) * (1.0 + sc1) + sh1
    hb = h.astype(_BF16)

    def proj_t(lo, n):
        return _dot_nt(wT_ref[lo:lo + n, :], hb)

    tab = tab_ref[...]
    cos_r, sin_r, cos_c, sin_c = tab[0:16], tab[16:32], tab[32:48], tab[48:64]
    cos_p, sin_p = tab[64:72], tab[72:80]

    def head_norm(x3, g_ref):
        ms = jnp.mean(x3 * x3, axis=1, keepdims=True)
        return (x3 * lax.rsqrt(ms + NORM_EPS)) * g_ref[...][None]

    def axial(x3):
        r1, r2, c1, c2 = x3[:, 0:16], x3[:, 16:32], x3[:, 32:48], x3[:, 48:64]
        return jnp.concatenate([r1 * cos_r - r2 * sin_r, r1 * sin_r + r2 * cos_r,
                                c1 * cos_c - c2 * sin_c, c1 * sin_c + c2 * cos_c], axis=1)

    def partial(x3):
        half = PARTIAL_ROT // 2
        x1, x2, rest = x3[:, 0:half], x3[:, half:PARTIAL_ROT], x3[:, PARTIAL_ROT:]
        return jnp.concatenate([x1 * cos_p - x2 * sin_p, x1 * sin_p + x2 * cos_p, rest], axis=1)

    qa = proj_t(_QA0, A_Q).reshape(A_HEADS, HEAD_DIM, ts)
    qa = axial(head_norm(qa, gq_ref)) * QK_SCALE
    qTa_ref[...] = qa.reshape(A_Q, ts).astype(_BF16)

    def max_key_norm2(k3):
        kf = k3.astype(_BF16).astype(_F32)
        return jnp.max(jnp.sum(kf * kf, axis=1), axis=1, keepdims=True)

    ones_rows = (lax.broadcasted_iota(jnp.int32, (ONES_ROWS, tk), 0) == 0).astype(_BF16)

    ka = axial(head_norm(proj_t(_KA0, A_KV).reshape(A_KV_HEADS, HEAD_DIM, ts), gk_ref))
    kA_ref[...] = ka.reshape(A_KV, ts).T.astype(_BF16)

    va = proj_t(_VA0, A_KV).astype(_BF16)
    for g in range(A_KV_HEADS):
        for c in range(ts // tk):
            vTa_ref[g, c] = jnp.concatenate(
                [va[g * HEAD_DIM:(g + 1) * HEAD_DIM, c * tk:(c + 1) * tk], ones_rows], axis=0)

    qb = proj_t(_QB0, B_QK).reshape(2 * B_HEADS, HEAD_DIM, ts)
    qTb_ref[...] = (partial(qb) * QK_SCALE).reshape(B_QK, ts).astype(_BF16)

    kb = partial(proj_t(_KB0, B_QK).reshape(2 * B_HEADS, HEAD_DIM, ts))
    kB_ref[...] = kb.reshape(B_QK, ts).T.astype(_BF16)

    vb = proj_t(_VB0, B_V).astype(_BF16)
    for hd in range(B_HEADS):
        for c in range(ts // tk):
            vTb_ref[hd, c] = jnp.concatenate(
                [vb[hd * B_V_DIM:(hd + 1) * B_V_DIM, c * tk:(c + 1) * tk], ones_rows], axis=0)

    @pl.when(pl.program_id(1) == 0)
    def _():
        knorm_ref[...] = jnp.zeros(knorm_ref.shape, _F32)

    lanes = knorm_ref.shape[1]
    b_rows = slice(_KNORM_B0, _KNORM_B0 + 2 * B_HEADS)
    a_rows = slice(_KNORM_A0, _KNORM_A0 + A_KV_HEADS)
    knorm_ref[b_rows, :] = jnp.maximum(knorm_ref[b_rows, :], jnp.broadcast_to(max_key_norm2(kb), (2 * B_HEADS, lanes)))
    knorm_ref[a_rows, :] = jnp.maximum(knorm_ref[a_rows, :], jnp.broadcast_to(max_key_norm2(ka), (A_KV_HEADS, lanes)))


def _inproj(x, mod, gmix, w_inT, gq, gk, tab, *, ts, tk):
    n, s, d = x.shape
    nck = s // tk
    cpt = ts // tk
    out_shape = (
        jax.ShapeDtypeStruct((n, A_Q, s), _BF16),
        jax.ShapeDtypeStruct((n, s, A_KV), _BF16),
        jax.ShapeDtypeStruct((n, A_KV_HEADS, nck, HEAD_DIM + ONES_ROWS, tk), _BF16),
        jax.ShapeDtypeStruct((n, B_QK, s), _BF16),
        jax.ShapeDtypeStruct((n, s, B_QK), _BF16),
        jax.ShapeDtypeStruct((n, B_HEADS, nck, B_V_DIM + ONES_ROWS, tk), _BF16),
        jax.ShapeDtypeStruct((n, _KNORM_ROWS, 128), _F32),
    )
    return pl.pallas_call(
        functools.partial(_inproj_kernel, tk=tk),
        grid=(n, s // ts),
        in_specs=[
            pl.BlockSpec((None, ts, d), lambda b, t: (b, t, 0)),
            pl.BlockSpec((None, 6, d), lambda b, t: (b, 0, 0)),
            pl.BlockSpec((1, d), lambda b, t: (0, 0)),
            pl.BlockSpec((IN_WIDTH, d), lambda b, t: (0, 0)),
            pl.BlockSpec((HEAD_DIM, 1), lambda b, t: (0, 0)),
            pl.BlockSpec((HEAD_DIM, 1), lambda b, t: (0, 0)),
            pl.BlockSpec((80, ts), lambda b, t: (0, t)),
        ],
        out_specs=(
            pl.BlockSpec((None, A_Q, ts), lambda b, t: (b, 0, t)),
            pl.BlockSpec((None, ts, A_KV), lambda b, t: (b, t, 0)),
            pl.BlockSpec((None, A_KV_HEADS, cpt, HEAD_DIM + ONES_ROWS, tk), lambda b, t: (b, 0, t, 0, 0)),
            pl.BlockSpec((None, B_QK, ts), lambda b, t: (b, 0, t)),
            pl.BlockSpec((None, ts, B_QK), lambda b, t: (b, t, 0)),
            pl.BlockSpec((None, B_HEADS, cpt, B_V_DIM + ONES_ROWS, tk), lambda b, t: (b, 0, t, 0, 0)),
            pl.BlockSpec((None, _KNORM_ROWS, 128), lambda b, t: (b, 0, 0)),
        ),
        out_shape=out_shape,
        compiler_params=_params(("parallel", "arbitrary")),
        name="mix_in_projection",
    )(x, mod, gmix, w_inT, gq, gk, tab)


def _prepare_queries(qT_ref, w_ref, bound_ref, nchains, rows_of_chain, knorm2_of_chain):
    tq = qT_ref.shape[-1]
    row = lax.broadcasted_iota(jnp.int32, (2 * HEAD_DIM, tq), 0)
    for c in range(nchains):
        q = qT_ref[c * HEAD_DIM:(c + 1) * HEAD_DIM, :].astype(_F32)
        qq = jnp.concatenate([q, q], axis=0)
        lo = rows_of_chain(c)
        w_ref[c] = jnp.where((row >= lo) & (row < lo + HEAD_DIM), qq, 0.0).astype(_BF16)
        bound_ref[c] = jnp.sqrt(jnp.sum(q * q, axis=0, keepdims=True) * knorm2_of_chain(c))


def _softmax_value_loop(k_ref, vT_ref, w_ref, bound_ref, s_even_ref, s_odd_ref, m_ref, acc_ref, *, nchains, tk):
    nck = vT_ref.shape[0]
    acc_ref[...] = jnp.zeros(acc_ref.shape, _F32)

    def k_chunk(j):
        return k_ref[pl.ds(pl.multiple_of(j * tk, tk), tk), :]

    use_bound = jnp.max(bound_ref[...]) <= FAST_SOFTMAX_MAX_BOUND

    @pl.when(use_bound)
    def _():
        def scores(j, step):
            dj, c = divmod(step, nchains)
            jj = j + dj
            if dj:
                jj = jnp.where(jj >= nck, jj - nck, jj)
            return _dot(k_chunk(jj), w_ref[c])

        for a in range(FAST_LOOKAHEAD):
            s_even_ref[a] = scores(0, a)

        def body(i, carry):
            j0 = i * FAST_UNROLL
            pending = [s_even_ref[a] for a in range(FAST_LOOKAHEAD)]
            for u in range(FAST_UNROLL):
                vc = vT_ref[j0 + u]
                for c in range(nchains):
                    pending.append(scores(j0, u * nchains + c + FAST_LOOKAHEAD))
                    p = jnp.exp2(pending.pop(0) - bound_ref[c]).astype(_BF16)
                    acc_ref[c] += _dot(vc, p)
            for a in range(FAST_LOOKAHEAD):
                s_even_ref[a] = pending[a]
            return carry

        lax.fori_loop(0, nck // FAST_UNROLL, body, 0)

    @pl.when(jnp.logical_not(use_bound))
    def _():
        m_ref[...] = jnp.full(m_ref.shape, -jnp.inf, _F32)

        def phase(j_cur, j_next, s_cur_ref, s_next_ref):
            kn = k_chunk(j_next)
            vc = vT_ref[j_cur]
            for c in range(nchains):
                s = s_cur_ref[c]
                s_next_ref[c] = _dot(kn, w_ref[c])
                m_prev = m_ref[c]
                m_new = jnp.maximum(m_prev, jnp.max(s, axis=0, keepdims=True))
                p = jnp.exp2(s - m_new).astype(_BF16)
                acc_ref[c] = jnp.exp2(m_prev - m_new) * acc_ref[c] + _dot(vc, p)
                m_ref[c] = m_new

        k0 = k_chunk(0)
        for c in range(nchains):
            s_even_ref[c] = _dot(k0, w_ref[c])

        def body(i, carry):
            j = 2 * i
            phase(j, j + 1, s_even_ref, s_odd_ref)
            phase(j + 1, jnp.where(j + 2 == nck, 0, j + 2), s_odd_ref, s_even_ref)
            return carry

        lax.fori_loop(0, nck // 2, body, 0)


def _attn_a_kernel(qT_ref, k_ref, vT_ref, knorm_ref, o_ref, w_ref, bound_ref, s_even_ref, s_odd_ref,
                   m_ref, acc_ref, *, tk):
    g = pl.program_id(1)
    knorm2 = knorm_ref[pl.ds(_KNORM_A0 + g, 1), 0:1]
    _prepare_queries(qT_ref, w_ref, bound_ref, A_GROUP, lambda c: g * HEAD_DIM, lambda c: knorm2)
    _softmax_value_loop(k_ref, vT_ref, w_ref, bound_ref, s_even_ref, s_odd_ref, m_ref, acc_ref,
                        nchains=A_GROUP, tk=tk)
    for c in range(A_GROUP):
        o = acc_ref[c, 0:HEAD_DIM, :] / acc_ref[c, HEAD_DIM:HEAD_DIM + 1, :]
        o_ref[c * HEAD_DIM:(c + 1) * HEAD_DIM, :] = o.astype(_BF16)


def _attention_scratch(nchains, dv, tq, tk):
    return [
        pltpu.VMEM((nchains, 2 * HEAD_DIM, tq), _BF16),
        pltpu.VMEM((nchains, 1, tq), _F32),
        pltpu.VMEM((max(nchains, FAST_LOOKAHEAD), tk, tq), _F32),
        pltpu.VMEM((nchains, tk, tq), _F32),
        pltpu.VMEM((nchains, 1, tq), _F32),
        pltpu.VMEM((nchains, dv + ONES_ROWS, tq), _F32),
    ]


def _attn_a(qTa, kA, vTa, knorm, *, tq):
    n, _, s = qTa.shape
    _, _, nck, dv1, tk = vTa.shape
    return pl.pallas_call(
        functools.partial(_attn_a_kernel, tk=tk),
        grid=(n, A_KV_HEADS, s // tq),
        in_specs=[
            pl.BlockSpec((None, A_GROUP * HEAD_DIM, tq), lambda b, g, i: (b, g, i)),
            pl.BlockSpec((None, s, A_KV), lambda b, g, i: (b, 0, 0)),
            pl.BlockSpec((None, None, nck, dv1, tk), lambda b, g, i: (b, g, 0, 0, 0)),
            pl.BlockSpec((None, _KNORM_ROWS, 128), lambda b, g, i: (b, 0, 0)),
        ],
        out_specs=pl.BlockSpec((None, A_GROUP * HEAD_DIM, tq), lambda b, g, i: (b, g, i)),
        out_shape=jax.ShapeDtypeStruct((n, A_Q, s), _BF16),
        scratch_shapes=_attention_scratch(A_GROUP, HEAD_DIM, tq, tk),
        compiler_params=_params(("parallel", "parallel", "parallel"), _ATTENTION_FLAGS),
        name="gqa_attention",
    )(qTa, kA, vTa, knorm)


def _attn_b_kernel(qT_ref, k_ref, vT_ref, knorm_ref, lam_ref, gsub_ref, o_ref, w_ref, bound_ref,
                   s_even_ref, s_odd_ref, m_ref, acc_ref, *, tk, lam_init):
    hd = pl.program_id(1)
    knorm2 = knorm_ref[pl.ds(_KNORM_B0 + 2 * hd, 2), 0:1]
    _prepare_queries(qT_ref, w_ref, bound_ref, 2, lambda c: c * HEAD_DIM, lambda c: knorm2[c:c + 1])
    _softmax_value_loop(k_ref, vT_ref, w_ref, bound_ref, s_even_ref, s_odd_ref, m_ref, acc_ref,
                        nchains=2, tk=tk)

    lp = lam_ref[...]
    lam = (jnp.exp(jnp.sum(lp[0:1] * lp[1:2], axis=1, keepdims=True))
           - jnp.exp(jnp.sum(lp[2:3] * lp[3:4], axis=1, keepdims=True)) + lam_init)
    o = (acc_ref[0, 0:B_V_DIM, :] / acc_ref[0, B_V_DIM:B_V_DIM + 1, :]
         - lam * (acc_ref[1, 0:B_V_DIM, :] / acc_ref[1, B_V_DIM:B_V_DIM + 1, :]))
    ms = jnp.mean(o * o, axis=0, keepdims=True)
    o = ((o * lax.rsqrt(ms + NORM_EPS)) * gsub_ref[...]) * (1.0 - lam_init)
    o_ref[...] = o.astype(_BF16)


def _attn_b(qTb, kB, vTb, knorm, lam_params, gsub, *, tq, lam_init):
    n, _, s = qTb.shape
    _, _, nck, dv1, tk = vTb.shape
    return pl.pallas_call(
        functools.partial(_attn_b_kernel, tk=tk, lam_init=lam_init),
        grid=(n, B_HEADS, s // tq),
        in_specs=[
            pl.BlockSpec((None, 2 * HEAD_DIM, tq), lambda b, hd, i: (b, hd, i)),
            pl.BlockSpec((None, s, 2 * HEAD_DIM), lambda b, hd, i: (b, 0, hd)),
            pl.BlockSpec((None, None, nck, dv1, tk), lambda b, hd, i: (b, hd, 0, 0, 0)),
            pl.BlockSpec((None, _KNORM_ROWS, 128), lambda b, hd, i: (b, 0, 0)),
            pl.BlockSpec((4, HEAD_DIM), lambda b, hd, i: (0, 0)),
            pl.BlockSpec((B_V_DIM, 1), lambda b, hd, i: (0, 0)),
        ],
        out_specs=pl.BlockSpec((None, B_V_DIM, tq), lambda b, hd, i: (b, hd, i)),
        out_shape=jax.ShapeDtypeStruct((n, B_V, s), _BF16),
        scratch_shapes=_attention_scratch(2, B_V_DIM, tq, tk),
        compiler_params=_params(("parallel", "parallel", "parallel"), _ATTENTION_FLAGS),
        name="diff_attention",
    )(qTb, kB, vTb, knorm, lam_params, gsub)


def _ffn_kernel(x_ref, oTa_ref, oTb_ref, mod_ref, wout_ref, gffn_ref, wgu_ref, wdn_ref, y_ref, act_ref):
    mixed = _dot_tn(oTa_ref[...], wout_ref[0:A_Q, :]) + _dot_tn(oTb_ref[...], wout_ref[A_Q:, :])
    gt1, sh2, sc2, gt2 = mod_ref[2:3, :], mod_ref[3:4, :], mod_ref[4:5, :], mod_ref[5:6, :]
    x1 = x_ref[...] + gt1 * mixed
    h = ((x1 * _rms_rows(x1)) * gffn_ref[...]) * (1.0 + sc2) + sh2
    hb = h.astype(_BF16)
    for c in range(wgu_ref.shape[0]):
        gu = _dot(hb, wgu_ref[c])
        gate, up = gu[:, :FF_CHUNK], gu[:, FF_CHUNK:]
        act_ref[:, c * FF_CHUNK:(c + 1) * FF_CHUNK] = ((gate * jax.nn.sigmoid(gate)) * up).astype(_BF16)
    y_ref[...] = x1 + gt2 * _dot(act_ref[...], wdn_ref[...])


def _resident(shape):
    return pl.BlockSpec(shape, lambda b, t: (0,) * len(shape), pipeline_mode=pl.Buffered(1))


def _out_ffn(x, oTa, oTb, mod, w_out, gffn, w_gu, w_dn, *, ts):
    n, s, d = x.shape
    nff, _, _ = w_gu.shape
    dff = w_dn.shape[0]
    return pl.pallas_call(
        _ffn_kernel,
        grid=(n, s // ts),
        in_specs=[
            pl.BlockSpec((None, ts, d), lambda b, t: (b, t, 0)),
            pl.BlockSpec((None, A_Q, ts), lambda b, t: (b, 0, t)),
            pl.BlockSpec((None, B_V, ts), lambda b, t: (b, 0, t)),
            pl.BlockSpec((None, 6, d), lambda b, t: (b, 0, 0)),
            _resident((A_Q + B_V, d)),
            pl.BlockSpec((1, d), lambda b, t: (0, 0)),
            _resident((nff, d, 2 * FF_CHUNK)),
            _resident((dff, d)),
        ],
        out_specs=pl.BlockSpec((None, ts, d), lambda b, t: (b, t, 0)),
        out_shape=jax.ShapeDtypeStruct((n, s, d), _F32),
        scratch_shapes=[pltpu.VMEM((ts, dff), _BF16)],
        compiler_params=_params(("parallel", "parallel")),
        name="out_proj_swiglu",
    )(x, oTa, oTb, mod, w_out, gffn, w_gu, w_dn)


def _final_kernel(x_ref, mod_ref, g_ref, y_ref):
    x = x_ref[...]
    shift, scale = mod_ref[0:1, :], mod_ref[1:2, :]
    y_ref[...] = ((x * _rms_rows(x)) * g_ref[...]) * (1.0 + scale) + shift


def _final_norm(x, modf, g, *, ts):
    n, s, d = x.shape
    return pl.pallas_call(
        _final_kernel,
        grid=(n, s // ts),
        in_specs=[
            pl.BlockSpec((None, ts, d), lambda b, t: (b, t, 0)),
            pl.BlockSpec((None, 2, d), lambda b, t: (b, 0, 0)),
            pl.BlockSpec((1, d), lambda b, t: (0, 0)),
        ],
        out_specs=pl.BlockSpec((None, ts, d), lambda b, t: (b, t, 0)),
        out_shape=jax.ShapeDtypeStruct((n, s, d), _F32),
        compiler_params=_params(("parallel", "parallel")),
        name="final_norm",
    )(x, modf, g)


def _rope_table_t(s):
    def tables(pos, dim, theta):
        inv = 1.0 / (theta ** (jnp.arange(0, dim, 2, dtype=_F32) / dim))
        ang = pos.astype(_F32)[:, None] * inv[None, :]
        return jnp.cos(ang).T, jnp.sin(ang).T

    t = jnp.arange(s)
    cos_r, sin_r = tables(t // GRID_W, HEAD_DIM // 2, AXIAL_THETA)
    cos_c, sin_c = tables(t % GRID_W, HEAD_DIM // 2, AXIAL_THETA)
    cos_p, sin_p = tables(t, PARTIAL_ROT, ROPE_THETA)
    return jnp.concatenate([cos_r, sin_r, cos_c, sin_c, cos_p, sin_p], axis=0)


def _trunk(x, c, w_ada, b_ada, g_mix, w_in, g_q_a, g_k_a, lam_q1, lam_k1, lam_q2, lam_k2, g_subln,
           w_out, g_ffn, w_gate_up, w_down, w_ada_final, b_ada_final, g_final, *, ts, tq, tk):
    n, s, d = x.shape
    depth = w_ada.shape[0]
    dff = w_down.shape[1]
    nff = dff // FF_CHUNK

    c_pad = jnp.zeros((8, d), _F32).at[:n].set(c)
    mod = _modulation(c_pad, w_ada, b_ada[:, None, :])[:, :n].reshape(depth, n, 6, d)
    modf = _modulation(c_pad, w_ada_final[None], b_ada_final[None, None, :])[0, :n].reshape(n, 2, d)

    tab = _rope_table_t(s)
    w_inT = jnp.swapaxes(w_in, 1, 2).astype(_BF16)
    w_out_b = w_out.astype(_BF16)
    w_gu = jnp.concatenate([w_gate_up[:, :, :dff].reshape(depth, d, nff, FF_CHUNK),
                            w_gate_up[:, :, dff:].reshape(depth, d, nff, FF_CHUNK)], axis=-1)
    w_gu = jnp.transpose(w_gu, (0, 2, 1, 3)).astype(_BF16)
    w_dn = w_down.astype(_BF16)
    lam_params = jnp.stack([lam_q1, lam_k1, lam_q2, lam_k2], axis=1)

    for l in range(depth):
        qTa, kA, vTa, qTb, kB, vTb, knorm = _inproj(
            x, mod[l], g_mix[l][None, :], w_inT[l], g_q_a[l][:, None], g_k_a[l][:, None], tab, ts=ts, tk=tk)
        oTa = _attn_a(qTa, kA, vTa, knorm, tq=tq)
        lam_init = 0.8 - 0.6 * math.exp(-0.3 * l)
        oTb = _attn_b(qTb, kB, vTb, knorm, lam_params[l], g_subln[l][:, None], tq=tq, lam_init=lam_init)
        x = _out_ffn(x, oTa, oTb, mod[l], w_out_b[l], g_ffn[l][None, :], w_gu[l], w_dn[l], ts=ts)
    return _final_norm(x, modf, g_final[None, :], ts=ts)


def kernel(x_prompt, x_sample, c_prompt, c_sample, w_ada, b_ada, g_mix, w_in, g_q_a, g_k_a, lam_q1, lam_k1, lam_q2, lam_k2, g_subln, w_out, g_ffn, w_gate_up, w_down, w_ada_final, b_ada_final, g_final):
    nb = x_prompt.shape[0]
    x = jnp.concatenate([x_prompt, x_sample], axis=0)
    c = jnp.concatenate([c_prompt, c_sample], axis=0)
    y = _trunk(x, c, w_ada, b_ada, g_mix, w_in, g_q_a, g_k_a, lam_q1, lam_k1, lam_q2, lam_k2, g_subln,
               w_out, g_ffn, w_gate_up, w_down, w_ada_final, b_ada_final, g_final, ts=512, tq=256, tk=256)
    return (y[:nb], y[nb:])
```

```python
import functools
import math

import jax
import jax.numpy as jnp
from jax import lax
from jax.experimental import pallas as pl
from jax.experimental.pallas import tpu as pltpu

HEAD_DIM = 64
A_HEADS = 8
A_KV_HEADS = 2
A_GROUP = A_HEADS // A_KV_HEADS
B_HEADS = 4
B_V_DIM = 2 * HEAD_DIM
A_Q = A_HEADS * HEAD_DIM
A_KV = A_KV_HEADS * HEAD_DIM
B_QK = B_HEADS * 2 * HEAD_DIM
B_V = B_HEADS * B_V_DIM
IN_WIDTH = A_Q + 2 * A_KV + 2 * B_QK + B_V
GRID_W = 64
AXIAL_THETA = 10000.0
ROPE_THETA = 500000.0
PARTIAL_ROT = HEAD_DIM // 4
NORM_EPS = 1e-6
QK_SCALE = HEAD_DIM ** -0.5 * math.log2(math.e)

_QA0, _KA0, _VA0 = 0, A_Q, A_Q + A_KV
_QB0 = A_Q + 2 * A_KV
_KB0 = _QB0 + B_QK
_VB0 = _KB0 + B_QK

ONES_ROWS = 16
_KNORM_B0, _KNORM_A0, _KNORM_ROWS = 0, 2 * B_HEADS, 16
FAST_SOFTMAX_MAX_BOUND = 48.0
FAST_STEPS_PER_TRIP = 64
FAST_LOOKAHEAD = 4

FF_CHUNK = 256
V7X_VMEM_LIMIT_BYTES = 56 * 1024 * 1024

_BF16 = jnp.bfloat16
_F32 = jnp.float32


def _params(semantics, flags=None):
    return pltpu.CompilerParams(dimension_semantics=semantics, vmem_limit_bytes=V7X_VMEM_LIMIT_BYTES, flags=flags)


_ATTENTION_FLAGS = None


def _dot(a, b):
    return jnp.dot(a, b, preferred_element_type=_F32)


def _dot_nt(a, b):
    return lax.dot_general(a, b, (((1,), (1,)), ((), ())), preferred_element_type=_F32)


def _dot_tn(a, b):
    return lax.dot_general(a, b, (((0,), (0,)), ((), ())), preferred_element_type=_F32)


def _split_bf16(x):
    hi = x.astype(_BF16)
    lo = (x - hi.astype(_F32)).astype(_BF16)
    return hi, lo


def _mod_kernel(c_ref, w_ref, b_ref, o_ref):
    c = c_ref[...]
    c_act = c * jax.nn.sigmoid(c)
    c_hi, c_lo = _split_bf16(c_act)
    w_hi, w_lo = _split_bf16(w_ref[...])
    acc = _dot(c_hi, w_hi) + (_dot(c_lo, w_hi) + _dot(c_hi, w_lo))
    o_ref[...] = acc + b_ref[...]


def _modulation(c_pad, w, b, tn=1024):
    nl, d, n = w.shape
    return pl.pallas_call(
        _mod_kernel,
        grid=(nl, n // tn),
        in_specs=[
            pl.BlockSpec((8, d), lambda l, j: (0, 0)),
            pl.BlockSpec((None, d, tn), lambda l, j: (l, 0, j)),
            pl.BlockSpec((None, 1, tn), lambda l, j: (l, 0, j)),
        ],
        out_specs=pl.BlockSpec((None, 8, tn), lambda l, j: (l, 0, j)),
        out_shape=jax.ShapeDtypeStruct((nl, 8, n), _F32),
        compiler_params=_params(("parallel", "parallel")),
        name="adaln_modulation",
    )(c_pad, w, b)


def _rms_rows(x):
    return lax.rsqrt(jnp.mean(x * x, axis=1, keepdims=True) + NORM_EPS)


def _inproj_kernel(x_ref, mod_ref, gmix_ref, wT_ref, gq_ref, gk_ref, tab_ref,
                   qTa_ref, kA_ref, vTa_ref, qTb_ref, kB_ref, vTb_ref, knorm_ref, *, tk):
    ts = x_ref.shape[0]
    x = x_ref[...]
    sh1 = mod_ref[0:1, :]
    sc1 = mod_ref[1:2, :]
    h = ((x * _rms_rows(x)) * gmix_ref[...]) * (1.0 + sc1) + sh1
    hb = h.astype(_BF16)

    def proj_t(lo, n):
        return _dot_nt(wT_ref[lo:lo + n, :], hb)

    tab = tab_ref[...]
    cos_r, sin_r, cos_c, sin_c = tab[0:16], tab[16:32], tab[32:48], tab[48:64]
    cos_p, sin_p = tab[64:72], tab[72:80]

    def head_norm(x3, g_ref):
        ms = jnp.mean(x3 * x3, axis=1, keepdims=True)
        return (x3 * lax.rsqrt(ms + NORM_EPS)) * g_ref[...][None]

    def axial(x3):
        r1, r2, c1, c2 = x3[:, 0:16], x3[:, 16:32], x3[:, 32:48], x3[:, 48:64]
        return jnp.concatenate([r1 * cos_r - r2 * sin_r, r1 * sin_r + r2 * cos_r,
                                c1 * cos_c - c2 * sin_c, c1 * sin_c + c2 * cos_c], axis=1)

    def partial(x3):
        half = PARTIAL_ROT // 2
        x1, x2, rest = x3[:, 0:half], x3[:, half:PARTIAL_ROT], x3[:, PARTIAL_ROT:]
        return jnp.concatenate([x1 * cos_p - x2 * sin_p, x1 * sin_p + x2 * cos_p, rest], axis=1)

    qa = proj_t(_QA0, A_Q).reshape(A_HEADS, HEAD_DIM, ts)
    qa = axial(head_norm(qa, gq_ref)) * QK_SCALE
    qTa_ref[...] = qa.reshape(A_Q, ts).astype(_BF16)

    def max_key_norm2(k3):
        kf = k3.astype(_BF16).astype(_F32)
        return jnp.max(jnp.sum(kf * kf, axis=1), axis=1, keepdims=True)

    ones_rows = (lax.broadcasted_iota(jnp.int32, (ONES_ROWS, tk), 0) == 0).astype(_BF16)

    ka = axial(head_norm(proj_t(_KA0, A_KV).reshape(A_KV_HEADS, HEAD_DIM, ts), gk_ref))
    kA_ref[...] = ka.reshape(A_KV, ts).T.astype(_BF16)

    va = proj_t(_VA0, A_KV).astype(_BF16)
    for g in range(A_KV_HEADS):
        for c in range(ts // tk):
            vTa_ref[g, c] = jnp.concatenate(
                [va[g * HEAD_DIM:(g + 1) * HEAD_DIM, c * tk:(c + 1) * tk], ones_rows], axis=0)

    qb = proj_t(_QB0, B_QK).reshape(2 * B_HEADS, HEAD_DIM, ts)
    qTb_ref[...] = (partial(qb) * QK_SCALE).reshape(B_QK, ts).astype(_BF16)

    kb = partial(proj_t(_KB0, B_QK).reshape(2 * B_HEADS, HEAD_DIM, ts))
    kB_ref[...] = kb.reshape(B_QK, ts).T.astype(_BF16)

    vb = proj_t(_VB0, B_V).astype(_BF16)
    for hd in range(B_HEADS):
        for c in range(ts // tk):
            vTb_ref[hd, c] = jnp.concatenate(
                [vb[hd * B_V_DIM:(hd + 1) * B_V_DIM, c * tk:(c + 1) * tk], ones_rows], axis=0)

    @pl.when(pl.program_id(1) == 0)
    def _():
        knorm_ref[...] = jnp.zeros(knorm_ref.shape, _F32)

    lanes = knorm_ref.shape[1]
    b_rows = slice(_KNORM_B0, _KNORM_B0 + 2 * B_HEADS)
    a_rows = slice(_KNORM_A0, _KNORM_A0 + A_KV_HEADS)
    knorm_ref[b_rows, :] = jnp.maximum(knorm_ref[b_rows, :], jnp.broadcast_to(max_key_norm2(kb), (2 * B_HEADS, lanes)))
    knorm_ref[a_rows, :] = jnp.maximum(knorm_ref[a_rows, :], jnp.broadcast_to(max_key_norm2(ka), (A_KV_HEADS, lanes)))


def _inproj(x, mod, gmix, w_inT, gq, gk, tab, *, ts, tk):
    n, s, d = x.shape
    nck = s // tk
    cpt = ts // tk
    out_shape = (
        jax.ShapeDtypeStruct((n, A_Q, s), _BF16),
        jax.ShapeDtypeStruct((n, s, A_KV), _BF16),
        jax.ShapeDtypeStruct((n, A_KV_HEADS, nck, HEAD_DIM + ONES_ROWS, tk), _BF16),
        jax.ShapeDtypeStruct((n, B_QK, s), _BF16),
        jax.ShapeDtypeStruct((n, s, B_QK), _BF16),
        jax.ShapeDtypeStruct((n, B_HEADS, nck, B_V_DIM + ONES_ROWS, tk), _BF16),
        jax.ShapeDtypeStruct((n, _KNORM_ROWS, 128), _F32),
    )
    return pl.pallas_call(
        functools.partial(_inproj_kernel, tk=tk),
        grid=(n, s // ts),
        in_specs=[
            pl.BlockSpec((None, ts, d), lambda b, t: (b, t, 0)),
            pl.BlockSpec((None, 6, d), lambda b, t: (b, 0, 0)),
            pl.BlockSpec((1, d), lambda b, t: (0, 0)),
            pl.BlockSpec((IN_WIDTH, d), lambda b, t: (0, 0)),
            pl.BlockSpec((HEAD_DIM, 1), lambda b, t: (0, 0)),
            pl.BlockSpec((HEAD_DIM, 1), lambda b, t: (0, 0)),
            pl.BlockSpec((80, ts), lambda b, t: (0, t)),
        ],
        out_specs=(
            pl.BlockSpec((None, A_Q, ts), lambda b, t: (b, 0, t)),
            pl.BlockSpec((None, ts, A_KV), lambda b, t: (b, t, 0)),
            pl.BlockSpec((None, A_KV_HEADS, cpt, HEAD_DIM + ONES_ROWS, tk), lambda b, t: (b, 0, t, 0, 0)),
            pl.BlockSpec((None, B_QK, ts), lambda b, t: (b, 0, t)),
            pl.BlockSpec((None, ts, B_QK), lambda b, t: (b, t, 0)),
            pl.BlockSpec((None, B_HEADS, cpt, B_V_DIM + ONES_ROWS, tk), lambda b, t: (b, 0, t, 0, 0)),
            pl.BlockSpec((None, _KNORM_ROWS, 128), lambda b, t: (b, 0, 0)),
        ),
        out_shape=out_shape,
        compiler_params=_params(("parallel", "arbitrary")),
        name="mix_in_projection",
    )(x, mod, gmix, w_inT, gq, gk, tab)


def _prepare_queries(qT_ref, w_ref, bound_ref, nchains, rows_of_chain, knorm2_of_chain):
    tq = qT_ref.shape[-1]
    row = lax.broadcasted_iota(jnp.int32, (2 * HEAD_DIM, tq), 0)
    for c in range(nchains):
        q = qT_ref[c * HEAD_DIM:(c + 1) * HEAD_DIM, :].astype(_F32)
        qq = jnp.concatenate([q, q], axis=0)
        lo = rows_of_chain(c)
        w_ref[c] = jnp.where((row >= lo) & (row < lo + HEAD_DIM), qq, 0.0).astype(_BF16)
        bound_ref[c] = jnp.sqrt(jnp.sum(q * q, axis=0, keepdims=True) * knorm2_of_chain(c))


def _softmax_value_loop(k_ref, vT_ref, w_ref, bound_ref, s_even_ref, s_odd_ref, m_ref, acc_ref, *, nchains, tk):
    nck = vT_ref.shape[0]
    acc_ref[...] = jnp.zeros(acc_ref.shape, _F32)

    def k_chunk(j):
        return k_ref[pl.ds(pl.multiple_of(j * tk, tk), tk), :]

    use_bound = jnp.max(bound_ref[...]) <= FAST_SOFTMAX_MAX_BOUND

    @pl.when(use_bound)
    def _():
        def scores(j, step):
            dj, c = divmod(step, nchains)
            jj = j + dj
            if dj:
                jj = jnp.where(jj >= nck, jj - nck, jj)
            return _dot(k_chunk(jj), w_ref[c])

        for a in range(FAST_LOOKAHEAD):
            s_even_ref[a] = scores(0, a)

        chunks_per_trip = min(FAST_STEPS_PER_TRIP // nchains, nck)
        assert nck % chunks_per_trip == 0

        def body(i, carry):
            j0 = i * chunks_per_trip
            pending = [s_even_ref[a] for a in range(FAST_LOOKAHEAD)]
            for u in range(chunks_per_trip):
                vc = vT_ref[j0 + u]
                for c in range(nchains):
                    pending.append(scores(j0, u * nchains + c + FAST_LOOKAHEAD))
                    p = jnp.exp2(pending.pop(0) - bound_ref[c]).astype(_BF16)
                    acc_ref[c] += _dot(vc, p)
            for a in range(FAST_LOOKAHEAD):
                s_even_ref[a] = pending[a]
            return carry

        lax.fori_loop(0, nck // chunks_per_trip, body, 0)

    @pl.when(jnp.logical_not(use_bound))
    def _():
        m_ref[...] = jnp.full(m_ref.shape, -jnp.inf, _F32)

        def phase(j_cur, j_next, s_cur_ref, s_next_ref):
            kn = k_chunk(j_next)
            vc = vT_ref[j_cur]
            for c in range(nchains):
                s = s_cur_ref[c]
                s_next_ref[c] = _dot(kn, w_ref[c])
                m_prev = m_ref[c]
                m_new = jnp.maximum(m_prev, jnp.max(s, axis=0, keepdims=True))
                p = jnp.exp2(s - m_new).astype(_BF16)
                acc_ref[c] = jnp.exp2(m_prev - m_new) * acc_ref[c] + _dot(vc, p)
                m_ref[c] = m_new

        k0 = k_chunk(0)
        for c in range(nchains):
            s_even_ref[c] = _dot(k0, w_ref[c])

        def body(i, carry):
            j = 2 * i
            phase(j, j + 1, s_even_ref, s_odd_ref)
            phase(j + 1, jnp.where(j + 2 == nck, 0, j + 2), s_odd_ref, s_even_ref)
            return carry

        lax.fori_loop(0, nck // 2, body, 0)


def _attn_a_kernel(qT_ref, k_ref, vT_ref, knorm_ref, o_ref, w_ref, bound_ref, s_even_ref, s_odd_ref,
                   m_ref, acc_ref, *, tk):
    g = pl.program_id(1)
    knorm2 = knorm_ref[pl.ds(_KNORM_A0 + g, 1), 0:1]
    _prepare_queries(qT_ref, w_ref, bound_ref, A_GROUP, lambda c: g * HEAD_DIM, lambda c: knorm2)
    _softmax_value_loop(k_ref, vT_ref, w_ref, bound_ref, s_even_ref, s_odd_ref, m_ref, acc_ref,
                        nchains=A_GROUP, tk=tk)
    for c in range(A_GROUP):
        o = acc_ref[c, 0:HEAD_DIM, :] / acc_ref[c, HEAD_DIM:HEAD_DIM + 1, :]
        o_ref[c * HEAD_DIM:(c + 1) * HEAD_DIM, :] = o.astype(_BF16)


def _attention_scratch(nchains, dv, tq, tk):
    return [
        pltpu.VMEM((nchains, 2 * HEAD_DIM, tq), _BF16),
        pltpu.VMEM((nchains, 1, tq), _F32),
        pltpu.VMEM((max(nchains, FAST_LOOKAHEAD), tk, tq), _F32),
        pltpu.VMEM((nchains, tk, tq), _F32),
        pltpu.VMEM((nchains, 1, tq), _F32),
        pltpu.VMEM((nchains, dv + ONES_ROWS, tq), _F32),
    ]


def _attn_a(qTa, kA, vTa, knorm, *, tq):
    n, _, s = qTa.shape
    _, _, nck, dv1, tk = vTa.shape
    return pl.pallas_call(
        functools.partial(_attn_a_kernel, tk=tk),
        grid=(n, A_KV_HEADS, s // tq),
        in_specs=[
            pl.BlockSpec((None, A_GROUP * HEAD_DIM, tq), lambda b, g, i: (b, g, i)),
            pl.BlockSpec((None, s, A_KV), lambda b, g, i: (b, 0, 0)),
            pl.BlockSpec((None, None, nck, dv1, tk), lambda b, g, i: (b, g, 0, 0, 0)),
            pl.BlockSpec((None, _KNORM_ROWS, 128), lambda b, g, i: (b, 0, 0)),
        ],
        out_specs=pl.BlockSpec((None, A_GROUP * HEAD_DIM, tq), lambda b, g, i: (b, g, i)),
        out_shape=jax.ShapeDtypeStruct((n, A_Q, s), _BF16),
        scratch_shapes=_attention_scratch(A_GROUP, HEAD_DIM, tq, tk),
        compiler_params=_params(("parallel", "parallel", "parallel"), _ATTENTION_FLAGS),
        name="gqa_attention",
    )(qTa, kA, vTa, knorm)


def _attn_b_kernel(qT_ref, k_ref, vT_ref, knorm_ref, lam_ref, gsub_ref, o_ref, w_ref, bound_ref,
                   s_even_ref, s_odd_ref, m_ref, acc_ref, *, tk, lam_init):
    hd = pl.program_id(1)
    knorm2 = knorm_ref[pl.ds(_KNORM_B0 + 2 * hd, 2), 0:1]
    _prepare_queries(qT_ref, w_ref, bound_ref, 2, lambda c: c * HEAD_DIM, lambda c: knorm2[c:c + 1])
    _softmax_value_loop(k_ref, vT_ref, w_ref, bound_ref, s_even_ref, s_odd_ref, m_ref, acc_ref,
                        nchains=2, tk=tk)

    lp = lam_ref[...]
    lam = (jnp.exp(jnp.sum(lp[0:1] * lp[1:2], axis=1, keepdims=True))
           - jnp.exp(jnp.sum(lp[2:3] * lp[3:4], axis=1, keepdims=True)) + lam_init)
    o = (acc_ref[0, 0:B_V_DIM, :] / acc_ref[0, B_V_DIM:B_V_DIM + 1, :]
         - lam * (acc_ref[1, 0:B_V_DIM, :] / acc_ref[1, B_V_DIM:B_V_DIM + 1, :]))
    ms = jnp.mean(o * o, axis=0, keepdims=True)
    o = ((o * lax.rsqrt(ms + NORM_EPS)) * gsub_ref[...]) * (1.0 - lam_init)
    o_ref[...] = o.astype(_BF16)


def _attn_b(qTb, kB, vTb, knorm, lam_params, gsub, *, tq, lam_init):
    n, _, s = qTb.shape
    _, _, nck, dv1, tk = vTb.shape
    return pl.pallas_call(
        functools.partial(_attn_b_kernel, tk=tk, lam_init=lam_init),
        grid=(n, B_HEADS, s // tq),
        in_specs=[
            pl.BlockSpec((None, 2 * HEAD_DIM, tq), lambda b, hd, i: (b, hd, i)),
            pl.BlockSpec((None, s, 2 * HEAD_DIM), lambda b, hd, i: (b, 0, hd)),
            pl.BlockSpec((None, None, nck, dv1, tk), lambda b, hd, i: (b, hd, 0, 0, 0)),
            pl.BlockSpec((None, _KNORM_ROWS, 128), lambda b, hd, i: (b, 0, 0)),
            pl.BlockSpec((4, HEAD_DIM), lambda b, hd, i: (0, 0)),
            pl.BlockSpec((B_V_DIM, 1), lambda b, hd, i: (0, 0)),
        ],
        out_specs=pl.BlockSpec((None, B_V_DIM, tq), lambda b, hd, i: (b, hd, i)),
        out_shape=jax.ShapeDtypeStruct((n, B_V, s), _BF16),
        scratch_shapes=_attention_scratch(2, B_V_DIM, tq, tk),
        compiler_params=_params(("parallel", "parallel", "parallel"), _ATTENTION_FLAGS),
        name="diff_attention",
    )(qTb, kB, vTb, knorm, lam_params, gsub)


def _ffn_kernel(x_ref, oTa_ref, oTb_ref, mod_ref, wout_ref, gffn_ref, wgu_ref, wdn_ref, y_ref, act_ref):
    mixed = _dot_tn(oTa_ref[...], wout_ref[0:A_Q, :]) + _dot_tn(oTb_ref[...], wout_ref[A_Q:, :])
    gt1, sh2, sc2, gt2 = mod_ref[2:3, :], mod_ref[3:4, :], mod_ref[4:5, :], mod_ref[5:6, :]
    x1 = x_ref[...] + gt1 * mixed
    h = ((x1 * _rms_rows(x1)) * gffn_ref[...]) * (1.0 + sc2) + sh2
    hb = h.astype(_BF16)
    for c in range(wgu_ref.shape[0]):
        gu = _dot(hb, wgu_ref[c])
        gate, up = gu[:, :FF_CHUNK], gu[:, FF_CHUNK:]
        act_ref[:, c * FF_CHUNK:(c + 1) * FF_CHUNK] = ((gate * jax.nn.sigmoid(gate)) * up).astype(_BF16)
    y_ref[...] = x1 + gt2 * _dot(act_ref[...], wdn_ref[...])


def _resident(shape):
    return pl.BlockSpec(shape, lambda b, t: (0,) * len(shape), pipeline_mode=pl.Buffered(1))


def _out_ffn(x, oTa, oTb, mod, w_out, gffn, w_gu, w_dn, *, ts):
    n, s, d = x.shape
    nff, _, _ = w_gu.shape
    dff = w_dn.shape[0]
    return pl.pallas_call(
        _ffn_kernel,
        grid=(n, s // ts),
        in_specs=[
            pl.BlockSpec((None, ts, d), lambda b, t: (b, t, 0)),
            pl.BlockSpec((None, A_Q, ts), lambda b, t: (b, 0, t)),
            pl.BlockSpec((None, B_V, ts), lambda b, t: (b, 0, t)),
            pl.BlockSpec((None, 6, d), lambda b, t: (b, 0, 0)),
            _resident((A_Q + B_V, d)),
            pl.BlockSpec((1, d), lambda b, t: (0, 0)),
            _resident((nff, d, 2 * FF_CHUNK)),
            _resident((dff, d)),
        ],
        out_specs=pl.BlockSpec((None, ts, d), lambda b, t: (b, t, 0)),
        out_shape=jax.ShapeDtypeStruct((n, s, d), _F32),
        scratch_shapes=[pltpu.VMEM((ts, dff), _BF16)],
        compiler_params=_params(("parallel", "parallel")),
        name="out_proj_swiglu",
    )(x, oTa, oTb, mod, w_out, gffn, w_gu, w_dn)


def _final_kernel(x_ref, mod_ref, g_ref, y_ref):
    x = x_ref[...]
    shift, scale = mod_ref[0:1, :], mod_ref[1:2, :]
    y_ref[...] = ((x * _rms_rows(x)) * g_ref[...]) * (1.0 + scale) + shift


def _final_norm(x, modf, g, *, ts):
    n, s, d = x.shape
    return pl.pallas_call(
        _final_kernel,
        grid=(n, s // ts),
        in_specs=[
            pl.BlockSpec((None, ts, d), lambda b, t: (b, t, 0)),
            pl.BlockSpec((None, 2, d), lambda b, t: (b, 0, 0)),
            pl.BlockSpec((1, d), lambda b, t: (0, 0)),
        ],
        out_specs=pl.BlockSpec((None, ts, d), lambda b, t: (b, t, 0)),
        out_shape=jax.ShapeDtypeStruct((n, s, d), _F32),
        compiler_params=_params(("parallel", "parallel")),
        name="final_norm",
    )(x, modf, g)


def _rope_table_t(s):
    def tables(pos, dim, theta):
        inv = 1.0 / (theta ** (jnp.arange(0, dim, 2, dtype=_F32) / dim))
        ang = pos.astype(_F32)[:, None] * inv[None, :]
        return jnp.cos(ang).T, jnp.sin(ang).T

    t = jnp.arange(s)
    cos_r, sin_r = tables(t // GRID_W, HEAD_DIM // 2, AXIAL_THETA)
    cos_c, sin_c = tables(t % GRID_W, HEAD_DIM // 2, AXIAL_THETA)
    cos_p, sin_p = tables(t, PARTIAL_ROT, ROPE_THETA)
    return jnp.concatenate([cos_r, sin_r, cos_c, sin_c, cos_p, sin_p], axis=0)


def _trunk(x, c, w_ada, b_ada, g_mix, w_in, g_q_a, g_k_a, lam_q1, lam_k1, lam_q2, lam_k2, g_subln,
           w_out, g_ffn, w_gate_up, w_down, w_ada_final, b_ada_final, g_final, *, ts, tq, tk):
    n, s, d = x.shape
    depth = w_ada.shape[0]
    dff = w_down.shape[1]
    nff = dff // FF_CHUNK

    c_pad = jnp.zeros((8, d), _F32).at[:n].set(c)
    mod = _modulation(c_pad, w_ada, b_ada[:, None, :])[:, :n].reshape(depth, n, 6, d)
    modf = _modulation(c_pad, w_ada_final[None], b_ada_final[None, None, :])[0, :n].reshape(n, 2, d)

    tab = _rope_table_t(s)
    w_inT = jnp.swapaxes(w_in, 1, 2).astype(_BF16)
    w_out_b = w_out.astype(_BF16)
    w_gu = jnp.concatenate([w_gate_up[:, :, :dff].reshape(depth, d, nff, FF_CHUNK),
                            w_gate_up[:, :, dff:].reshape(depth, d, nff, FF_CHUNK)], axis=-1)
    w_gu = jnp.transpose(w_gu, (0, 2, 1, 3)).astype(_BF16)
    w_dn = w_down.astype(_BF16)
    lam_params = jnp.stack([lam_q1, lam_k1, lam_q2, lam_k2], axis=1)

    for l in range(depth):
        qTa, kA, vTa, qTb, kB, vTb, knorm = _inproj(
            x, mod[l], g_mix[l][None, :], w_inT[l], g_q_a[l][:, None], g_k_a[l][:, None], tab, ts=ts, tk=tk)
        oTa = _attn_a(qTa, kA, vTa, knorm, tq=tq)
        lam_init = 0.8 - 0.6 * math.exp(-0.3 * l)
        oTb = _attn_b(qTb, kB, vTb, knorm, lam_params[l], g_subln[l][:, None], tq=tq, lam_init=lam_init)
        x = _out_ffn(x, oTa, oTb, mod[l], w_out_b[l], g_ffn[l][None, :], w_gu[l], w_dn[l], ts=ts)
    return _final_norm(x, modf, g_final[None, :], ts=ts)


def kernel(x_prompt, x_sample, c_prompt, c_sample, w_ada, b_ada, g_mix, w_in, g_q_a, g_k_a, lam_q1, lam_k1, lam_q2, lam_k2, g_subln, w_out, g_ffn, w_gate_up, w_down, w_ada_final, b_ada_final, g_final):
    nb = x_prompt.shape[0]
    x = jnp.concatenate([x_prompt, x_sample], axis=0)
    c = jnp.concatenate([c_prompt, c_sample], axis=0)
    y = _trunk(x, c, w_ada, b_ada, g_mix, w_in, g_q_a, g_k_a, lam_q1, lam_k1, lam_q2, lam_k2, g_subln,
               w_out, g_ffn, w_gate_up, w_down, w_ada_final, b_ada_final, g_final, ts=512, tq=256, tk=256)
    return (y[:nb], y[nb:])
```

```python
import functools
import math

import jax
import jax.numpy as jnp
from jax import lax
from jax.experimental import pallas as pl
from jax.experimental.pallas import tpu as pltpu

HEAD_DIM = 64
A_HEADS = 8
A_KV_HEADS = 2
A_GROUP = A_HEADS // A_KV_HEADS
B_HEADS = 4
B_V_DIM = 2 * HEAD_DIM
A_Q = A_HEADS * HEAD_DIM
A_KV = A_KV_HEADS * HEAD_DIM
B_QK = B_HEADS * 2 * HEAD_DIM
B_V = B_HEADS * B_V_DIM
IN_WIDTH = A_Q + 2 * A_KV + 2 * B_QK + B_V
GRID_W = 64
AXIAL_THETA = 10000.0
ROPE_THETA = 500000.0
PARTIAL_ROT = HEAD_DIM // 4
NORM_EPS = 1e-6
QK_SCALE = HEAD_DIM ** -0.5 * math.log2(math.e)

_QA0, _KA0, _VA0 = 0, A_Q, A_Q + A_KV
_QB0 = A_Q + 2 * A_KV
_KB0 = _QB0 + B_QK
_VB0 = _KB0 + B_QK

_KNORM_B0, _KNORM_A0, _KNORM_ROWS = 0, 2 * B_HEADS, 16
FAST_SOFTMAX_MAX_BOUND = 48.0
FAST_STEPS_PER_TRIP = 128
FAST_LOOKAHEAD = 4

FF_CHUNK = 256
V7X_VMEM_LIMIT_BYTES = 56 * 1024 * 1024

_BF16 = jnp.bfloat16
_F32 = jnp.float32


def _params(semantics, flags=None):
    return pltpu.CompilerParams(dimension_semantics=semantics, vmem_limit_bytes=V7X_VMEM_LIMIT_BYTES, flags=flags)


_ATTENTION_FLAGS = None


def _dot(a, b):
    return jnp.dot(a, b, preferred_element_type=_F32)


def _dot_nt(a, b):
    return lax.dot_general(a, b, (((1,), (1,)), ((), ())), preferred_element_type=_F32)


def _dot_tn(a, b):
    return lax.dot_general(a, b, (((0,), (0,)), ((), ())), preferred_element_type=_F32)


def _split_bf16(x):
    hi = x.astype(_BF16)
    lo = (x - hi.astype(_F32)).astype(_BF16)
    return hi, lo


def _mod_kernel(c_ref, w_ref, b_ref, o_ref):
    c = c_ref[...]
    c_act = c * jax.nn.sigmoid(c)
    c_hi, c_lo = _split_bf16(c_act)
    w_hi, w_lo = _split_bf16(w_ref[...])
    acc = _dot(c_hi, w_hi) + (_dot(c_lo, w_hi) + _dot(c_hi, w_lo))
    o_ref[...] = acc + b_ref[...]


def _modulation(c_pad, w, b, tn=1024):
    nl, d, n = w.shape
    return pl.pallas_call(
        _mod_kernel,
        grid=(nl, n // tn),
        in_specs=[
            pl.BlockSpec((8, d), lambda l, j: (0, 0)),
            pl.BlockSpec((None, d, tn), lambda l, j: (l, 0, j)),
            pl.BlockSpec((None, 1, tn), lambda l, j: (l, 0, j)),
        ],
        out_specs=pl.BlockSpec((None, 8, tn), lambda l, j: (l, 0, j)),
        out_shape=jax.ShapeDtypeStruct((nl, 8, n), _F32),
        compiler_params=_params(("parallel", "parallel")),
        name="adaln_modulation",
    )(c_pad, w, b)


def _rms_rows(x):
    return lax.rsqrt(jnp.mean(x * x, axis=1, keepdims=True) + NORM_EPS)


def _inproj_kernel(x_ref, mod_ref, gmix_ref, wT_ref, gq_ref, gk_ref, tab_ref,
                   qTa_ref, kA_ref, vTa_ref, qTb_ref, kB_ref, vTb_ref, knorm_ref, *, tk):
    ts = x_ref.shape[0]
    x = x_ref[...]
    sh1 = mod_ref[0:1, :]
    sc1 = mod_ref[1:2, :]
    h = ((x * _rms_rows(x)) * gmix_ref[...]) * (1.0 + sc1) + sh1
    hb = h.astype(_BF16)

    def proj_t(lo, n):
        return _dot_nt(wT_ref[lo:lo + n, :], hb)

    tab = tab_ref[...]
    cos_r, sin_r, cos_c, sin_c = tab[0:16], tab[16:32], tab[32:48], tab[48:64]
    cos_p, sin_p = tab[64:72], tab[72:80]

    def head_norm(x3, g_ref):
        ms = jnp.mean(x3 * x3, axis=1, keepdims=True)
        return (x3 * lax.rsqrt(ms + NORM_EPS)) * g_ref[...][None]

    def axial(x3):
        r1, r2, c1, c2 = x3[:, 0:16], x3[:, 16:32], x3[:, 32:48], x3[:, 48:64]
        return jnp.concatenate([r1 * cos_r - r2 * sin_r, r1 * sin_r + r2 * cos_r,
                                c1 * cos_c - c2 * sin_c, c1 * sin_c + c2 * cos_c], axis=1)

    def partial(x3):
        half = PARTIAL_ROT // 2
        x1, x2, rest = x3[:, 0:half], x3[:, half:PARTIAL_ROT], x3[:, PARTIAL_ROT:]
        return jnp.concatenate([x1 * cos_p - x2 * sin_p, x1 * sin_p + x2 * cos_p, rest], axis=1)

    qa = proj_t(_QA0, A_Q).reshape(A_HEADS, HEAD_DIM, ts)
    qa = axial(head_norm(qa, gq_ref)) * QK_SCALE
    qTa_ref[...] = qa.reshape(A_Q, ts).astype(_BF16)

    def max_key_norm2(k3):
        kf = k3.astype(_BF16).astype(_F32)
        return jnp.max(jnp.sum(kf * kf, axis=1), axis=1, keepdims=True)

    one_hot_rows = (lax.broadcasted_iota(jnp.int32, (HEAD_DIM, ts), 0) == 0).astype(_F32)

    def widened_keys_t(k3):
        parts = []
        for hh in range(k3.shape[0]):
            parts += [k3[hh], one_hot_rows]
        return jnp.concatenate(parts, axis=0).T.astype(_BF16)

    ka = axial(head_norm(proj_t(_KA0, A_KV).reshape(A_KV_HEADS, HEAD_DIM, ts), gk_ref))
    kA_ref[...] = widened_keys_t(ka)

    va = proj_t(_VA0, A_KV).astype(_BF16)
    for g in range(A_KV_HEADS):
        for c in range(ts // tk):
            vTa_ref[g, c] = va[g * HEAD_DIM:(g + 1) * HEAD_DIM, c * tk:(c + 1) * tk]

    qb = proj_t(_QB0, B_QK).reshape(2 * B_HEADS, HEAD_DIM, ts)
    qTb_ref[...] = (partial(qb) * QK_SCALE).reshape(B_QK, ts).astype(_BF16)

    kb = partial(proj_t(_KB0, B_QK).reshape(2 * B_HEADS, HEAD_DIM, ts))
    kB_ref[...] = widened_keys_t(kb)

    vb = proj_t(_VB0, B_V).astype(_BF16)
    for hd in range(B_HEADS):
        for c in range(ts // tk):
            vTb_ref[hd, c] = vb[hd * B_V_DIM:(hd + 1) * B_V_DIM, c * tk:(c + 1) * tk]

    @pl.when(pl.program_id(1) == 0)
    def _():
        knorm_ref[...] = jnp.zeros(knorm_ref.shape, _F32)

    lanes = knorm_ref.shape[1]
    b_rows = slice(_KNORM_B0, _KNORM_B0 + 2 * B_HEADS)
    a_rows = slice(_KNORM_A0, _KNORM_A0 + A_KV_HEADS)
    knorm_ref[b_rows, :] = jnp.maximum(knorm_ref[b_rows, :], jnp.broadcast_to(max_key_norm2(kb), (2 * B_HEADS, lanes)))
    knorm_ref[a_rows, :] = jnp.maximum(knorm_ref[a_rows, :], jnp.broadcast_to(max_key_norm2(ka), (A_KV_HEADS, lanes)))


def _inproj(x, mod, gmix, w_inT, gq, gk, tab, *, ts, tk):
    n, s, d = x.shape
    nck = s // tk
    cpt = ts // tk
    out_shape = (
        jax.ShapeDtypeStruct((n, A_Q, s), _BF16),
        jax.ShapeDtypeStruct((n, s, 2 * A_KV), _BF16),
        jax.ShapeDtypeStruct((n, A_KV_HEADS, nck, HEAD_DIM, tk), _BF16),
        jax.ShapeDtypeStruct((n, B_QK, s), _BF16),
        jax.ShapeDtypeStruct((n, s, 2 * B_QK), _BF16),
        jax.ShapeDtypeStruct((n, B_HEADS, nck, B_V_DIM, tk), _BF16),
        jax.ShapeDtypeStruct((n, _KNORM_ROWS, 128), _F32),
    )
    return pl.pallas_call(
        functools.partial(_inproj_kernel, tk=tk),
        grid=(n, s // ts),
        in_specs=[
            pl.BlockSpec((None, ts, d), lambda b, t: (b, t, 0)),
            pl.BlockSpec((None, 6, d), lambda b, t: (b, 0, 0)),
            pl.BlockSpec((1, d), lambda b, t: (0, 0)),
            pl.BlockSpec((IN_WIDTH, d), lambda b, t: (0, 0)),
            pl.BlockSpec((HEAD_DIM, 1), lambda b, t: (0, 0)),
            pl.BlockSpec((HEAD_DIM, 1), lambda b, t: (0, 0)),
            pl.BlockSpec((80, ts), lambda b, t: (0, t)),
        ],
        out_specs=(
            pl.BlockSpec((None, A_Q, ts), lambda b, t: (b, 0, t)),
            pl.BlockSpec((None, ts, 2 * A_KV), lambda b, t: (b, t, 0)),
            pl.BlockSpec((None, A_KV_HEADS, cpt, HEAD_DIM, tk), lambda b, t: (b, 0, t, 0, 0)),
            pl.BlockSpec((None, B_QK, ts), lambda b, t: (b, 0, t)),
            pl.BlockSpec((None, ts, 2 * B_QK), lambda b, t: (b, t, 0)),
            pl.BlockSpec((None, B_HEADS, cpt, B_V_DIM, tk), lambda b, t: (b, 0, t, 0, 0)),
            pl.BlockSpec((None, _KNORM_ROWS, 128), lambda b, t: (b, 0, 0)),
        ),
        out_shape=out_shape,
        compiler_params=_params(("parallel", "arbitrary")),
        name="mix_in_projection",
    )(x, mod, gmix, w_inT, gq, gk, tab)


def _softmax_value_loop(qT_ref, k_ref, vT_ref, knorm2_of_chain, key_head_of_chain, w_ref, s_even_ref, s_odd_ref,
                        m_ref, l_ref, acc_ref, *, nchains, tk):
    nck = vT_ref.shape[0]
    tq = qT_ref.shape[-1]
    acc_ref[...] = jnp.zeros(acc_ref.shape, _F32)
    l_ref[...] = jnp.zeros(l_ref.shape, _F32)

    qs = [qT_ref[c * HEAD_DIM:(c + 1) * HEAD_DIM, :] for c in range(nchains)]
    bounds = []
    for c in range(nchains):
        qf = qs[c].astype(_F32)
        bounds.append(jnp.sqrt(jnp.sum(qf * qf, axis=0, keepdims=True) * knorm2_of_chain(c)))
    use_bound = jnp.max(jnp.concatenate(bounds, axis=0)) <= FAST_SOFTMAX_MAX_BOUND

    first_row = lax.broadcasted_iota(jnp.int32, (HEAD_DIM, tq), 0) == 0
    for c in range(nchains):
        offset = jnp.where(use_bound, -bounds[c], 0.0)
        offset_rows = jnp.where(first_row, jnp.broadcast_to(offset, (HEAD_DIM, tq)), 0.0)
        w_ref[c] = jnp.concatenate([qs[c], offset_rows.astype(_BF16)], axis=0)

    def k_chunk(j, c):
        lane0 = key_head_of_chain(c) * 2 * HEAD_DIM
        return k_ref[pl.ds(pl.multiple_of(j * tk, tk), tk), lane0:lane0 + 2 * HEAD_DIM]

    def partial_sums(p):
        return jnp.sum(p.reshape(tk // 8, 8, tq), axis=0)

    @pl.when(use_bound)
    def _():
        def scores(j, step):
            dj, c = divmod(step, nchains)
            jj = j + dj
            if dj:
                jj = jnp.where(jj >= nck, jj - nck, jj)
            return _dot(k_chunk(jj, c), w_ref[c])

        for a in range(FAST_LOOKAHEAD):
            s_even_ref[a] = scores(0, a)

        chunks_per_trip = min(FAST_STEPS_PER_TRIP // nchains, nck)
        assert nck % chunks_per_trip == 0

        def body(i, carry):
            j0 = i * chunks_per_trip
            pending = [s_even_ref[a] for a in range(FAST_LOOKAHEAD)]
            for u in range(chunks_per_trip):
                vc = vT_ref[j0 + u]
                for c in range(nchains):
                    pending.append(scores(j0, u * nchains + c + FAST_LOOKAHEAD))
                    p = jnp.exp2(pending.pop(0))
                    l_ref[c] += partial_sums(p)
                    acc_ref[c] += _dot(vc, p.astype(_BF16))
            for a in range(FAST_LOOKAHEAD):
                s_even_ref[a] = pending[a]
            return carry

        lax.fori_loop(0, nck // chunks_per_trip, body, 0)

    @pl.when(jnp.logical_not(use_bound))
    def _():
        m_ref[...] = jnp.full(m_ref.shape, -jnp.inf, _F32)

        def phase(j_cur, j_next, s_cur_ref, s_next_ref):
            vc = vT_ref[j_cur]
            for c in range(nchains):
                s = s_cur_ref[c]
                s_next_ref[c] = _dot(k_chunk(j_next, c), w_ref[c])
                m_prev = m_ref[c]
                m_new = jnp.maximum(m_prev, jnp.max(s, axis=0, keepdims=True))
                alpha = jnp.exp2(m_prev - m_new)
                p = jnp.exp2(s - m_new)
                l_ref[c] = alpha * l_ref[c] + partial_sums(p)
                acc_ref[c] = alpha * acc_ref[c] + _dot(vc, p.astype(_BF16))
                m_ref[c] = m_new

        for c in range(nchains):
            s_even_ref[c] = _dot(k_chunk(0, c), w_ref[c])

        def body(i, carry):
            j = 2 * i
            phase(j, j + 1, s_even_ref, s_odd_ref)
            phase(j + 1, jnp.where(j + 2 == nck, 0, j + 2), s_odd_ref, s_even_ref)
            return carry

        lax.fori_loop(0, nck // 2, body, 0)


def _denominator(l_ref, c):
    return jnp.sum(l_ref[c], axis=0, keepdims=True)


def _attn_a_kernel(qT_ref, k_ref, vT_ref, knorm_ref, o_ref, w_ref, s_even_ref, s_odd_ref, m_ref, l_ref, acc_ref,
                   *, tk):
    g = pl.program_id(1)
    knorm2 = knorm_ref[pl.ds(_KNORM_A0 + g, 1), 0:1]
    _softmax_value_loop(qT_ref, k_ref, vT_ref, lambda c: knorm2, lambda c: 0, w_ref, s_even_ref, s_odd_ref,
                        m_ref, l_ref, acc_ref, nchains=A_GROUP, tk=tk)
    for c in range(A_GROUP):
        o_ref[c * HEAD_DIM:(c + 1) * HEAD_DIM, :] = (acc_ref[c] / _denominator(l_ref, c)).astype(_BF16)


def _attention_scratch(nchains, dv, tq, tk):
    return [
        pltpu.VMEM((nchains, 2 * HEAD_DIM, tq), _BF16),
        pltpu.VMEM((max(nchains, FAST_LOOKAHEAD), tk, tq), _F32),
        pltpu.VMEM((nchains, tk, tq), _F32),
        pltpu.VMEM((nchains, 1, tq), _F32),
        pltpu.VMEM((nchains, 8, tq), _F32),
        pltpu.VMEM((nchains, dv, tq), _F32),
    ]


def _attn_a(qTa, kA, vTa, knorm, *, tq):
    n, _, s = qTa.shape
    _, _, nck, dv, tk = vTa.shape
    return pl.pallas_call(
        functools.partial(_attn_a_kernel, tk=tk),
        grid=(n, A_KV_HEADS, s // tq),
        in_specs=[
            pl.BlockSpec((None, A_GROUP * HEAD_DIM, tq), lambda b, g, i: (b, g, i)),
            pl.BlockSpec((None, s, 2 * HEAD_DIM), lambda b, g, i: (b, 0, g)),
            pl.BlockSpec((None, None, nck, dv, tk), lambda b, g, i: (b, g, 0, 0, 0)),
            pl.BlockSpec((None, _KNORM_ROWS, 128), lambda b, g, i: (b, 0, 0)),
        ],
        out_specs=pl.BlockSpec((None, A_GROUP * HEAD_DIM, tq), lambda b, g, i: (b, g, i)),
        out_shape=jax.ShapeDtypeStruct((n, A_Q, s), _BF16),
        scratch_shapes=_attention_scratch(A_GROUP, HEAD_DIM, tq, tk),
        compiler_params=_params(("parallel", "parallel", "parallel"), _ATTENTION_FLAGS),
        name="gqa_attention",
    )(qTa, kA, vTa, knorm)


def _attn_b_kernel(qT_ref, k_ref, vT_ref, knorm_ref, lam_ref, gsub_ref, o_ref, w_ref, s_even_ref, s_odd_ref,
                   m_ref, l_ref, acc_ref, *, tk, lam_init):
    hd = pl.program_id(1)
    knorm2 = knorm_ref[pl.ds(_KNORM_B0 + 2 * hd, 2), 0:1]
    _softmax_value_loop(qT_ref, k_ref, vT_ref, lambda c: knorm2[c:c + 1], lambda c: c, w_ref, s_even_ref,
                        s_odd_ref, m_ref, l_ref, acc_ref, nchains=2, tk=tk)

    lp = lam_ref[...]
    lam = (jnp.exp(jnp.sum(lp[0:1] * lp[1:2], axis=1, keepdims=True))
           - jnp.exp(jnp.sum(lp[2:3] * lp[3:4], axis=1, keepdims=True)) + lam_init)
    o = acc_ref[0] / _denominator(l_ref, 0) - lam * (acc_ref[1] / _denominator(l_ref, 1))
    ms = jnp.mean(o * o, axis=0, keepdims=True)
    o = ((o * lax.rsqrt(ms + NORM_EPS)) * gsub_ref[...]) * (1.0 - lam_init)
    o_ref[...] = o.astype(_BF16)


def _attn_b(qTb, kB, vTb, knorm, lam_params, gsub, *, tq, lam_init):
    n, _, s = qTb.shape
    _, _, nck, dv, tk = vTb.shape
    return pl.pallas_call(
        functools.partial(_attn_b_kernel, tk=tk, lam_init=lam_init),
        grid=(n, B_HEADS, s // tq),
        in_specs=[
            pl.BlockSpec((None, 2 * HEAD_DIM, tq), lambda b, hd, i: (b, hd, i)),
            pl.BlockSpec((None, s, 4 * HEAD_DIM), lambda b, hd, i: (b, 0, hd)),
            pl.BlockSpec((None, None, nck, dv, tk), lambda b, hd, i: (b, hd, 0, 0, 0)),
            pl.BlockSpec((None, _KNORM_ROWS, 128), lambda b, hd, i: (b, 0, 0)),
            pl.BlockSpec((4, HEAD_DIM), lambda b, hd, i: (0, 0)),
            pl.BlockSpec((B_V_DIM, 1), lambda b, hd, i: (0, 0)),
        ],
        out_specs=pl.BlockSpec((None, B_V_DIM, tq), lambda b, hd, i: (b, hd, i)),
        out_shape=jax.ShapeDtypeStruct((n, B_V, s), _BF16),
        scratch_shapes=_attention_scratch(2, B_V_DIM, tq, tk),
        compiler_params=_params(("parallel", "parallel", "parallel"), _ATTENTION_FLAGS),
        name="diff_attention",
    )(qTb, kB, vTb, knorm, lam_params, gsub)


def _ffn_kernel(x_ref, oTa_ref, oTb_ref, mod_ref, wout_ref, gffn_ref, wgu_ref, wdn_ref, y_ref, act_ref):
    mixed = _dot_tn(oTa_ref[...], wout_ref[0:A_Q, :]) + _dot_tn(oTb_ref[...], wout_ref[A_Q:, :])
    gt1, sh2, sc2, gt2 = mod_ref[2:3, :], mod_ref[3:4, :], mod_ref[4:5, :], mod_ref[5:6, :]
    x1 = x_ref[...] + gt1 * mixed
    h = ((x1 * _rms_rows(x1)) * gffn_ref[...]) * (1.0 + sc2) + sh2
    hb = h.astype(_BF16)
    for c in range(wgu_ref.shape[0]):
        gu = _dot(hb, wgu_ref[c])
        gate, up = gu[:, :FF_CHUNK], gu[:, FF_CHUNK:]
        act_ref[:, c * FF_CHUNK:(c + 1) * FF_CHUNK] = ((gate * jax.nn.sigmoid(gate)) * up).astype(_BF16)
    y_ref[...] = x1 + gt2 * _dot(act_ref[...], wdn_ref[...])


def _resident(shape):
    return pl.BlockSpec(shape, lambda b, t: (0,) * len(shape), pipeline_mode=pl.Buffered(1))


def _out_ffn(x, oTa, oTb, mod, w_out, gffn, w_gu, w_dn, *, ts):
    n, s, d = x.shape
    nff, _, _ = w_gu.shape
    dff = w_dn.shape[0]
    return pl.pallas_call(
        _ffn_kernel,
        grid=(n, s // ts),
        in_specs=[
            pl.BlockSpec((None, ts, d), lambda b, t: (b, t, 0)),
            pl.BlockSpec((None, A_Q, ts), lambda b, t: (b, 0, t)),
            pl.BlockSpec((None, B_V, ts), lambda b, t: (b, 0, t)),
            pl.BlockSpec((None, 6, d), lambda b, t: (b, 0, 0)),
            _resident((A_Q + B_V, d)),
            pl.BlockSpec((1, d), lambda b, t: (0, 0)),
            _resident((nff, d, 2 * FF_CHUNK)),
            _resident((dff, d)),
        ],
        out_specs=pl.BlockSpec((None, ts, d), lambda b, t: (b, t, 0)),
        out_shape=jax.ShapeDtypeStruct((n, s, d), _F32),
        scratch_shapes=[pltpu.VMEM((ts, dff), _BF16)],
        compiler_params=_params(("parallel", "parallel")),
        name="out_proj_swiglu",
    )(x, oTa, oTb, mod, w_out, gffn, w_gu, w_dn)


def _final_kernel(x_ref, mod_ref, g_ref, y_ref):
    x = x_ref[...]
    shift, scale = mod_ref[0:1, :], mod_ref[1:2, :]
    y_ref[...] = ((x * _rms_rows(x)) * g_ref[...]) * (1.0 + scale) + shift


def _final_norm(x, modf, g, *, ts):
    n, s, d = x.shape
    return pl.pallas_call(
        _final_kernel,
        grid=(n, s // ts),
        in_specs=[
            pl.BlockSpec((None, ts, d), lambda b, t: (b, t, 0)),
            pl.BlockSpec((None, 2, d), lambda b, t: (b, 0, 0)),
            pl.BlockSpec((1, d), lambda b, t: (0, 0)),
        ],
        out_specs=pl.BlockSpec((None, ts, d), lambda b, t: (b, t, 0)),
        out_shape=jax.ShapeDtypeStruct((n, s, d), _F32),
        compiler_params=_params(("parallel", "parallel")),
        name="final_norm",
    )(x, modf, g)


def _rope_table_t(s):
    def tables(pos, dim, theta):
        inv = 1.0 / (theta ** (jnp.arange(0, dim, 2, dtype=_F32) / dim))
        ang = pos.astype(_F32)[:, None] * inv[None, :]
        return jnp.cos(ang).T, jnp.sin(ang).T

    t = jnp.arange(s)
    cos_r, sin_r = tables(t // GRID_W, HEAD_DIM // 2, AXIAL_THETA)
    cos_c, sin_c = tables(t % GRID_W, HEAD_DIM // 2, AXIAL_THETA)
    cos_p, sin_p = tables(t, PARTIAL_ROT, ROPE_THETA)
    return jnp.concatenate([cos_r, sin_r, cos_c, sin_c, cos_p, sin_p], axis=0)


def _trunk(x, c, w_ada, b_ada, g_mix, w_in, g_q_a, g_k_a, lam_q1, lam_k1, lam_q2, lam_k2, g_subln,
           w_out, g_ffn, w_gate_up, w_down, w_ada_final, b_ada_final, g_final, *, ts, tq, tk):
    n, s, d = x.shape
    depth = w_ada.shape[0]
    dff = w_down.shape[1]
    nff = dff // FF_CHUNK

    c_pad = jnp.zeros((8, d), _F32).at[:n].set(c)
    mod = _modulation(c_pad, w_ada, b_ada[:, None, :])[:, :n].reshape(depth, n, 6, d)
    modf = _modulation(c_pad, w_ada_final[None], b_ada_final[None, None, :])[0, :n].reshape(n, 2, d)

    tab = _rope_table_t(s)
    w_inT = jnp.swapaxes(w_in, 1, 2).astype(_BF16)
    w_out_b = w_out.astype(_BF16)
    w_gu = jnp.concatenate([w_gate_up[:, :, :dff].reshape(depth, d, nff, FF_CHUNK),
                            w_gate_up[:, :, dff:].reshape(depth, d, nff, FF_CHUNK)], axis=-1)
    w_gu = jnp.transpose(w_gu, (0, 2, 1, 3)).astype(_BF16)
    w_dn = w_down.astype(_BF16)
    lam_params = jnp.stack([lam_q1, lam_k1, lam_q2, lam_k2], axis=1)

    for l in range(depth):
        qTa, kA, vTa, qTb, kB, vTb, knorm = _inproj(
            x, mod[l], g_mix[l][None, :], w_inT[l], g_q_a[l][:, None], g_k_a[l][:, None], tab, ts=ts, tk=tk)
        oTa = _attn_a(qTa, kA, vTa, knorm, tq=tq)
        lam_init = 0.8 - 0.6 * math.exp(-0.3 * l)
        oTb = _attn_b(qTb, kB, vTb, knorm, lam_params[l], g_subln[l][:, None], tq=tq, lam_init=lam_init)
        x = _out_ffn(x, oTa, oTb, mod[l], w_out_b[l], g_ffn[l][None, :], w_gu[l], w_dn[l], ts=ts)
    return _final_norm(x, modf, g_final[None, :], ts=ts)


def kernel(x_prompt, x_sample, c_prompt, c_sample, w_ada, b_ada, g_mix, w_in, g_q_a, g_k_a, lam_q1, lam_k1, lam_q2, lam_k2, g_subln, w_out, g_ffn, w_gate_up, w_down, w_ada_final, b_ada_final, g_final):
    nb = x_prompt.shape[0]
    x = jnp.concatenate([x_prompt, x_sample], axis=0)
    c = jnp.concatenate([c_prompt, c_sample], axis=0)
    y = _trunk(x, c, w_ada, b_ada, g_mix, w_in, g_q_a, g_k_a, lam_q1, lam_k1, lam_q2, lam_k2, g_subln,
               w_out, g_ffn, w_gate_up, w_down, w_ada_final, b_ada_final, g_final, ts=512, tq=256, tk=256)
    return (y[:nb], y[nb:])
```

```python
import functools
import math

import jax
import jax.numpy as jnp
from jax import lax
from jax.experimental import pallas as pl
from jax.experimental.pallas import tpu as pltpu

HEAD_DIM = 64
A_HEADS = 8
A_KV_HEADS = 2
A_GROUP = A_HEADS // A_KV_HEADS
B_HEADS = 4
B_V_DIM = 2 * HEAD_DIM
A_Q = A_HEADS * HEAD_DIM
A_KV = A_KV_HEADS * HEAD_DIM
B_QK = B_HEADS * 2 * HEAD_DIM
B_V = B_HEADS * B_V_DIM
IN_WIDTH = A_Q + 2 * A_KV + 2 * B_QK + B_V
GRID_W = 64
AXIAL_THETA = 10000.0
ROPE_THETA = 500000.0
PARTIAL_ROT = HEAD_DIM // 4
NORM_EPS = 1e-6
QK_SCALE = HEAD_DIM ** -0.5 * math.log2(math.e)

_QA0, _KA0, _VA0 = 0, A_Q, A_Q + A_KV
_QB0 = A_Q + 2 * A_KV
_KB0 = _QB0 + B_QK
_VB0 = _KB0 + B_QK

_KNORM_B0, _KNORM_A0, _KNORM_ROWS = 0, 2 * B_HEADS, 16
FAST_SOFTMAX_MAX_BOUND = 48.0
B_HEADS_PER_STEP = 2
FAST_STEPS_PER_TRIP = 128
FAST_LOOKAHEAD = 4

FF_CHUNK = 256
V7X_VMEM_LIMIT_BYTES = 56 * 1024 * 1024

_BF16 = jnp.bfloat16
_F32 = jnp.float32


def _params(semantics):
    return pltpu.CompilerParams(dimension_semantics=semantics, vmem_limit_bytes=V7X_VMEM_LIMIT_BYTES)


def _once_per_row(block_shape, index_map):
    return pl.BlockSpec(block_shape, index_map, pipeline_mode=pl.Buffered(1))


def _dot(a, b):
    return jnp.dot(a, b, preferred_element_type=_F32)


def _dot_nt(a, b):
    return lax.dot_general(a, b, (((1,), (1,)), ((), ())), preferred_element_type=_F32)


def _dot_tn(a, b):
    return lax.dot_general(a, b, (((0,), (0,)), ((), ())), preferred_element_type=_F32)


def _split_bf16(x):
    hi = x.astype(_BF16)
    lo = (x - hi.astype(_F32)).astype(_BF16)
    return hi, lo


def _mod_kernel(c_ref, w_ref, b_ref, o_ref):
    c = c_ref[...]
    c_act = c * jax.nn.sigmoid(c)
    c_hi, c_lo = _split_bf16(c_act)
    w_hi, w_lo = _split_bf16(w_ref[...])
    acc = _dot(c_hi, w_hi) + (_dot(c_lo, w_hi) + _dot(c_hi, w_lo))
    o_ref[...] = acc + b_ref[...]


def _modulation(c_pad, w, b, tn=1024):
    nl, d, n = w.shape
    return pl.pallas_call(
        _mod_kernel,
        grid=(nl, n // tn),
        in_specs=[
            pl.BlockSpec((8, d), lambda l, j: (0, 0)),
            pl.BlockSpec((None, d, tn), lambda l, j: (l, 0, j)),
            pl.BlockSpec((None, 1, tn), lambda l, j: (l, 0, j)),
        ],
        out_specs=pl.BlockSpec((None, 8, tn), lambda l, j: (l, 0, j)),
        out_shape=jax.ShapeDtypeStruct((nl, 8, n), _F32),
        compiler_params=_params(("parallel", "parallel")),
        name="adaln_modulation",
    )(c_pad, w, b)


def _rms_rows(x):
    return lax.rsqrt(jnp.mean(x * x, axis=1, keepdims=True) + NORM_EPS)


def _inproj_kernel(x_ref, mod_ref, gmix_ref, wT_ref, gq_ref, gk_ref, tab_ref,
                   qTa_ref, kA_ref, vTa_ref, qTb_ref, kB_ref, vTb_ref, knorm_ref, *, tk):
    ts = x_ref.shape[0]
    x = x_ref[...]
    sh1 = mod_ref[0:1, :]
    sc1 = mod_ref[1:2, :]
    h = ((x * _rms_rows(x)) * gmix_ref[...]) * (1.0 + sc1) + sh1
    hb = h.astype(_BF16)

    def proj_t(lo, n):
        return _dot_nt(wT_ref[lo:lo + n, :], hb)

    tab = tab_ref[...]
    cos_r, sin_r, cos_c, sin_c = tab[0:16], tab[16:32], tab[32:48], tab[48:64]
    cos_p, sin_p = tab[64:72], tab[72:80]

    def head_norm(x3, g_ref):
        ms = jnp.mean(x3 * x3, axis=1, keepdims=True)
        return (x3 * lax.rsqrt(ms + NORM_EPS)) * g_ref[...][None]

    def axial(x3):
        r1, r2, c1, c2 = x3[:, 0:16], x3[:, 16:32], x3[:, 32:48], x3[:, 48:64]
        return jnp.concatenate([r1 * cos_r - r2 * sin_r, r1 * sin_r + r2 * cos_r,
                                c1 * cos_c - c2 * sin_c, c1 * sin_c + c2 * cos_c], axis=1)

    def partial(x3):
        half = PARTIAL_ROT // 2
        x1, x2, rest = x3[:, 0:half], x3[:, half:PARTIAL_ROT], x3[:, PARTIAL_ROT:]
        return jnp.concatenate([x1 * cos_p - x2 * sin_p, x1 * sin_p + x2 * cos_p, rest], axis=1)

    qa = proj_t(_QA0, A_Q).reshape(A_HEADS, HEAD_DIM, ts)
    qa = axial(head_norm(qa, gq_ref)) * QK_SCALE
    qTa_ref[...] = qa.reshape(A_Q, ts).astype(_BF16)

    def max_key_norm2(k3):
        kf = k3.astype(_BF16).astype(_F32)
        return jnp.max(jnp.sum(kf * kf, axis=1), axis=1, keepdims=True)

    one_hot_rows = (lax.broadcasted_iota(jnp.int32, (HEAD_DIM, ts), 0) == 0).astype(_F32)

    def widened_keys_t(k3):
        parts = []
        for hh in range(k3.shape[0]):
            parts += [k3[hh], one_hot_rows]
        return jnp.concatenate(parts, axis=0).T.astype(_BF16)

    ka = axial(head_norm(proj_t(_KA0, A_KV).reshape(A_KV_HEADS, HEAD_DIM, ts), gk_ref))
    kA_ref[...] = widened_keys_t(ka)

    va = proj_t(_VA0, A_KV).astype(_BF16)
    for g in range(A_KV_HEADS):
        for c in range(ts // tk):
            vTa_ref[g, c] = va[g * HEAD_DIM:(g + 1) * HEAD_DIM, c * tk:(c + 1) * tk]

    qb = proj_t(_QB0, B_QK).reshape(2 * B_HEADS, HEAD_DIM, ts)
    qTb_ref[...] = (partial(qb) * QK_SCALE).reshape(B_QK, ts).astype(_BF16)

    kb = partial(proj_t(_KB0, B_QK).reshape(2 * B_HEADS, HEAD_DIM, ts))
    kB_ref[...] = widened_keys_t(kb)

    vb = proj_t(_VB0, B_V).astype(_BF16)
    for hd in range(B_HEADS):
        for c in range(ts // tk):
            vTb_ref[hd, c] = vb[hd * B_V_DIM:(hd + 1) * B_V_DIM, c * tk:(c + 1) * tk]

    @pl.when(pl.program_id(1) == 0)
    def _():
        knorm_ref[...] = jnp.zeros(knorm_ref.shape, _F32)

    lanes = knorm_ref.shape[1]
    b_rows = slice(_KNORM_B0, _KNORM_B0 + 2 * B_HEADS)
    a_rows = slice(_KNORM_A0, _KNORM_A0 + A_KV_HEADS)
    knorm_ref[b_rows, :] = jnp.maximum(knorm_ref[b_rows, :], jnp.broadcast_to(max_key_norm2(kb), (2 * B_HEADS, lanes)))
    knorm_ref[a_rows, :] = jnp.maximum(knorm_ref[a_rows, :], jnp.broadcast_to(max_key_norm2(ka), (A_KV_HEADS, lanes)))


def _inproj(x, mod, gmix, w_inT, gq, gk, tab, *, ts, tk):
    n, s, d = x.shape
    nck = s // tk
    cpt = ts // tk
    out_shape = (
        jax.ShapeDtypeStruct((n, A_Q, s), _BF16),
        jax.ShapeDtypeStruct((n, s, 2 * A_KV), _BF16),
        jax.ShapeDtypeStruct((n, A_KV_HEADS, nck, HEAD_DIM, tk), _BF16),
        jax.ShapeDtypeStruct((n, B_QK, s), _BF16),
        jax.ShapeDtypeStruct((n, s, 2 * B_QK), _BF16),
        jax.ShapeDtypeStruct((n, B_HEADS, nck, B_V_DIM, tk), _BF16),
        jax.ShapeDtypeStruct((n, _KNORM_ROWS, 128), _F32),
    )
    return pl.pallas_call(
        functools.partial(_inproj_kernel, tk=tk),
        grid=(n, s // ts),
        in_specs=[
            pl.BlockSpec((None, ts, d), lambda b, t: (b, t, 0)),
            pl.BlockSpec((None, 6, d), lambda b, t: (b, 0, 0)),
            pl.BlockSpec((1, d), lambda b, t: (0, 0)),
            pl.BlockSpec((IN_WIDTH, d), lambda b, t: (0, 0)),
            pl.BlockSpec((HEAD_DIM, 1), lambda b, t: (0, 0)),
            pl.BlockSpec((HEAD_DIM, 1), lambda b, t: (0, 0)),
            pl.BlockSpec((80, ts), lambda b, t: (0, t)),
        ],
        out_specs=(
            pl.BlockSpec((None, A_Q, ts), lambda b, t: (b, 0, t)),
            pl.BlockSpec((None, ts, 2 * A_KV), lambda b, t: (b, t, 0)),
            pl.BlockSpec((None, A_KV_HEADS, cpt, HEAD_DIM, tk), lambda b, t: (b, 0, t, 0, 0)),
            pl.BlockSpec((None, B_QK, ts), lambda b, t: (b, 0, t)),
            pl.BlockSpec((None, ts, 2 * B_QK), lambda b, t: (b, t, 0)),
            pl.BlockSpec((None, B_HEADS, cpt, B_V_DIM, tk), lambda b, t: (b, 0, t, 0, 0)),
            pl.BlockSpec((None, _KNORM_ROWS, 128), lambda b, t: (b, 0, 0)),
        ),
        out_shape=out_shape,
        compiler_params=_params(("parallel", "arbitrary")),
        name="mix_in_projection",
    )(x, mod, gmix, w_inT, gq, gk, tab)


def _softmax_value_loop(qT_ref, k_ref, vT_ref, knorm2_of_chain, key_head_of_chain, value_head_of_chain, w_ref,
                        s_even_ref, s_odd_ref, m_ref, l_ref, acc_ref, *, nchains, tk):
    nck = vT_ref.shape[1]
    tq = qT_ref.shape[-1]
    acc_ref[...] = jnp.zeros(acc_ref.shape, _F32)
    l_ref[...] = jnp.zeros(l_ref.shape, _F32)

    qs = [qT_ref[c * HEAD_DIM:(c + 1) * HEAD_DIM, :] for c in range(nchains)]
    bounds = []
    for c in range(nchains):
        qf = qs[c].astype(_F32)
        bounds.append(jnp.sqrt(jnp.sum(qf * qf, axis=0, keepdims=True) * knorm2_of_chain(c)))
    use_bound = jnp.max(jnp.concatenate(bounds, axis=0)) <= FAST_SOFTMAX_MAX_BOUND

    first_row = lax.broadcasted_iota(jnp.int32, (HEAD_DIM, tq), 0) == 0
    for c in range(nchains):
        offset = jnp.where(use_bound, -bounds[c], 0.0)
        offset_rows = jnp.where(first_row, jnp.broadcast_to(offset, (HEAD_DIM, tq)), 0.0)
        w_ref[c] = jnp.concatenate([qs[c], offset_rows.astype(_BF16)], axis=0)

    def k_chunk(j, c):
        lane0 = key_head_of_chain(c) * 2 * HEAD_DIM
        return k_ref[pl.ds(pl.multiple_of(j * tk, tk), tk), lane0:lane0 + 2 * HEAD_DIM]

    def partial_sums(p):
        return jnp.sum(p.reshape(tk // 8, 8, tq), axis=0)

    @pl.when(use_bound)
    def _():
        def scores(j, step):
            dj, c = divmod(step, nchains)
            jj = j + dj
            if dj:
                jj = jnp.where(jj >= nck, jj - nck, jj)
            return _dot(k_chunk(jj, c), w_ref[c])

        for a in range(FAST_LOOKAHEAD):
            s_even_ref[a] = scores(0, a)

        chunks_per_trip = min(FAST_STEPS_PER_TRIP // nchains, nck)
        assert nck % chunks_per_trip == 0

        def body(i, carry):
            j0 = i * chunks_per_trip
            pending = [s_even_ref[a] for a in range(FAST_LOOKAHEAD)]
            for u in range(chunks_per_trip):
                for c in range(nchains):
                    pending.append(scores(j0, u * nchains + c + FAST_LOOKAHEAD))
                    p = jnp.exp2(pending.pop(0))
                    l_ref[c] += partial_sums(p)
                    acc_ref[c] += _dot(vT_ref[value_head_of_chain(c), j0 + u], p.astype(_BF16))
            for a in range(FAST_LOOKAHEAD):
                s_even_ref[a] = pending[a]
            return carry

        lax.fori_loop(0, nck // chunks_per_trip, body, 0)

    @pl.when(jnp.logical_not(use_bound))
    def _():
        m_ref[...] = jnp.full(m_ref.shape, -jnp.inf, _F32)

        def phase(j_cur, j_next, s_cur_ref, s_next_ref):
            for c in range(nchains):
                vc = vT_ref[value_head_of_chain(c), j_cur]
                s = s_cur_ref[c]
                s_next_ref[c] = _dot(k_chunk(j_next, c), w_ref[c])
                m_prev = m_ref[c]
                m_new = jnp.maximum(m_prev, jnp.max(s, axis=0, keepdims=True))
                alpha = jnp.exp2(m_prev - m_new)
                p = jnp.exp2(s - m_new)
                l_ref[c] = alpha * l_ref[c] + partial_sums(p)
                acc_ref[c] = alpha * acc_ref[c] + _dot(vc, p.astype(_BF16))
                m_ref[c] = m_new

        for c in range(nchains):
            s_even_ref[c] = _dot(k_chunk(0, c), w_ref[c])

        def body(i, carry):
            j = 2 * i
            phase(j, j + 1, s_even_ref, s_odd_ref)
            phase(j + 1, jnp.where(j + 2 == nck, 0, j + 2), s_odd_ref, s_even_ref)
            return carry

        lax.fori_loop(0, nck // 2, body, 0)


def _denominator(l_ref, c):
    return jnp.sum(l_ref[c], axis=0, keepdims=True)


def _attn_a_kernel(qT_ref, k_ref, vT_ref, knorm_ref, o_ref, w_ref, s_even_ref, s_odd_ref, m_ref, l_ref, acc_ref,
                   *, tk):
    knorm2 = knorm_ref[_KNORM_A0:_KNORM_A0 + A_KV_HEADS, 0:1]
    kv_head = lambda c: c // A_GROUP
    _softmax_value_loop(qT_ref, k_ref, vT_ref, lambda c: knorm2[kv_head(c):kv_head(c) + 1], kv_head, kv_head,
                        w_ref, s_even_ref, s_odd_ref, m_ref, l_ref, acc_ref, nchains=A_HEADS, tk=tk)
    for c in range(A_HEADS):
        o_ref[c * HEAD_DIM:(c + 1) * HEAD_DIM, :] = (acc_ref[c] / _denominator(l_ref, c)).astype(_BF16)


def _attention_scratch(nchains, dv, tq, tk):
    return [
        pltpu.VMEM((nchains, 2 * HEAD_DIM, tq), _BF16),
        pltpu.VMEM((max(nchains, FAST_LOOKAHEAD), tk, tq), _F32),
        pltpu.VMEM((nchains, tk, tq), _F32),
        pltpu.VMEM((nchains, 1, tq), _F32),
        pltpu.VMEM((nchains, 8, tq), _F32),
        pltpu.VMEM((nchains, dv, tq), _F32),
    ]


def _attn_a(qTa, kA, vTa, knorm, *, tq):
    n, _, s = qTa.shape
    _, nkv, nck, dv, tk = vTa.shape
    return pl.pallas_call(
        functools.partial(_attn_a_kernel, tk=tk),
        grid=(n, s // tq),
        in_specs=[
            pl.BlockSpec((None, A_Q, tq), lambda b, i: (b, 0, i)),
            _once_per_row((None, s, nkv * 2 * HEAD_DIM), lambda b, i: (b, 0, 0)),
            _once_per_row((None, nkv, nck, dv, tk), lambda b, i: (b, 0, 0, 0, 0)),
            pl.BlockSpec((None, _KNORM_ROWS, 128), lambda b, i: (b, 0, 0)),
        ],
        out_specs=pl.BlockSpec((None, A_Q, tq), lambda b, i: (b, 0, i)),
        out_shape=jax.ShapeDtypeStruct((n, A_Q, s), _BF16),
        scratch_shapes=_attention_scratch(A_HEADS, HEAD_DIM, tq, tk),
        compiler_params=_params(("parallel", "parallel")),
        name="gqa_attention",
    )(qTa, kA, vTa, knorm)


def _attn_b_kernel(qT_ref, k_ref, vT_ref, knorm_ref, lam_ref, gsub_ref, o_ref, w_ref, s_even_ref, s_odd_ref,
                   m_ref, l_ref, acc_ref, *, tk, lam_init):
    nchains = 2 * B_HEADS_PER_STEP
    knorm2 = knorm_ref[pl.ds(_KNORM_B0 + nchains * pl.program_id(1), nchains), 0:1]
    _softmax_value_loop(qT_ref, k_ref, vT_ref, lambda c: knorm2[c:c + 1], lambda c: c, lambda c: c // 2,
                        w_ref, s_even_ref, s_odd_ref, m_ref, l_ref, acc_ref, nchains=nchains, tk=tk)

    lp = lam_ref[...]
    lam = (jnp.exp(jnp.sum(lp[0:1] * lp[1:2], axis=1, keepdims=True))
           - jnp.exp(jnp.sum(lp[2:3] * lp[3:4], axis=1, keepdims=True)) + lam_init)
    for h in range(B_HEADS_PER_STEP):
        o = (acc_ref[2 * h] / _denominator(l_ref, 2 * h)
             - lam * (acc_ref[2 * h + 1] / _denominator(l_ref, 2 * h + 1)))
        ms = jnp.mean(o * o, axis=0, keepdims=True)
        o = ((o * lax.rsqrt(ms + NORM_EPS)) * gsub_ref[...]) * (1.0 - lam_init)
        o_ref[h * B_V_DIM:(h + 1) * B_V_DIM, :] = o.astype(_BF16)


def _attn_b(qTb, kB, vTb, knorm, lam_params, gsub, *, tq, lam_init):
    n, _, s = qTb.shape
    _, _, nck, dv, tk = vTb.shape
    hps = B_HEADS_PER_STEP
    return pl.pallas_call(
        functools.partial(_attn_b_kernel, tk=tk, lam_init=lam_init),
        grid=(n, B_HEADS // hps, s // tq),
        in_specs=[
            pl.BlockSpec((None, hps * 2 * HEAD_DIM, tq), lambda b, hp, i: (b, hp, i)),
            _once_per_row((None, s, hps * 4 * HEAD_DIM), lambda b, hp, i: (b, 0, hp)),
            _once_per_row((None, hps, nck, dv, tk), lambda b, hp, i: (b, hp, 0, 0, 0)),
            pl.BlockSpec((None, _KNORM_ROWS, 128), lambda b, hp, i: (b, 0, 0)),
            pl.BlockSpec((4, HEAD_DIM), lambda b, hp, i: (0, 0)),
            pl.BlockSpec((B_V_DIM, 1), lambda b, hp, i: (0, 0)),
        ],
        out_specs=pl.BlockSpec((None, hps * B_V_DIM, tq), lambda b, hp, i: (b, hp, i)),
        out_shape=jax.ShapeDtypeStruct((n, B_V, s), _BF16),
        scratch_shapes=_attention_scratch(2 * hps, B_V_DIM, tq, tk),
        compiler_params=_params(("parallel", "parallel", "parallel")),
        name="diff_attention",
    )(qTb, kB, vTb, knorm, lam_params, gsub)


def _mix_and_ffn(x_ref, oTa_ref, oTb_ref, mod_ref, wout_ref, gffn_ref, wgu_ref, wdn_ref, act_ref):
    mixed = _dot_tn(oTa_ref[...], wout_ref[0:A_Q, :]) + _dot_tn(oTb_ref[...], wout_ref[A_Q:, :])
    gt1, sh2, sc2, gt2 = mod_ref[2:3, :], mod_ref[3:4, :], mod_ref[4:5, :], mod_ref[5:6, :]
    x1 = x_ref[...] + gt1 * mixed
    h = ((x1 * _rms_rows(x1)) * gffn_ref[...]) * (1.0 + sc2) + sh2
    hb = h.astype(_BF16)
    for c in range(wgu_ref.shape[0]):
        gu = _dot(hb, wgu_ref[c])
        gate, up = gu[:, :FF_CHUNK], gu[:, FF_CHUNK:]
        act_ref[:, c * FF_CHUNK:(c + 1) * FF_CHUNK] = ((gate * jax.nn.sigmoid(gate)) * up).astype(_BF16)
    return x1 + gt2 * _dot(act_ref[...], wdn_ref[...])


def _ffn_kernel(x_ref, oTa_ref, oTb_ref, mod_ref, wout_ref, gffn_ref, wgu_ref, wdn_ref, y_ref, act_ref):
    y_ref[...] = _mix_and_ffn(x_ref, oTa_ref, oTb_ref, mod_ref, wout_ref, gffn_ref, wgu_ref, wdn_ref, act_ref)


def _ffn_final_kernel(x_ref, oTa_ref, oTb_ref, mod_ref, wout_ref, gffn_ref, wgu_ref, wdn_ref, modf_ref, gfin_ref,
                      y_ref, act_ref):
    x = _mix_and_ffn(x_ref, oTa_ref, oTb_ref, mod_ref, wout_ref, gffn_ref, wgu_ref, wdn_ref, act_ref)
    shift, scale = modf_ref[0:1, :], modf_ref[1:2, :]
    y_ref[...] = ((x * _rms_rows(x)) * gfin_ref[...]) * (1.0 + scale) + shift


def _resident(shape):
    return pl.BlockSpec(shape, lambda b, t: (0,) * len(shape), pipeline_mode=pl.Buffered(1))


def _out_ffn(x, oTa, oTb, mod, w_out, gffn, w_gu, w_dn, final=None, *, ts):
    n, s, d = x.shape
    nff, _, _ = w_gu.shape
    dff = w_dn.shape[0]
    in_specs = [
        pl.BlockSpec((None, ts, d), lambda b, t: (b, t, 0)),
        pl.BlockSpec((None, A_Q, ts), lambda b, t: (b, 0, t)),
        pl.BlockSpec((None, B_V, ts), lambda b, t: (b, 0, t)),
        pl.BlockSpec((None, 6, d), lambda b, t: (b, 0, 0)),
        _resident((A_Q + B_V, d)),
        pl.BlockSpec((1, d), lambda b, t: (0, 0)),
        _resident((nff, d, 2 * FF_CHUNK)),
        _resident((dff, d)),
    ]
    operands = (x, oTa, oTb, mod, w_out, gffn, w_gu, w_dn)
    if final is not None:
        in_specs += [pl.BlockSpec((None, 2, d), lambda b, t: (b, 0, 0)), pl.BlockSpec((1, d), lambda b, t: (0, 0))]
        operands += tuple(final)
    return pl.pallas_call(
        _ffn_kernel if final is None else _ffn_final_kernel,
        grid=(n, s // ts),
        in_specs=in_specs,
        out_specs=pl.BlockSpec((None, ts, d), lambda b, t: (b, t, 0)),
        out_shape=jax.ShapeDtypeStruct((n, s, d), _F32),
        scratch_shapes=[pltpu.VMEM((ts, dff), _BF16)],
        compiler_params=_params(("parallel", "parallel")),
        name="out_proj_swiglu",
    )(*operands)


def _rope_table_t(s):
    def tables(pos, dim, theta):
        inv = 1.0 / (theta ** (jnp.arange(0, dim, 2, dtype=_F32) / dim))
        ang = pos.astype(_F32)[:, None] * inv[None, :]
        return jnp.cos(ang).T, jnp.sin(ang).T

    t = jnp.arange(s)
    cos_r, sin_r = tables(t // GRID_W, HEAD_DIM // 2, AXIAL_THETA)
    cos_c, sin_c = tables(t % GRID_W, HEAD_DIM // 2, AXIAL_THETA)
    cos_p, sin_p = tables(t, PARTIAL_ROT, ROPE_THETA)
    return jnp.concatenate([cos_r, sin_r, cos_c, sin_c, cos_p, sin_p], axis=0)


def _trunks(xs, cs, w_ada, b_ada, g_mix, w_in, g_q_a, g_k_a, lam_q1, lam_k1, lam_q2, lam_k2, g_subln,
            w_out, g_ffn, w_gate_up, w_down, w_ada_final, b_ada_final, g_final, *, ts, tq, tk):
    d = xs[0].shape[-1]
    depth = w_ada.shape[0]
    dff = w_down.shape[1]
    nff = dff // FF_CHUNK

    rows = [c.shape[0] for c in cs]
    c_pad = jnp.zeros((8, d), _F32).at[:sum(rows)].set(jnp.concatenate(cs, axis=0))
    mod_all = _modulation(c_pad, w_ada, b_ada[:, None, :])
    modf_all = _modulation(c_pad, w_ada_final[None], b_ada_final[None, None, :])[0]

    w_inT = jnp.swapaxes(w_in, 1, 2).astype(_BF16)
    w_out_b = w_out.astype(_BF16)
    w_gu = jnp.concatenate([w_gate_up[:, :, :dff].reshape(depth, d, nff, FF_CHUNK),
                            w_gate_up[:, :, dff:].reshape(depth, d, nff, FF_CHUNK)], axis=-1)
    w_gu = jnp.transpose(w_gu, (0, 2, 1, 3)).astype(_BF16)
    w_dn = w_down.astype(_BF16)
    lam_params = jnp.stack([lam_q1, lam_k1, lam_q2, lam_k2], axis=1)

    outs = []
    row0 = 0
    for x, n in zip(xs, rows):
        s = x.shape[1]
        mod = mod_all[:, row0:row0 + n].reshape(depth, n, 6, d)
        modf = modf_all[row0:row0 + n].reshape(n, 2, d)
        row0 += n
        tab = _rope_table_t(s)
        for l in range(depth):
            qTa, kA, vTa, qTb, kB, vTb, knorm = _inproj(
                x, mod[l], g_mix[l][None, :], w_inT[l], g_q_a[l][:, None], g_k_a[l][:, None], tab, ts=ts, tk=tk)
            oTa = _attn_a(qTa, kA, vTa, knorm, tq=tq)
            lam_init = 0.8 - 0.6 * math.exp(-0.3 * l)
            oTb = _attn_b(qTb, kB, vTb, knorm, lam_params[l], g_subln[l][:, None], tq=tq, lam_init=lam_init)
            final = (modf, g_final[None, :]) if l == depth - 1 else None
            x = _out_ffn(x, oTa, oTb, mod[l], w_out_b[l], g_ffn[l][None, :], w_gu[l], w_dn[l], final, ts=ts)
        outs.append(x)
    return tuple(outs)


def _trunk(x, c, *weights, ts, tq, tk):
    return _trunks((x,), (c,), *weights, ts=ts, tq=tq, tk=tk)[0]


def kernel(x_prompt, x_sample, c_prompt, c_sample, w_ada, b_ada, g_mix, w_in, g_q_a, g_k_a, lam_q1, lam_k1, lam_q2, lam_k2, g_subln, w_out, g_ffn, w_gate_up, w_down, w_ada_final, b_ada_final, g_final):
    return _trunks((x_prompt, x_sample), (c_prompt, c_sample), w_ada, b_ada, g_mix, w_in, g_q_a, g_k_a,
                   lam_q1, lam_k1, lam_q2, lam_k2, g_subln, w_out, g_ffn, w_gate_up, w_down,
                   w_ada_final, b_ada_final, g_final, ts=512, tq=256, tk=256)
```

```python
import functools
import math

import jax
import jax.numpy as jnp
from jax import lax
from jax.experimental import pallas as pl
from jax.experimental.pallas import tpu as pltpu

HEAD_DIM = 64
A_HEADS = 8
A_KV_HEADS = 2
A_GROUP = A_HEADS // A_KV_HEADS
B_HEADS = 4
B_V_DIM = 2 * HEAD_DIM
A_Q = A_HEADS * HEAD_DIM
A_KV = A_KV_HEADS * HEAD_DIM
B_QK = B_HEADS * 2 * HEAD_DIM
B_V = B_HEADS * B_V_DIM
IN_WIDTH = A_Q + 2 * A_KV + 2 * B_QK + B_V
GRID_W = 64
AXIAL_THETA = 10000.0
ROPE_THETA = 500000.0
PARTIAL_ROT = HEAD_DIM // 4
NORM_EPS = 1e-6
QK_SCALE = HEAD_DIM ** -0.5 * math.log2(math.e)

_QA0, _KA0, _VA0 = 0, A_Q, A_Q + A_KV
_QB0 = A_Q + 2 * A_KV
_KB0 = _QB0 + B_QK
_VB0 = _KB0 + B_QK

_KNORM_B0, _KNORM_A0, _KNORM_ROWS = 0, 2 * B_HEADS, 16
FAST_SOFTMAX_MAX_BOUND = 48.0
B_HEADS_PER_STEP = 2
FAST_STEPS_PER_TRIP = 256
FAST_LOOKAHEAD = 4

FF_CHUNK = 256
V7X_VMEM_LIMIT_BYTES = 56 * 1024 * 1024

_BF16 = jnp.bfloat16
_F32 = jnp.float32


def _params(semantics):
    return pltpu.CompilerParams(dimension_semantics=semantics, vmem_limit_bytes=V7X_VMEM_LIMIT_BYTES)


def _once_per_row(block_shape, index_map):
    return pl.BlockSpec(block_shape, index_map, pipeline_mode=pl.Buffered(1))


def _dot(a, b):
    return jnp.dot(a, b, preferred_element_type=_F32)


def _dot_nt(a, b):
    return lax.dot_general(a, b, (((1,), (1,)), ((), ())), preferred_element_type=_F32)


def _dot_tn(a, b):
    return lax.dot_general(a, b, (((0,), (0,)), ((), ())), preferred_element_type=_F32)


def _split_bf16(x):
    hi = x.astype(_BF16)
    lo = (x - hi.astype(_F32)).astype(_BF16)
    return hi, lo


def _mod_kernel(c_ref, w_ref, b_ref, o_ref):
    c = c_ref[...]
    c_act = c * jax.nn.sigmoid(c)
    c_hi, c_lo = _split_bf16(c_act)
    w_hi, w_lo = _split_bf16(w_ref[...])
    acc = _dot(c_hi, w_hi) + (_dot(c_lo, w_hi) + _dot(c_hi, w_lo))
    o_ref[...] = acc + b_ref[...]


def _modulation(c_pad, w, b, tn=1024):
    nl, d, n = w.shape
    return pl.pallas_call(
        _mod_kernel,
        grid=(nl, n // tn),
        in_specs=[
            pl.BlockSpec((8, d), lambda l, j: (0, 0)),
            pl.BlockSpec((None, d, tn), lambda l, j: (l, 0, j)),
            pl.BlockSpec((None, 1, tn), lambda l, j: (l, 0, j)),
        ],
        out_specs=pl.BlockSpec((None, 8, tn), lambda l, j: (l, 0, j)),
        out_shape=jax.ShapeDtypeStruct((nl, 8, n), _F32),
        compiler_params=_params(("parallel", "parallel")),
        name="adaln_modulation",
    )(c_pad, w, b)


def _rms_rows(x):
    return lax.rsqrt(jnp.mean(x * x, axis=1, keepdims=True) + NORM_EPS)


def _inproj_kernel(x_ref, mod_ref, gmix_ref, wT_ref, gq_ref, gk_ref, tab_ref,
                   qTa_ref, kA_ref, vTa_ref, qTb_ref, kB_ref, vTb_ref, knorm_ref, *, tk):
    ts = x_ref.shape[0]
    x = x_ref[...]
    sh1 = mod_ref[0:1, :]
    sc1 = mod_ref[1:2, :]
    h = ((x * _rms_rows(x)) * gmix_ref[...]) * (1.0 + sc1) + sh1
    hb = h.astype(_BF16)

    def proj_t(lo, n):
        return _dot_nt(wT_ref[lo:lo + n, :], hb)

    tab = tab_ref[...]
    cos_r, sin_r, cos_c, sin_c = tab[0:16], tab[16:32], tab[32:48], tab[48:64]
    cos_p, sin_p = tab[64:72], tab[72:80]

    def head_norm(x3, g_ref):
        ms = jnp.mean(x3 * x3, axis=1, keepdims=True)
        return (x3 * lax.rsqrt(ms + NORM_EPS)) * g_ref[...][None]

    def axial(x3):
        r1, r2, c1, c2 = x3[:, 0:16], x3[:, 16:32], x3[:, 32:48], x3[:, 48:64]
        return jnp.concatenate([r1 * cos_r - r2 * sin_r, r1 * sin_r + r2 * cos_r,
                                c1 * cos_c - c2 * sin_c, c1 * sin_c + c2 * cos_c], axis=1)

    def partial(x3):
        half = PARTIAL_ROT // 2
        x1, x2, rest = x3[:, 0:half], x3[:, half:PARTIAL_ROT], x3[:, PARTIAL_ROT:]
        return jnp.concatenate([x1 * cos_p - x2 * sin_p, x1 * sin_p + x2 * cos_p, rest], axis=1)

    qa = proj_t(_QA0, A_Q).reshape(A_HEADS, HEAD_DIM, ts)
    qa = axial(head_norm(qa, gq_ref)) * QK_SCALE
    qTa_ref[...] = qa.reshape(A_Q, ts).astype(_BF16)

    def max_key_norm2(k3):
        kf = k3.astype(_BF16).astype(_F32)
        return jnp.max(jnp.sum(kf * kf, axis=1), axis=1, keepdims=True)

    one_hot_rows = (lax.broadcasted_iota(jnp.int32, (HEAD_DIM, ts), 0) == 0).astype(_F32)

    def widened_keys_t(k3):
        parts = []
        for hh in range(k3.shape[0]):
            parts += [k3[hh], one_hot_rows]
        return jnp.concatenate(parts, axis=0).T.astype(_BF16)

    ka = axial(head_norm(proj_t(_KA0, A_KV).reshape(A_KV_HEADS, HEAD_DIM, ts), gk_ref))
    kA_ref[...] = widened_keys_t(ka)

    va = proj_t(_VA0, A_KV).astype(_BF16)
    for g in range(A_KV_HEADS):
        for c in range(ts // tk):
            vTa_ref[g, c] = va[g * HEAD_DIM:(g + 1) * HEAD_DIM, c * tk:(c + 1) * tk]

    qb = proj_t(_QB0, B_QK).reshape(2 * B_HEADS, HEAD_DIM, ts)
    qTb_ref[...] = (partial(qb) * QK_SCALE).reshape(B_QK, ts).astype(_BF16)

    kb = partial(proj_t(_KB0, B_QK).reshape(2 * B_HEADS, HEAD_DIM, ts))
    kB_ref[...] = widened_keys_t(kb)

    vb = proj_t(_VB0, B_V).astype(_BF16)
    for hd in range(B_HEADS):
        for c in range(ts // tk):
            vTb_ref[hd, c] = vb[hd * B_V_DIM:(hd + 1) * B_V_DIM, c * tk:(c + 1) * tk]

    @pl.when(pl.program_id(1) == 0)
    def _():
        knorm_ref[...] = jnp.zeros(knorm_ref.shape, _F32)

    lanes = knorm_ref.shape[1]
    b_rows = slice(_KNORM_B0, _KNORM_B0 + 2 * B_HEADS)
    a_rows = slice(_KNORM_A0, _KNORM_A0 + A_KV_HEADS)
    knorm_ref[b_rows, :] = jnp.maximum(knorm_ref[b_rows, :], jnp.broadcast_to(max_key_norm2(kb), (2 * B_HEADS, lanes)))
    knorm_ref[a_rows, :] = jnp.maximum(knorm_ref[a_rows, :], jnp.broadcast_to(max_key_norm2(ka), (A_KV_HEADS, lanes)))


def _inproj(x, mod, gmix, w_inT, gq, gk, tab, *, ts, tk):
    n, s, d = x.shape
    nck = s // tk
    cpt = ts // tk
    out_shape = (
        jax.ShapeDtypeStruct((n, A_Q, s), _BF16),
        jax.ShapeDtypeStruct((n, s, 2 * A_KV), _BF16),
        jax.ShapeDtypeStruct((n, A_KV_HEADS, nck, HEAD_DIM, tk), _BF16),
        jax.ShapeDtypeStruct((n, B_QK, s), _BF16),
        jax.ShapeDtypeStruct((n, s, 2 * B_QK), _BF16),
        jax.ShapeDtypeStruct((n, B_HEADS, nck, B_V_DIM, tk), _BF16),
        jax.ShapeDtypeStruct((n, _KNORM_ROWS, 128), _F32),
    )
    return pl.pallas_call(
        functools.partial(_inproj_kernel, tk=tk),
        grid=(n, s // ts),
        in_specs=[
            pl.BlockSpec((None, ts, d), lambda b, t: (b, t, 0)),
            pl.BlockSpec((None, 6, d), lambda b, t: (b, 0, 0)),
            pl.BlockSpec((1, d), lambda b, t: (0, 0)),
            pl.BlockSpec((IN_WIDTH, d), lambda b, t: (0, 0)),
            pl.BlockSpec((HEAD_DIM, 1), lambda b, t: (0, 0)),
            pl.BlockSpec((HEAD_DIM, 1), lambda b, t: (0, 0)),
            pl.BlockSpec((80, ts), lambda b, t: (0, t)),
        ],
        out_specs=(
            pl.BlockSpec((None, A_Q, ts), lambda b, t: (b, 0, t)),
            pl.BlockSpec((None, ts, 2 * A_KV), lambda b, t: (b, t, 0)),
            pl.BlockSpec((None, A_KV_HEADS, cpt, HEAD_DIM, tk), lambda b, t: (b, 0, t, 0, 0)),
            pl.BlockSpec((None, B_QK, ts), lambda b, t: (b, 0, t)),
            pl.BlockSpec((None, ts, 2 * B_QK), lambda b, t: (b, t, 0)),
            pl.BlockSpec((None, B_HEADS, cpt, B_V_DIM, tk), lambda b, t: (b, 0, t, 0, 0)),
            pl.BlockSpec((None, _KNORM_ROWS, 128), lambda b, t: (b, 0, 0)),
        ),
        out_shape=out_shape,
        compiler_params=_params(("parallel", "arbitrary")),
        name="mix_in_projection",
    )(x, mod, gmix, w_inT, gq, gk, tab)


def _softmax_value_loop(qT_ref, k_ref, vT_ref, knorm2_of_chain, key_head_of_chain, value_head_of_chain, w_ref,
                        s_even_ref, s_odd_ref, m_ref, l_ref, acc_ref, *, nchains, tk):
    nck = vT_ref.shape[1]
    tq = qT_ref.shape[-1]
    acc_ref[...] = jnp.zeros(acc_ref.shape, _F32)
    l_ref[...] = jnp.zeros(l_ref.shape, _F32)

    qs = [qT_ref[c * HEAD_DIM:(c + 1) * HEAD_DIM, :] for c in range(nchains)]
    bounds = []
    for c in range(nchains):
        qf = qs[c].astype(_F32)
        bounds.append(jnp.sqrt(jnp.sum(qf * qf, axis=0, keepdims=True) * knorm2_of_chain(c)))
    use_bound = jnp.max(jnp.concatenate(bounds, axis=0)) <= FAST_SOFTMAX_MAX_BOUND

    first_row = lax.broadcasted_iota(jnp.int32, (HEAD_DIM, tq), 0) == 0
    for c in range(nchains):
        offset = -jnp.minimum(bounds[c], FAST_SOFTMAX_MAX_BOUND)
        offset_rows = jnp.where(first_row, jnp.broadcast_to(offset, (HEAD_DIM, tq)), 0.0)
        w_ref[c] = jnp.concatenate([qs[c], offset_rows.astype(_BF16)], axis=0)

    def k_chunk(j, c):
        lane0 = key_head_of_chain(c) * 2 * HEAD_DIM
        return k_ref[pl.ds(pl.multiple_of(j * tk, tk), tk), lane0:lane0 + 2 * HEAD_DIM]

    def partial_sums(p):
        return jnp.sum(p.reshape(tk // 8, 8, tq), axis=0)

    assert nchains >= FAST_LOOKAHEAD
    for a in range(FAST_LOOKAHEAD):
        s_even_ref[a] = _dot(k_chunk(0, a), w_ref[a])

    @pl.when(use_bound)
    def _():
        def scores(j, step):
            dj, c = divmod(step, nchains)
            jj = j + dj
            if dj:
                jj = jnp.where(jj >= nck, jj - nck, jj)
            return _dot(k_chunk(jj, c), w_ref[c])

        chunks_per_trip = min(FAST_STEPS_PER_TRIP // nchains, nck)
        assert nck % chunks_per_trip == 0

        def body(i, carry):
            j0 = i * chunks_per_trip
            pending = [s_even_ref[a] for a in range(FAST_LOOKAHEAD)]
            for u in range(chunks_per_trip):
                for c in range(nchains):
                    pending.append(scores(j0, u * nchains + c + FAST_LOOKAHEAD))
                    p = jnp.exp2(pending.pop(0))
                    l_ref[c] += partial_sums(p)
                    acc_ref[c] += _dot(vT_ref[value_head_of_chain(c), j0 + u], p.astype(_BF16))
            for a in range(FAST_LOOKAHEAD):
                s_even_ref[a] = pending[a]
            return carry

        lax.fori_loop(0, nck // chunks_per_trip, body, 0)

    @pl.when(jnp.logical_not(use_bound))
    def _():
        m_ref[...] = jnp.full(m_ref.shape, -jnp.inf, _F32)

        def phase(j_cur, j_next, s_cur_ref, s_next_ref):
            for c in range(nchains):
                vc = vT_ref[value_head_of_chain(c), j_cur]
                s = s_cur_ref[c]
                s_next_ref[c] = _dot(k_chunk(j_next, c), w_ref[c])
                m_prev = m_ref[c]
                m_new = jnp.maximum(m_prev, jnp.max(s, axis=0, keepdims=True))
                alpha = jnp.exp2(m_prev - m_new)
                p = jnp.exp2(s - m_new)
                l_ref[c] = alpha * l_ref[c] + partial_sums(p)
                acc_ref[c] = alpha * acc_ref[c] + _dot(vc, p.astype(_BF16))
                m_ref[c] = m_new

        for c in range(FAST_LOOKAHEAD, nchains):
            s_even_ref[c] = _dot(k_chunk(0, c), w_ref[c])

        def body(i, carry):
            j = 2 * i
            phase(j, j + 1, s_even_ref, s_odd_ref)
            phase(j + 1, jnp.where(j + 2 == nck, 0, j + 2), s_odd_ref, s_even_ref)
            return carry

        lax.fori_loop(0, nck // 2, body, 0)


def _denominator(l_ref, c):
    return jnp.sum(l_ref[c], axis=0, keepdims=True)


def _attn_a_kernel(qT_ref, k_ref, vT_ref, knorm_ref, o_ref, w_ref, s_even_ref, s_odd_ref, m_ref, l_ref, acc_ref,
                   *, tk):
    knorm2 = knorm_ref[_KNORM_A0:_KNORM_A0 + A_KV_HEADS, 0:1]
    kv_head = lambda c: c // A_GROUP
    _softmax_value_loop(qT_ref, k_ref, vT_ref, lambda c: knorm2[kv_head(c):kv_head(c) + 1], kv_head, kv_head,
                        w_ref, s_even_ref, s_odd_ref, m_ref, l_ref, acc_ref, nchains=A_HEADS, tk=tk)
    for c in range(A_HEADS):
        o_ref[c * HEAD_DIM:(c + 1) * HEAD_DIM, :] = (acc_ref[c] / _denominator(l_ref, c)).astype(_BF16)


def _attention_scratch(nchains, dv, tq, tk):
    return [
        pltpu.VMEM((nchains, 2 * HEAD_DIM, tq), _BF16),
        pltpu.VMEM((max(nchains, FAST_LOOKAHEAD), tk, tq), _F32),
        pltpu.VMEM((nchains, tk, tq), _F32),
        pltpu.VMEM((nchains, 1, tq), _F32),
        pltpu.VMEM((nchains, 8, tq), _F32),
        pltpu.VMEM((nchains, dv, tq), _F32),
    ]


def _attn_a(qTa, kA, vTa, knorm, *, tq):
    n, _, s = qTa.shape
    _, nkv, nck, dv, tk = vTa.shape
    return pl.pallas_call(
        functools.partial(_attn_a_kernel, tk=tk),
        grid=(n, s // tq),
        in_specs=[
            pl.BlockSpec((None, A_Q, tq), lambda b, i: (b, 0, i)),
            _once_per_row((None, s, nkv * 2 * HEAD_DIM), lambda b, i: (b, 0, 0)),
            _once_per_row((None, nkv, nck, dv, tk), lambda b, i: (b, 0, 0, 0, 0)),
            pl.BlockSpec((None, _KNORM_ROWS, 128), lambda b, i: (b, 0, 0)),
        ],
        out_specs=pl.BlockSpec((None, A_Q, tq), lambda b, i: (b, 0, i)),
        out_shape=jax.ShapeDtypeStruct((n, A_Q, s), _BF16),
        scratch_shapes=_attention_scratch(A_HEADS, HEAD_DIM, tq, tk),
        compiler_params=_params(("parallel", "parallel")),
        name="gqa_attention",
    )(qTa, kA, vTa, knorm)


def _attn_b_kernel(qT_ref, k_ref, vT_ref, knorm_ref, lam_ref, gsub_ref, o_ref, w_ref, s_even_ref, s_odd_ref,
                   m_ref, l_ref, acc_ref, *, tk, lam_init):
    nchains = 2 * B_HEADS_PER_STEP
    knorm2 = knorm_ref[pl.ds(_KNORM_B0 + nchains * pl.program_id(1), nchains), 0:1]
    _softmax_value_loop(qT_ref, k_ref, vT_ref, lambda c: knorm2[c:c + 1], lambda c: c, lambda c: c // 2,
                        w_ref, s_even_ref, s_odd_ref, m_ref, l_ref, acc_ref, nchains=nchains, tk=tk)

    lp = lam_ref[...]
    lam = (jnp.exp(jnp.sum(lp[0:1] * lp[1:2], axis=1, keepdims=True))
           - jnp.exp(jnp.sum(lp[2:3] * lp[3:4], axis=1, keepdims=True)) + lam_init)
    for h in range(B_HEADS_PER_STEP):
        o = (acc_ref[2 * h] / _denominator(l_ref, 2 * h)
             - lam * (acc_ref[2 * h + 1] / _denominator(l_ref, 2 * h + 1)))
        ms = jnp.mean(o * o, axis=0, keepdims=True)
        o = ((o * lax.rsqrt(ms + NORM_EPS)) * gsub_ref[...]) * (1.0 - lam_init)
        o_ref[h * B_V_DIM:(h + 1) * B_V_DIM, :] = o.astype(_BF16)


def _attn_b(qTb, kB, vTb, knorm, lam_params, gsub, *, tq, lam_init):
    n, _, s = qTb.shape
    _, _, nck, dv, tk = vTb.shape
    hps = B_HEADS_PER_STEP
    return pl.pallas_call(
        functools.partial(_attn_b_kernel, tk=tk, lam_init=lam_init),
        grid=(n, B_HEADS // hps, s // tq),
        in_specs=[
            pl.BlockSpec((None, hps * 2 * HEAD_DIM, tq), lambda b, hp, i: (b, hp, i)),
            _once_per_row((None, s, hps * 4 * HEAD_DIM), lambda b, hp, i: (b, 0, hp)),
            _once_per_row((None, hps, nck, dv, tk), lambda b, hp, i: (b, hp, 0, 0, 0)),
            pl.BlockSpec((None, _KNORM_ROWS, 128), lambda b, hp, i: (b, 0, 0)),
            pl.BlockSpec((4, HEAD_DIM), lambda b, hp, i: (0, 0)),
            pl.BlockSpec((B_V_DIM, 1), lambda b, hp, i: (0, 0)),
        ],
        out_specs=pl.BlockSpec((None, hps * B_V_DIM, tq), lambda b, hp, i: (b, hp, i)),
        out_shape=jax.ShapeDtypeStruct((n, B_V, s), _BF16),
        scratch_shapes=_attention_scratch(2 * hps, B_V_DIM, tq, tk),
        compiler_params=_params(("parallel", "parallel", "parallel")),
        name="diff_attention",
    )(qTb, kB, vTb, knorm, lam_params, gsub)


def _mix_and_ffn(x_ref, oTa_ref, oTb_ref, mod_ref, wout_ref, gffn_ref, wgu_ref, wdn_ref, act_ref):
    mixed = _dot_tn(oTa_ref[...], wout_ref[0:A_Q, :]) + _dot_tn(oTb_ref[...], wout_ref[A_Q:, :])
    gt1, sh2, sc2, gt2 = mod_ref[2:3, :], mod_ref[3:4, :], mod_ref[4:5, :], mod_ref[5:6, :]
    x1 = x_ref[...] + gt1 * mixed
    h = ((x1 * _rms_rows(x1)) * gffn_ref[...]) * (1.0 + sc2) + sh2
    hb = h.astype(_BF16)
    for c in range(wgu_ref.shape[0]):
        gu = _dot(hb, wgu_ref[c])
        gate, up = gu[:, :FF_CHUNK], gu[:, FF_CHUNK:]
        act_ref[:, c * FF_CHUNK:(c + 1) * FF_CHUNK] = ((gate * jax.nn.sigmoid(gate)) * up).astype(_BF16)
    return x1 + gt2 * _dot(act_ref[...], wdn_ref[...])


def _ffn_kernel(x_ref, oTa_ref, oTb_ref, mod_ref, wout_ref, gffn_ref, wgu_ref, wdn_ref, y_ref, act_ref):
    y_ref[...] = _mix_and_ffn(x_ref, oTa_ref, oTb_ref, mod_ref, wout_ref, gffn_ref, wgu_ref, wdn_ref, act_ref)


def _ffn_final_kernel(x_ref, oTa_ref, oTb_ref, mod_ref, wout_ref, gffn_ref, wgu_ref, wdn_ref, modf_ref, gfin_ref,
                      y_ref, act_ref):
    x = _mix_and_ffn(x_ref, oTa_ref, oTb_ref, mod_ref, wout_ref, gffn_ref, wgu_ref, wdn_ref, act_ref)
    shift, scale = modf_ref[0:1, :], modf_ref[1:2, :]
    y_ref[...] = ((x * _rms_rows(x)) * gfin_ref[...]) * (1.0 + scale) + shift


def _resident(shape):
    return pl.BlockSpec(shape, lambda b, t: (0,) * len(shape), pipeline_mode=pl.Buffered(1))


def _out_ffn(x, oTa, oTb, mod, w_out, gffn, w_gu, w_dn, final=None, *, ts):
    n, s, d = x.shape
    nff, _, _ = w_gu.shape
    dff = w_dn.shape[0]
    in_specs = [
        pl.BlockSpec((None, ts, d), lambda b, t: (b, t, 0)),
        pl.BlockSpec((None, A_Q, ts), lambda b, t: (b, 0, t)),
        pl.BlockSpec((None, B_V, ts), lambda b, t: (b, 0, t)),
        pl.BlockSpec((None, 6, d), lambda b, t: (b, 0, 0)),
        _resident((A_Q + B_V, d)),
        pl.BlockSpec((1, d), lambda b, t: (0, 0)),
        _resident((nff, d, 2 * FF_CHUNK)),
        _resident((dff, d)),
    ]
    operands = (x, oTa, oTb, mod, w_out, gffn, w_gu, w_dn)
    if final is not None:
        in_specs += [pl.BlockSpec((None, 2, d), lambda b, t: (b, 0, 0)), pl.BlockSpec((1, d), lambda b, t: (0, 0))]
        operands += tuple(final)
    return pl.pallas_call(
        _ffn_kernel if final is None else _ffn_final_kernel,
        grid=(n, s // ts),
        in_specs=in_specs,
        out_specs=pl.BlockSpec((None, ts, d), lambda b, t: (b, t, 0)),
        out_shape=jax.ShapeDtypeStruct((n, s, d), _F32),
        scratch_shapes=[pltpu.VMEM((ts, dff), _BF16)],
        compiler_params=_params(("parallel", "parallel")),
        name="out_proj_swiglu",
    )(*operands)


def _rope_table_t(s):
    def tables(pos, dim, theta):
        inv = 1.0 / (theta ** (jnp.arange(0, dim, 2, dtype=_F32) / dim))
        ang = pos.astype(_F32)[:, None] * inv[None, :]
        return jnp.cos(ang).T, jnp.sin(ang).T

    t = jnp.arange(s)
    cos_r, sin_r = tables(t // GRID_W, HEAD_DIM // 2, AXIAL_THETA)
    cos_c, sin_c = tables(t % GRID_W, HEAD_DIM // 2, AXIAL_THETA)
    cos_p, sin_p = tables(t, PARTIAL_ROT, ROPE_THETA)
    return jnp.concatenate([cos_r, sin_r, cos_c, sin_c, cos_p, sin_p], axis=0)


def _trunks(xs, cs, w_ada, b_ada, g_mix, w_in, g_q_a, g_k_a, lam_q1, lam_k1, lam_q2, lam_k2, g_subln,
            w_out, g_ffn, w_gate_up, w_down, w_ada_final, b_ada_final, g_final, *, ts, tq, tk):
    d = xs[0].shape[-1]
    depth = w_ada.shape[0]
    dff = w_down.shape[1]
    nff = dff // FF_CHUNK

    rows = [c.shape[0] for c in cs]
    c_pad = jnp.zeros((8, d), _F32).at[:sum(rows)].set(jnp.concatenate(cs, axis=0))
    mod_all = _modulation(c_pad, w_ada, b_ada[:, None, :])
    modf_all = _modulation(c_pad, w_ada_final[None], b_ada_final[None, None, :])[0]

    w_inT = jnp.swapaxes(w_in, 1, 2).astype(_BF16)
    w_out_b = w_out.astype(_BF16)
    w_gu = jnp.concatenate([w_gate_up[:, :, :dff].reshape(depth, d, nff, FF_CHUNK),
                            w_gate_up[:, :, dff:].reshape(depth, d, nff, FF_CHUNK)], axis=-1)
    w_gu = jnp.transpose(w_gu, (0, 2, 1, 3)).astype(_BF16)
    w_dn = w_down.astype(_BF16)
    lam_params = jnp.stack([lam_q1, lam_k1, lam_q2, lam_k2], axis=1)

    outs = []
    row0 = 0
    for x, n in zip(xs, rows):
        s = x.shape[1]
        mod = mod_all[:, row0:row0 + n].reshape(depth, n, 6, d)
        modf = modf_all[row0:row0 + n].reshape(n, 2, d)
        row0 += n
        tab = _rope_table_t(s)
        for l in range(depth):
            qTa, kA, vTa, qTb, kB, vTb, knorm = _inproj(
                x, mod[l], g_mix[l][None, :], w_inT[l], g_q_a[l][:, None], g_k_a[l][:, None], tab, ts=ts, tk=tk)
            oTa = _attn_a(qTa, kA, vTa, knorm, tq=tq)
            lam_init = 0.8 - 0.6 * math.exp(-0.3 * l)
            oTb = _attn_b(qTb, kB, vTb, knorm, lam_params[l], g_subln[l][:, None], tq=tq, lam_init=lam_init)
            final = (modf, g_final[None, :]) if l == depth - 1 else None
            x = _out_ffn(x, oTa, oTb, mod[l], w_out_b[l], g_ffn[l][None, :], w_gu[l], w_dn[l], final, ts=ts)
        outs.append(x)
    return tuple(outs)


def _trunk(x, c, *weights, ts, tq, tk):
    return _trunks((x,), (c,), *weights, ts=ts, tq=tq, tk=tk)[0]


def kernel(x_prompt, x_sample, c_prompt, c_sample, w_ada, b_ada, g_mix, w_in, g_q_a, g_k_a, lam_q1, lam_k1, lam_q2, lam_k2, g_subln, w_out, g_ffn, w_gate_up, w_down, w_ada_final, b_ada_final, g_final):
    return _trunks((x_prompt, x_sample), (c_prompt, c_sample), w_ada, b_ada, g_mix, w_in, g_q_a, g_k_a,
                   lam_q1, lam_k1, lam_q2, lam_k2, g_subln, w_out, g_ffn, w_gate_up, w_down,
                   w_ada_final, b_ada_final, g_final, ts=512, tq=256, tk=256)
```

```python
import functools
import math

import jax
import jax.numpy as jnp
from jax import lax
from jax.experimental import pallas as pl
from jax.experimental.pallas import tpu as pltpu

HEAD_DIM = 64
A_HEADS = 8
A_KV_HEADS = 2
A_GROUP = A_HEADS // A_KV_HEADS
B_HEADS = 4
B_V_DIM = 2 * HEAD_DIM
A_Q = A_HEADS * HEAD_DIM
A_KV = A_KV_HEADS * HEAD_DIM
B_QK = B_HEADS * 2 * HEAD_DIM
B_V = B_HEADS * B_V_DIM
IN_WIDTH = A_Q + 2 * A_KV + 2 * B_QK + B_V
GRID_W = 64
AXIAL_THETA = 10000.0
ROPE_THETA = 500000.0
PARTIAL_ROT = HEAD_DIM // 4
NORM_EPS = 1e-6
QK_SCALE = HEAD_DIM ** -0.5 * math.log2(math.e)

_QA0, _KA0, _VA0 = 0, A_Q, A_Q + A_KV
_QB0 = A_Q + 2 * A_KV
_KB0 = _QB0 + B_QK
_VB0 = _KB0 + B_QK

_KNORM_B0, _KNORM_A0, _KNORM_ROWS = 0, 2 * B_HEADS, 16
FAST_SOFTMAX_MAX_BOUND = 48.0
B_HEADS_PER_STEP = 2
QUERY_BLOCKS_PER_STEP = 2
FAST_STEPS_PER_TRIP = 256
FAST_LOOKAHEAD = 4

FF_CHUNK = 256
V7X_VMEM_LIMIT_BYTES = 56 * 1024 * 1024

_BF16 = jnp.bfloat16
_F32 = jnp.float32


def _params(semantics):
    return pltpu.CompilerParams(dimension_semantics=semantics, vmem_limit_bytes=V7X_VMEM_LIMIT_BYTES)


def _once_per_row(block_shape, index_map):
    return pl.BlockSpec(block_shape, index_map, pipeline_mode=pl.Buffered(1))


def _dot(a, b):
    return jnp.dot(a, b, preferred_element_type=_F32)


def _dot_nt(a, b):
    return lax.dot_general(a, b, (((1,), (1,)), ((), ())), preferred_element_type=_F32)


def _dot_tn(a, b):
    return lax.dot_general(a, b, (((0,), (0,)), ((), ())), preferred_element_type=_F32)


def _split_bf16(x):
    hi = x.astype(_BF16)
    lo = (x - hi.astype(_F32)).astype(_BF16)
    return hi, lo


def _mod_kernel(c_ref, w_ref, b_ref, o_ref):
    c = c_ref[...]
    c_act = c * jax.nn.sigmoid(c)
    c_hi, c_lo = _split_bf16(c_act)
    w_hi, w_lo = _split_bf16(w_ref[...])
    acc = _dot(c_hi, w_hi) + (_dot(c_lo, w_hi) + _dot(c_hi, w_lo))
    o_ref[...] = acc + b_ref[...]


def _modulation(c_pad, w, b, tn=1024):
    nl, d, n = w.shape
    return pl.pallas_call(
        _mod_kernel,
        grid=(nl, n // tn),
        in_specs=[
            pl.BlockSpec((8, d), lambda l, j: (0, 0)),
            pl.BlockSpec((None, d, tn), lambda l, j: (l, 0, j)),
            pl.BlockSpec((None, 1, tn), lambda l, j: (l, 0, j)),
        ],
        out_specs=pl.BlockSpec((None, 8, tn), lambda l, j: (l, 0, j)),
        out_shape=jax.ShapeDtypeStruct((nl, 8, n), _F32),
        compiler_params=_params(("parallel", "parallel")),
        name="adaln_modulation",
    )(c_pad, w, b)


def _rms_rows(x):
    return lax.rsqrt(jnp.mean(x * x, axis=1, keepdims=True) + NORM_EPS)


def _inproj_kernel(x_ref, mod_ref, gmix_ref, wT_ref, gq_ref, gk_ref, tab_ref,
                   qTa_ref, kA_ref, vTa_ref, qTb_ref, kB_ref, vTb_ref, knorm_ref, *, tk):
    ts = x_ref.shape[0]
    x = x_ref[...]
    sh1 = mod_ref[0:1, :]
    sc1 = mod_ref[1:2, :]
    h = ((x * _rms_rows(x)) * gmix_ref[...]) * (1.0 + sc1) + sh1
    hb = h.astype(_BF16)

    def proj_t(lo, n):
        return _dot_nt(wT_ref[lo:lo + n, :], hb)

    tab = tab_ref[...]
    cos_r, sin_r, cos_c, sin_c = tab[0:16], tab[16:32], tab[32:48], tab[48:64]
    cos_p, sin_p = tab[64:72], tab[72:80]

    def head_norm(x3, g_ref):
        ms = jnp.mean(x3 * x3, axis=1, keepdims=True)
        return (x3 * lax.rsqrt(ms + NORM_EPS)) * g_ref[...][None]

    def axial(x3):
        r1, r2, c1, c2 = x3[:, 0:16], x3[:, 16:32], x3[:, 32:48], x3[:, 48:64]
        return jnp.concatenate([r1 * cos_r - r2 * sin_r, r1 * sin_r + r2 * cos_r,
                                c1 * cos_c - c2 * sin_c, c1 * sin_c + c2 * cos_c], axis=1)

    def partial(x3):
        half = PARTIAL_ROT // 2
        x1, x2, rest = x3[:, 0:half], x3[:, half:PARTIAL_ROT], x3[:, PARTIAL_ROT:]
        return jnp.concatenate([x1 * cos_p - x2 * sin_p, x1 * sin_p + x2 * cos_p, rest], axis=1)

    qa = proj_t(_QA0, A_Q).reshape(A_HEADS, HEAD_DIM, ts)
    qa = axial(head_norm(qa, gq_ref)) * QK_SCALE
    qTa_ref[...] = qa.reshape(A_Q, ts).astype(_BF16)

    def max_key_norm2(k3):
        kf = k3.astype(_BF16).astype(_F32)
        return jnp.max(jnp.sum(kf * kf, axis=1), axis=1, keepdims=True)

    one_hot_rows = (lax.broadcasted_iota(jnp.int32, (HEAD_DIM, ts), 0) == 0).astype(_F32)

    def widened_keys_t(k3):
        parts = []
        for hh in range(k3.shape[0]):
            parts += [k3[hh], one_hot_rows]
        return jnp.concatenate(parts, axis=0).T.astype(_BF16)

    ka = axial(head_norm(proj_t(_KA0, A_KV).reshape(A_KV_HEADS, HEAD_DIM, ts), gk_ref))
    kA_ref[...] = widened_keys_t(ka)

    va = proj_t(_VA0, A_KV).astype(_BF16)
    for g in range(A_KV_HEADS):
        for c in range(ts // tk):
            vTa_ref[g, c] = va[g * HEAD_DIM:(g + 1) * HEAD_DIM, c * tk:(c + 1) * tk]

    qb = proj_t(_QB0, B_QK).reshape(2 * B_HEADS, HEAD_DIM, ts)
    qTb_ref[...] = (partial(qb) * QK_SCALE).reshape(B_QK, ts).astype(_BF16)

    kb = partial(proj_t(_KB0, B_QK).reshape(2 * B_HEADS, HEAD_DIM, ts))
    kB_ref[...] = widened_keys_t(kb)

    vb = proj_t(_VB0, B_V).astype(_BF16)
    for hd in range(B_HEADS):
        for c in range(ts // tk):
            vTb_ref[hd, c] = vb[hd * B_V_DIM:(hd + 1) * B_V_DIM, c * tk:(c + 1) * tk]

    @pl.when(pl.program_id(1) == 0)
    def _():
        knorm_ref[...] = jnp.zeros(knorm_ref.shape, _F32)

    lanes = knorm_ref.shape[1]
    b_rows = slice(_KNORM_B0, _KNORM_B0 + 2 * B_HEADS)
    a_rows = slice(_KNORM_A0, _KNORM_A0 + A_KV_HEADS)
    knorm_ref[b_rows, :] = jnp.maximum(knorm_ref[b_rows, :], jnp.broadcast_to(max_key_norm2(kb), (2 * B_HEADS, lanes)))
    knorm_ref[a_rows, :] = jnp.maximum(knorm_ref[a_rows, :], jnp.broadcast_to(max_key_norm2(ka), (A_KV_HEADS, lanes)))


def _inproj(x, mod, gmix, w_inT, gq, gk, tab, *, ts, tk):
    n, s, d = x.shape
    nck = s // tk
    cpt = ts // tk
    out_shape = (
        jax.ShapeDtypeStruct((n, A_Q, s), _BF16),
        jax.ShapeDtypeStruct((n, s, 2 * A_KV), _BF16),
        jax.ShapeDtypeStruct((n, A_KV_HEADS, nck, HEAD_DIM, tk), _BF16),
        jax.ShapeDtypeStruct((n, B_QK, s), _BF16),
        jax.ShapeDtypeStruct((n, s, 2 * B_QK), _BF16),
        jax.ShapeDtypeStruct((n, B_HEADS, nck, B_V_DIM, tk), _BF16),
        jax.ShapeDtypeStruct((n, _KNORM_ROWS, 128), _F32),
    )
    return pl.pallas_call(
        functools.partial(_inproj_kernel, tk=tk),
        grid=(n, s // ts),
        in_specs=[
            pl.BlockSpec((None, ts, d), lambda b, t: (b, t, 0)),
            pl.BlockSpec((None, 6, d), lambda b, t: (b, 0, 0)),
            pl.BlockSpec((1, d), lambda b, t: (0, 0)),
            pl.BlockSpec((IN_WIDTH, d), lambda b, t: (0, 0)),
            pl.BlockSpec((HEAD_DIM, 1), lambda b, t: (0, 0)),
            pl.BlockSpec((HEAD_DIM, 1), lambda b, t: (0, 0)),
            pl.BlockSpec((80, ts), lambda b, t: (0, t)),
        ],
        out_specs=(
            pl.BlockSpec((None, A_Q, ts), lambda b, t: (b, 0, t)),
            pl.BlockSpec((None, ts, 2 * A_KV), lambda b, t: (b, t, 0)),
            pl.BlockSpec((None, A_KV_HEADS, cpt, HEAD_DIM, tk), lambda b, t: (b, 0, t, 0, 0)),
            pl.BlockSpec((None, B_QK, ts), lambda b, t: (b, 0, t)),
            pl.BlockSpec((None, ts, 2 * B_QK), lambda b, t: (b, t, 0)),
            pl.BlockSpec((None, B_HEADS, cpt, B_V_DIM, tk), lambda b, t: (b, 0, t, 0, 0)),
            pl.BlockSpec((None, _KNORM_ROWS, 128), lambda b, t: (b, 0, 0)),
        ),
        out_shape=out_shape,
        compiler_params=_params(("parallel", "arbitrary")),
        name="mix_in_projection",
    )(x, mod, gmix, w_inT, gq, gk, tab)


def _softmax_value_loop(qT_ref, k_ref, vT_ref, knorm2_of_row, key_head_of_row, value_head_of_row, w_ref,
                        s_even_ref, s_odd_ref, m_ref, l_ref, acc_ref, *, qrows, tq, tk):
    nck = vT_ref.shape[1]
    nchains = qrows * (qT_ref.shape[-1] // tq)
    knorm2_of_chain = lambda c: knorm2_of_row(c % qrows)
    key_head_of_chain = lambda c: key_head_of_row(c % qrows)
    value_head_of_chain = lambda c: value_head_of_row(c % qrows)
    acc_ref[...] = jnp.zeros(acc_ref.shape, _F32)
    l_ref[...] = jnp.zeros(l_ref.shape, _F32)

    qs = [qT_ref[(c % qrows) * HEAD_DIM:(c % qrows + 1) * HEAD_DIM, (c // qrows) * tq:(c // qrows + 1) * tq]
          for c in range(nchains)]
    bounds = []
    for c in range(nchains):
        qf = qs[c].astype(_F32)
        bounds.append(jnp.sqrt(jnp.sum(qf * qf, axis=0, keepdims=True) * knorm2_of_chain(c)))
    use_bound = jnp.max(jnp.concatenate(bounds, axis=0)) <= FAST_SOFTMAX_MAX_BOUND

    first_row = lax.broadcasted_iota(jnp.int32, (HEAD_DIM, tq), 0) == 0
    for c in range(nchains):
        offset = -jnp.minimum(bounds[c], FAST_SOFTMAX_MAX_BOUND)
        offset_rows = jnp.where(first_row, jnp.broadcast_to(offset, (HEAD_DIM, tq)), 0.0)
        w_ref[c] = jnp.concatenate([qs[c], offset_rows.astype(_BF16)], axis=0)

    def k_chunk(j, c):
        lane0 = key_head_of_chain(c) * 2 * HEAD_DIM
        return k_ref[pl.ds(pl.multiple_of(j * tk, tk), tk), lane0:lane0 + 2 * HEAD_DIM]

    def partial_sums(p):
        return jnp.sum(p.reshape(tk // 8, 8, tq), axis=0)

    assert nchains >= FAST_LOOKAHEAD
    for a in range(FAST_LOOKAHEAD):
        s_even_ref[a] = _dot(k_chunk(0, a), w_ref[a])

    @pl.when(use_bound)
    def _():
        def scores(j, step):
            dj, c = divmod(step, nchains)
            jj = j + dj
            if dj and not isinstance(jj, int):
                jj = jnp.where(jj >= nck, jj - nck, jj)
            return _dot(k_chunk(jj, c), w_ref[c])

        chunks_per_trip = min(FAST_STEPS_PER_TRIP // nchains, nck)
        assert nck % chunks_per_trip == 0

        single_trip = chunks_per_trip == nck

        def body(i, carry):
            j0 = i * chunks_per_trip
            pending = [s_even_ref[a] for a in range(FAST_LOOKAHEAD)]
            for u in range(chunks_per_trip):
                for c in range(nchains):
                    ahead = u * nchains + c + FAST_LOOKAHEAD
                    if not (single_trip and ahead >= nck * nchains):
                        pending.append(scores(j0, ahead))
                    p = jnp.exp2(pending.pop(0))
                    l_ref[c] += partial_sums(p)
                    acc_ref[c] += _dot(vT_ref[value_head_of_chain(c), j0 + u], p.astype(_BF16))
            for a, tile in enumerate(pending):
                s_even_ref[a] = tile
            return carry

        if single_trip:
            body(0, 0)
        else:
            lax.fori_loop(0, nck // chunks_per_trip, body, 0)

    @pl.when(jnp.logical_not(use_bound))
    def _():
        m_ref[...] = jnp.full(m_ref.shape, -jnp.inf, _F32)

        def phase(j_cur, j_next, s_cur_ref, s_next_ref):
            for c in range(nchains):
                vc = vT_ref[value_head_of_chain(c), j_cur]
                s = s_cur_ref[c]
                s_next_ref[c] = _dot(k_chunk(j_next, c), w_ref[c])
                m_prev = m_ref[c]
                m_new = jnp.maximum(m_prev, jnp.max(s, axis=0, keepdims=True))
                alpha = jnp.exp2(m_prev - m_new)
                p = jnp.exp2(s - m_new)
                l_ref[c] = alpha * l_ref[c] + partial_sums(p)
                acc_ref[c] = alpha * acc_ref[c] + _dot(vc, p.astype(_BF16))
                m_ref[c] = m_new

        for c in range(FAST_LOOKAHEAD, nchains):
            s_even_ref[c] = _dot(k_chunk(0, c), w_ref[c])

        def body(i, carry):
            j = 2 * i
            phase(j, j + 1, s_even_ref, s_odd_ref)
            phase(j + 1, jnp.where(j + 2 == nck, 0, j + 2), s_odd_ref, s_even_ref)
            return carry

        lax.fori_loop(0, nck // 2, body, 0)


def _denominator(l_ref, c):
    return jnp.sum(l_ref[c], axis=0, keepdims=True)


def _attn_a_kernel(qT_ref, k_ref, vT_ref, knorm_ref, o_ref, w_ref, s_even_ref, s_odd_ref, m_ref, l_ref, acc_ref,
                   *, tq, tk):
    knorm2 = knorm_ref[_KNORM_A0:_KNORM_A0 + A_KV_HEADS, 0:1]
    kv_head = lambda r: r // A_GROUP
    _softmax_value_loop(qT_ref, k_ref, vT_ref, lambda r: knorm2[kv_head(r):kv_head(r) + 1], kv_head, kv_head,
                        w_ref, s_even_ref, s_odd_ref, m_ref, l_ref, acc_ref, qrows=A_HEADS, tq=tq, tk=tk)
    for c in range(acc_ref.shape[0]):
        blk, r = divmod(c, A_HEADS)
        o = acc_ref[c] / _denominator(l_ref, c)
        o_ref[r * HEAD_DIM:(r + 1) * HEAD_DIM, blk * tq:(blk + 1) * tq] = o.astype(_BF16)


def _attention_scratch(nchains, dv, tq, tk):
    return [
        pltpu.VMEM((nchains, 2 * HEAD_DIM, tq), _BF16),
        pltpu.VMEM((max(nchains, FAST_LOOKAHEAD), tk, tq), _F32),
        pltpu.VMEM((nchains, tk, tq), _F32),
        pltpu.VMEM((nchains, 1, tq), _F32),
        pltpu.VMEM((nchains, 8, tq), _F32),
        pltpu.VMEM((nchains, dv, tq), _F32),
    ]


def _attn_a(qTa, kA, vTa, knorm, *, tq):
    n, _, s = qTa.shape
    _, nkv, nck, dv, tk = vTa.shape
    nblk = QUERY_BLOCKS_PER_STEP
    return pl.pallas_call(
        functools.partial(_attn_a_kernel, tq=tq, tk=tk),
        grid=(n, s // (nblk * tq)),
        in_specs=[
            pl.BlockSpec((None, A_Q, nblk * tq), lambda b, i: (b, 0, i)),
            _once_per_row((None, s, nkv * 2 * HEAD_DIM), lambda b, i: (b, 0, 0)),
            _once_per_row((None, nkv, nck, dv, tk), lambda b, i: (b, 0, 0, 0, 0)),
            pl.BlockSpec((None, _KNORM_ROWS, 128), lambda b, i: (b, 0, 0)),
        ],
        out_specs=pl.BlockSpec((None, A_Q, nblk * tq), lambda b, i: (b, 0, i)),
        out_shape=jax.ShapeDtypeStruct((n, A_Q, s), _BF16),
        scratch_shapes=_attention_scratch(nblk * A_HEADS, HEAD_DIM, tq, tk),
        compiler_params=_params(("parallel", "parallel")),
        name="gqa_attention",
    )(qTa, kA, vTa, knorm)


def _attn_b_kernel(qT_ref, k_ref, vT_ref, knorm_ref, lam_ref, gsub_ref, o_ref, w_ref, s_even_ref, s_odd_ref,
                   m_ref, l_ref, acc_ref, *, tq, tk, lam_init):
    qrows = 2 * B_HEADS_PER_STEP
    knorm2 = knorm_ref[pl.ds(_KNORM_B0 + qrows * pl.program_id(1), qrows), 0:1]
    _softmax_value_loop(qT_ref, k_ref, vT_ref, lambda r: knorm2[r:r + 1], lambda r: r, lambda r: r // 2,
                        w_ref, s_even_ref, s_odd_ref, m_ref, l_ref, acc_ref, qrows=qrows, tq=tq, tk=tk)

    lp = lam_ref[...]
    lam = (jnp.exp(jnp.sum(lp[0:1] * lp[1:2], axis=1, keepdims=True))
           - jnp.exp(jnp.sum(lp[2:3] * lp[3:4], axis=1, keepdims=True)) + lam_init)
    for blk in range(acc_ref.shape[0] // qrows):
        for h in range(B_HEADS_PER_STEP):
            c1 = blk * qrows + 2 * h
            o = acc_ref[c1] / _denominator(l_ref, c1) - lam * (acc_ref[c1 + 1] / _denominator(l_ref, c1 + 1))
            ms = jnp.mean(o * o, axis=0, keepdims=True)
            o = ((o * lax.rsqrt(ms + NORM_EPS)) * gsub_ref[...]) * (1.0 - lam_init)
            o_ref[h * B_V_DIM:(h + 1) * B_V_DIM, blk * tq:(blk + 1) * tq] = o.astype(_BF16)


def _attn_b(qTb, kB, vTb, knorm, lam_params, gsub, *, tq, lam_init):
    n, _, s = qTb.shape
    _, _, nck, dv, tk = vTb.shape
    hps = B_HEADS_PER_STEP
    nblk = QUERY_BLOCKS_PER_STEP
    return pl.pallas_call(
        functools.partial(_attn_b_kernel, tq=tq, tk=tk, lam_init=lam_init),
        grid=(n, B_HEADS // hps, s // (nblk * tq)),
        in_specs=[
            pl.BlockSpec((None, hps * 2 * HEAD_DIM, nblk * tq), lambda b, hp, i: (b, hp, i)),
            _once_per_row((None, s, hps * 4 * HEAD_DIM), lambda b, hp, i: (b, 0, hp)),
            _once_per_row((None, hps, nck, dv, tk), lambda b, hp, i: (b, hp, 0, 0, 0)),
            pl.BlockSpec((None, _KNORM_ROWS, 128), lambda b, hp, i: (b, 0, 0)),
            pl.BlockSpec((4, HEAD_DIM), lambda b, hp, i: (0, 0)),
            pl.BlockSpec((B_V_DIM, 1), lambda b, hp, i: (0, 0)),
        ],
        out_specs=pl.BlockSpec((None, hps * B_V_DIM, nblk * tq), lambda b, hp, i: (b, hp, i)),
        out_shape=jax.ShapeDtypeStruct((n, B_V, s), _BF16),
        scratch_shapes=_attention_scratch(nblk * 2 * hps, B_V_DIM, tq, tk),
        compiler_params=_params(("parallel", "parallel", "parallel")),
        name="diff_attention",
    )(qTb, kB, vTb, knorm, lam_params, gsub)


def _mix_and_ffn(x_ref, oTa_ref, oTb_ref, mod_ref, wout_ref, gffn_ref, wgu_ref, wdn_ref, act_ref):
    mixed = _dot_tn(oTa_ref[...], wout_ref[0:A_Q, :]) + _dot_tn(oTb_ref[...], wout_ref[A_Q:, :])
    gt1, sh2, sc2, gt2 = mod_ref[2:3, :], mod_ref[3:4, :], mod_ref[4:5, :], mod_ref[5:6, :]
    x1 = x_ref[...] + gt1 * mixed
    h = ((x1 * _rms_rows(x1)) * gffn_ref[...]) * (1.0 + sc2) + sh2
    hb = h.astype(_BF16)
    for c in range(wgu_ref.shape[0]):
        gu = _dot(hb, wgu_ref[c])
        gate, up = gu[:, :FF_CHUNK], gu[:, FF_CHUNK:]
        act_ref[:, c * FF_CHUNK:(c + 1) * FF_CHUNK] = ((gate * jax.nn.sigmoid(gate)) * up).astype(_BF16)
    return x1 + gt2 * _dot(act_ref[...], wdn_ref[...])


def _ffn_kernel(x_ref, oTa_ref, oTb_ref, mod_ref, wout_ref, gffn_ref, wgu_ref, wdn_ref, y_ref, act_ref):
    y_ref[...] = _mix_and_ffn(x_ref, oTa_ref, oTb_ref, mod_ref, wout_ref, gffn_ref, wgu_ref, wdn_ref, act_ref)


def _ffn_final_kernel(x_ref, oTa_ref, oTb_ref, mod_ref, wout_ref, gffn_ref, wgu_ref, wdn_ref, modf_ref, gfin_ref,
                      y_ref, act_ref):
    x = _mix_and_ffn(x_ref, oTa_ref, oTb_ref, mod_ref, wout_ref, gffn_ref, wgu_ref, wdn_ref, act_ref)
    shift, scale = modf_ref[0:1, :], modf_ref[1:2, :]
    y_ref[...] = ((x * _rms_rows(x)) * gfin_ref[...]) * (1.0 + scale) + shift


def _resident(shape):
    return pl.BlockSpec(shape, lambda b, t: (0,) * len(shape), pipeline_mode=pl.Buffered(1))


def _out_ffn(x, oTa, oTb, mod, w_out, gffn, w_gu, w_dn, final=None, *, ts):
    n, s, d = x.shape
    nff, _, _ = w_gu.shape
    dff = w_dn.shape[0]
    in_specs = [
        pl.BlockSpec((None, ts, d), lambda b, t: (b, t, 0)),
        pl.BlockSpec((None, A_Q, ts), lambda b, t: (b, 0, t)),
        pl.BlockSpec((None, B_V, ts), lambda b, t: (b, 0, t)),
        pl.BlockSpec((None, 6, d), lambda b, t: (b, 0, 0)),
        _resident((A_Q + B_V, d)),
        pl.BlockSpec((1, d), lambda b, t: (0, 0)),
        _resident((nff, d, 2 * FF_CHUNK)),
        _resident((dff, d)),
    ]
    operands = (x, oTa, oTb, mod, w_out, gffn, w_gu, w_dn)
    if final is not None:
        in_specs += [pl.BlockSpec((None, 2, d), lambda b, t: (b, 0, 0)), pl.BlockSpec((1, d), lambda b, t: (0, 0))]
        operands += tuple(final)
    return pl.pallas_call(
        _ffn_kernel if final is None else _ffn_final_kernel,
        grid=(n, s // ts),
        in_specs=in_specs,
        out_specs=pl.BlockSpec((None, ts, d), lambda b, t: (b, t, 0)),
        out_shape=jax.ShapeDtypeStruct((n, s, d), _F32),
        scratch_shapes=[pltpu.VMEM((ts, dff), _BF16)],
        compiler_params=_params(("parallel", "parallel")),
        name="out_proj_swiglu",
    )(*operands)


def _rope_table_t(s):
    def tables(pos, dim, theta):
        inv = 1.0 / (theta ** (jnp.arange(0, dim, 2, dtype=_F32) / dim))
        ang = pos.astype(_F32)[:, None] * inv[None, :]
        return jnp.cos(ang).T, jnp.sin(ang).T

    t = jnp.arange(s)
    cos_r, sin_r = tables(t // GRID_W, HEAD_DIM // 2, AXIAL_THETA)
    cos_c, sin_c = tables(t % GRID_W, HEAD_DIM // 2, AXIAL_THETA)
    cos_p, sin_p = tables(t, PARTIAL_ROT, ROPE_THETA)
    return jnp.concatenate([cos_r, sin_r, cos_c, sin_c, cos_p, sin_p], axis=0)


def _trunks(xs, cs, w_ada, b_ada, g_mix, w_in, g_q_a, g_k_a, lam_q1, lam_k1, lam_q2, lam_k2, g_subln,
            w_out, g_ffn, w_gate_up, w_down, w_ada_final, b_ada_final, g_final, *, ts, tq, tk):
    d = xs[0].shape[-1]
    depth = w_ada.shape[0]
    dff = w_down.shape[1]
    nff = dff // FF_CHUNK

    rows = [c.shape[0] for c in cs]
    c_pad = jnp.zeros((8, d), _F32).at[:sum(rows)].set(jnp.concatenate(cs, axis=0))
    mod_all = _modulation(c_pad, w_ada, b_ada[:, None, :])
    modf_all = _modulation(c_pad, w_ada_final[None], b_ada_final[None, None, :])[0]

    w_inT = jnp.swapaxes(w_in, 1, 2).astype(_BF16)
    w_out_b = w_out.astype(_BF16)
    w_gu = jnp.concatenate([w_gate_up[:, :, :dff].reshape(depth, d, nff, FF_CHUNK),
                            w_gate_up[:, :, dff:].reshape(depth, d, nff, FF_CHUNK)], axis=-1)
    w_gu = jnp.transpose(w_gu, (0, 2, 1, 3)).astype(_BF16)
    w_dn = w_down.astype(_BF16)
    lam_params = jnp.stack([lam_q1, lam_k1, lam_q2, lam_k2], axis=1)

    outs = []
    row0 = 0
    for x, n in zip(xs, rows):
        s = x.shape[1]
        mod = mod_all[:, row0:row0 + n].reshape(depth, n, 6, d)
        modf = modf_all[row0:row0 + n].reshape(n, 2, d)
        row0 += n
        tab = _rope_table_t(s)
        for l in range(depth):
            qTa, kA, vTa, qTb, kB, vTb, knorm = _inproj(
                x, mod[l], g_mix[l][None, :], w_inT[l], g_q_a[l][:, None], g_k_a[l][:, None], tab, ts=ts, tk=tk)
            oTa = _attn_a(qTa, kA, vTa, knorm, tq=tq)
            lam_init = 0.8 - 0.6 * math.exp(-0.3 * l)
            oTb = _attn_b(qTb, kB, vTb, knorm, lam_params[l], g_subln[l][:, None], tq=tq, lam_init=lam_init)
            final = (modf, g_final[None, :]) if l == depth - 1 else None
            x = _out_ffn(x, oTa, oTb, mod[l], w_out_b[l], g_ffn[l][None, :], w_gu[l], w_dn[l], final, ts=ts)
        outs.append(x)
    return tuple(outs)


def _trunk(x, c, *weights, ts, tq, tk):
    return _trunks((x,), (c,), *weights, ts=ts, tq=tq, tk=tk)[0]


def kernel(x_prompt, x_sample, c_prompt, c_sample, w_ada, b_ada, g_mix, w_in, g_q_a, g_k_a, lam_q1, lam_k1, lam_q2, lam_k2, g_subln, w_out, g_ffn, w_gate_up, w_down, w_ada_final, b_ada_final, g_final):
    return _trunks((x_prompt, x_sample), (c_prompt, c_sample), w_ada, b_ada, g_mix, w_in, g_q_a, g_k_a,
                   lam_q1, lam_k1, lam_q2, lam_k2, g_subln, w_out, g_ffn, w_gate_up, w_down,
                   w_ada_final, b_ada_final, g_final, ts=512, tq=256, tk=256)
```

```python
import functools
import math

import jax
import jax.numpy as jnp
from jax import lax
from jax.experimental import pallas as pl
from jax.experimental.pallas import tpu as pltpu

HEAD_DIM = 64
A_HEADS = 8
A_KV_HEADS = 2
A_GROUP = A_HEADS // A_KV_HEADS
B_HEADS = 4
B_V_DIM = 2 * HEAD_DIM
A_Q = A_HEADS * HEAD_DIM
A_KV = A_KV_HEADS * HEAD_DIM
B_QK = B_HEADS * 2 * HEAD_DIM
B_V = B_HEADS * B_V_DIM
IN_WIDTH = A_Q + 2 * A_KV + 2 * B_QK + B_V
GRID_W = 64
AXIAL_THETA = 10000.0
ROPE_THETA = 500000.0
PARTIAL_ROT = HEAD_DIM // 4
NORM_EPS = 1e-6
QK_SCALE = HEAD_DIM ** -0.5 * math.log2(math.e)

_QA0, _KA0, _VA0 = 0, A_Q, A_Q + A_KV
_QB0 = A_Q + 2 * A_KV
_KB0 = _QB0 + B_QK
_VB0 = _KB0 + B_QK

_KNORM_B0, _KNORM_A0, _KNORM_ROWS = 0, 2 * B_HEADS, 16
FAST_SOFTMAX_MAX_BOUND = 48.0
OFFSET_ROWS = 16
B_HEADS_PER_STEP = 2
QUERY_BLOCKS_PER_STEP = 2
FAST_STEPS_PER_TRIP = 256
FAST_LOOKAHEAD = 4

FF_CHUNK = 256
V7X_VMEM_LIMIT_BYTES = 56 * 1024 * 1024

_BF16 = jnp.bfloat16
_F32 = jnp.float32


def _params(semantics):
    return pltpu.CompilerParams(dimension_semantics=semantics, vmem_limit_bytes=V7X_VMEM_LIMIT_BYTES)


def _once_per_row(block_shape, index_map):
    return pl.BlockSpec(block_shape, index_map, pipeline_mode=pl.Buffered(1))


def _dot(a, b):
    return jnp.dot(a, b, preferred_element_type=_F32)


def _dot_nt(a, b):
    return lax.dot_general(a, b, (((1,), (1,)), ((), ())), preferred_element_type=_F32)


def _dot_tn(a, b):
    return lax.dot_general(a, b, (((0,), (0,)), ((), ())), preferred_element_type=_F32)


def _split_bf16(x):
    hi = x.astype(_BF16)
    lo = (x - hi.astype(_F32)).astype(_BF16)
    return hi, lo


def _mod_kernel(c_ref, w_ref, b_ref, o_ref):
    c = c_ref[...]
    c_act = c * jax.nn.sigmoid(c)
    c_hi, c_lo = _split_bf16(c_act)
    w_hi, w_lo = _split_bf16(w_ref[...])
    acc = _dot(c_hi, w_hi) + (_dot(c_lo, w_hi) + _dot(c_hi, w_lo))
    o_ref[...] = acc + b_ref[...]


def _modulation(c_pad, w, b, tn=1024):
    nl, d, n = w.shape
    return pl.pallas_call(
        _mod_kernel,
        grid=(nl, n // tn),
        in_specs=[
            pl.BlockSpec((8, d), lambda l, j: (0, 0)),
            pl.BlockSpec((None, d, tn), lambda l, j: (l, 0, j)),
            pl.BlockSpec((None, 1, tn), lambda l, j: (l, 0, j)),
        ],
        out_specs=pl.BlockSpec((None, 8, tn), lambda l, j: (l, 0, j)),
        out_shape=jax.ShapeDtypeStruct((nl, 8, n), _F32),
        compiler_params=_params(("parallel", "parallel")),
        name="adaln_modulation",
    )(c_pad, w, b)


def _rms_rows(x):
    return lax.rsqrt(jnp.mean(x * x, axis=1, keepdims=True) + NORM_EPS)


def _inproj_kernel(x_ref, mod_ref, gmix_ref, wT_ref, gq_ref, gk_ref, tab_ref,
                   qTa_ref, kA_ref, vTa_ref, qTb_ref, kB_ref, vTb_ref, knorm_ref, *, tk):
    ts = x_ref.shape[0]
    x = x_ref[...]
    sh1 = mod_ref[0:1, :]
    sc1 = mod_ref[1:2, :]
    h = ((x * _rms_rows(x)) * gmix_ref[...]) * (1.0 + sc1) + sh1
    hb = h.astype(_BF16)

    def proj_t(lo, n):
        return _dot_nt(wT_ref[lo:lo + n, :], hb)

    tab = tab_ref[...]
    cos_r, sin_r, cos_c, sin_c = tab[0:16], tab[16:32], tab[32:48], tab[48:64]
    cos_p, sin_p = tab[64:72], tab[72:80]

    def head_norm(x3, g_ref):
        ms = jnp.mean(x3 * x3, axis=1, keepdims=True)
        return (x3 * lax.rsqrt(ms + NORM_EPS)) * g_ref[...][None]

    def axial(x3):
        r1, r2, c1, c2 = x3[:, 0:16], x3[:, 16:32], x3[:, 32:48], x3[:, 48:64]
        return jnp.concatenate([r1 * cos_r - r2 * sin_r, r1 * sin_r + r2 * cos_r,
                                c1 * cos_c - c2 * sin_c, c1 * sin_c + c2 * cos_c], axis=1)

    def partial(x3):
        half = PARTIAL_ROT // 2
        x1, x2, rest = x3[:, 0:half], x3[:, half:PARTIAL_ROT], x3[:, PARTIAL_ROT:]
        return jnp.concatenate([x1 * cos_p - x2 * sin_p, x1 * sin_p + x2 * cos_p, rest], axis=1)

    qa = proj_t(_QA0, A_Q).reshape(A_HEADS, HEAD_DIM, ts)
    qa = axial(head_norm(qa, gq_ref)) * QK_SCALE
    qTa_ref[...] = qa.reshape(A_Q, ts).astype(_BF16)

    def max_key_norm2(k3):
        kf = k3.astype(_BF16).astype(_F32)
        return jnp.max(jnp.sum(kf * kf, axis=1), axis=1, keepdims=True)

    one_hot_rows = (lax.broadcasted_iota(jnp.int32, (HEAD_DIM, ts), 0) == 0).astype(_F32)

    def widened_keys_t(k3):
        parts = []
        for hh in range(k3.shape[0]):
            parts += [k3[hh], one_hot_rows]
        return jnp.concatenate(parts, axis=0).T.astype(_BF16)

    ka = axial(head_norm(proj_t(_KA0, A_KV).reshape(A_KV_HEADS, HEAD_DIM, ts), gk_ref))
    kA_ref[...] = widened_keys_t(ka)

    va = proj_t(_VA0, A_KV).astype(_BF16)
    for g in range(A_KV_HEADS):
        for c in range(ts // tk):
            vTa_ref[g, c] = va[g * HEAD_DIM:(g + 1) * HEAD_DIM, c * tk:(c + 1) * tk]

    qb = proj_t(_QB0, B_QK).reshape(2 * B_HEADS, HEAD_DIM, ts)
    qTb_ref[...] = (partial(qb) * QK_SCALE).reshape(B_QK, ts).astype(_BF16)

    kb = partial(proj_t(_KB0, B_QK).reshape(2 * B_HEADS, HEAD_DIM, ts))
    kB_ref[...] = widened_keys_t(kb)

    vb = proj_t(_VB0, B_V).astype(_BF16)
    for hd in range(B_HEADS):
        for c in range(ts // tk):
            vTb_ref[hd, c] = vb[hd * B_V_DIM:(hd + 1) * B_V_DIM, c * tk:(c + 1) * tk]

    @pl.when(pl.program_id(1) == 0)
    def _():
        knorm_ref[...] = jnp.zeros(knorm_ref.shape, _F32)

    lanes = knorm_ref.shape[1]
    b_rows = slice(_KNORM_B0, _KNORM_B0 + 2 * B_HEADS)
    a_rows = slice(_KNORM_A0, _KNORM_A0 + A_KV_HEADS)
    knorm_ref[b_rows, :] = jnp.maximum(knorm_ref[b_rows, :], jnp.broadcast_to(max_key_norm2(kb), (2 * B_HEADS, lanes)))
    knorm_ref[a_rows, :] = jnp.maximum(knorm_ref[a_rows, :], jnp.broadcast_to(max_key_norm2(ka), (A_KV_HEADS, lanes)))


def _inproj(x, mod, gmix, w_inT, gq, gk, tab, *, ts, tk):
    n, s, d = x.shape
    nck = s // tk
    cpt = ts // tk
    out_shape = (
        jax.ShapeDtypeStruct((n, A_Q, s), _BF16),
        jax.ShapeDtypeStruct((n, s, 2 * A_KV), _BF16),
        jax.ShapeDtypeStruct((n, A_KV_HEADS, nck, HEAD_DIM, tk), _BF16),
        jax.ShapeDtypeStruct((n, B_QK, s), _BF16),
        jax.ShapeDtypeStruct((n, s, 2 * B_QK), _BF16),
        jax.ShapeDtypeStruct((n, B_HEADS, nck, B_V_DIM, tk), _BF16),
        jax.ShapeDtypeStruct((n, _KNORM_ROWS, 128), _F32),
    )
    return pl.pallas_call(
        functools.partial(_inproj_kernel, tk=tk),
        grid=(n, s // ts),
        in_specs=[
            pl.BlockSpec((None, ts, d), lambda b, t: (b, t, 0)),
            pl.BlockSpec((None, 6, d), lambda b, t: (b, 0, 0)),
            pl.BlockSpec((1, d), lambda b, t: (0, 0)),
            pl.BlockSpec((IN_WIDTH, d), lambda b, t: (0, 0)),
            pl.BlockSpec((HEAD_DIM, 1), lambda b, t: (0, 0)),
            pl.BlockSpec((HEAD_DIM, 1), lambda b, t: (0, 0)),
            pl.BlockSpec((80, ts), lambda b, t: (0, t)),
        ],
        out_specs=(
            pl.BlockSpec((None, A_Q, ts), lambda b, t: (b, 0, t)),
            pl.BlockSpec((None, ts, 2 * A_KV), lambda b, t: (b, t, 0)),
            pl.BlockSpec((None, A_KV_HEADS, cpt, HEAD_DIM, tk), lambda b, t: (b, 0, t, 0, 0)),
            pl.BlockSpec((None, B_QK, ts), lambda b, t: (b, 0, t)),
            pl.BlockSpec((None, ts, 2 * B_QK), lambda b, t: (b, t, 0)),
            pl.BlockSpec((None, B_HEADS, cpt, B_V_DIM, tk), lambda b, t: (b, 0, t, 0, 0)),
            pl.BlockSpec((None, _KNORM_ROWS, 128), lambda b, t: (b, 0, 0)),
        ),
        out_shape=out_shape,
        compiler_params=_params(("parallel", "arbitrary")),
        name="mix_in_projection",
    )(x, mod, gmix, w_inT, gq, gk, tab)


def _softmax_value_loop(qT_ref, k_ref, vT_ref, knorm2_of_row, key_head_of_row, value_head_of_row, w_ref,
                        s_even_ref, s_odd_ref, m_ref, l_ref, acc_ref, *, qrows, tq, tk):
    nck = vT_ref.shape[1]
    nchains = qrows * (qT_ref.shape[-1] // tq)
    knorm2_of_chain = lambda c: knorm2_of_row(c % qrows)
    key_head_of_chain = lambda c: key_head_of_row(c % qrows)
    value_head_of_chain = lambda c: value_head_of_row(c % qrows)
    acc_ref[...] = jnp.zeros(acc_ref.shape, _F32)
    l_ref[...] = jnp.zeros(l_ref.shape, _F32)

    qs = [qT_ref[(c % qrows) * HEAD_DIM:(c % qrows + 1) * HEAD_DIM, (c // qrows) * tq:(c // qrows + 1) * tq]
          for c in range(nchains)]
    bounds = []
    for c in range(nchains):
        qf = qs[c].astype(_F32)
        bounds.append(jnp.sqrt(jnp.sum(qf * qf, axis=0, keepdims=True) * knorm2_of_chain(c)))
    use_bound = jnp.max(jnp.concatenate(bounds, axis=0)) <= FAST_SOFTMAX_MAX_BOUND

    first_row = lax.broadcasted_iota(jnp.int32, (OFFSET_ROWS, tq), 0) == 0
    for c in range(nchains):
        offset = -jnp.minimum(bounds[c], FAST_SOFTMAX_MAX_BOUND)
        offset_rows = jnp.where(first_row, jnp.broadcast_to(offset, (OFFSET_ROWS, tq)), 0.0)
        w_ref[c] = jnp.concatenate([qs[c], offset_rows.astype(_BF16)], axis=0)

    def k_chunk(j, c):
        lane0 = key_head_of_chain(c) * 2 * HEAD_DIM
        return k_ref[pl.ds(pl.multiple_of(j * tk, tk), tk), lane0:lane0 + HEAD_DIM + OFFSET_ROWS]

    def partial_sums(p):
        return jnp.sum(p.reshape(tk // 8, 8, tq), axis=0)

    assert nchains >= FAST_LOOKAHEAD
    for a in range(FAST_LOOKAHEAD):
        s_even_ref[a] = _dot(k_chunk(0, a), w_ref[a])

    @pl.when(use_bound)
    def _():
        def scores(j, step):
            dj, c = divmod(step, nchains)
            jj = j + dj
            if dj and not isinstance(jj, int):
                jj = jnp.where(jj >= nck, jj - nck, jj)
            return _dot(k_chunk(jj, c), w_ref[c])

        chunks_per_trip = min(FAST_STEPS_PER_TRIP // nchains, nck)
        assert nck % chunks_per_trip == 0

        single_trip = chunks_per_trip == nck

        def body(i, carry):
            j0 = i * chunks_per_trip
            pending = [s_even_ref[a] for a in range(FAST_LOOKAHEAD)]
            for u in range(chunks_per_trip):
                for c in range(nchains):
                    ahead = u * nchains + c + FAST_LOOKAHEAD
                    if not (single_trip and ahead >= nck * nchains):
                        pending.append(scores(j0, ahead))
                    p = jnp.exp2(pending.pop(0))
                    l_ref[c] += partial_sums(p)
                    acc_ref[c] += _dot(vT_ref[value_head_of_chain(c), j0 + u], p.astype(_BF16))
            for a, tile in enumerate(pending):
                s_even_ref[a] = tile
            return carry

        if single_trip:
            body(0, 0)
        else:
            lax.fori_loop(0, nck // chunks_per_trip, body, 0)

    @pl.when(jnp.logical_not(use_bound))
    def _():
        m_ref[...] = jnp.full(m_ref.shape, -jnp.inf, _F32)

        def phase(j_cur, j_next, s_cur_ref, s_next_ref):
            for c in range(nchains):
                vc = vT_ref[value_head_of_chain(c), j_cur]
                s = s_cur_ref[c]
                s_next_ref[c] = _dot(k_chunk(j_next, c), w_ref[c])
                m_prev = m_ref[c]
                m_new = jnp.maximum(m_prev, jnp.max(s, axis=0, keepdims=True))
                alpha = jnp.exp2(m_prev - m_new)
                p = jnp.exp2(s - m_new)
                l_ref[c] = alpha * l_ref[c] + partial_sums(p)
                acc_ref[c] = alpha * acc_ref[c] + _dot(vc, p.astype(_BF16))
                m_ref[c] = m_new

        for c in range(FAST_LOOKAHEAD, nchains):
            s_even_ref[c] = _dot(k_chunk(0, c), w_ref[c])

        def body(i, carry):
            j = 2 * i
            phase(j, j + 1, s_even_ref, s_odd_ref)
            phase(j + 1, jnp.where(j + 2 == nck, 0, j + 2), s_odd_ref, s_even_ref)
            return carry

        lax.fori_loop(0, nck // 2, body, 0)


def _denominator(l_ref, c):
    return jnp.sum(l_ref[c], axis=0, keepdims=True)


def _attn_a_kernel(qT_ref, k_ref, vT_ref, knorm_ref, o_ref, w_ref, s_even_ref, s_odd_ref, m_ref, l_ref, acc_ref,
                   *, tq, tk):
    knorm2 = knorm_ref[_KNORM_A0:_KNORM_A0 + A_KV_HEADS, 0:1]
    kv_head = lambda r: r // A_GROUP
    _softmax_value_loop(qT_ref, k_ref, vT_ref, lambda r: knorm2[kv_head(r):kv_head(r) + 1], kv_head, kv_head,
                        w_ref, s_even_ref, s_odd_ref, m_ref, l_ref, acc_ref, qrows=A_HEADS, tq=tq, tk=tk)
    for c in range(acc_ref.shape[0]):
        blk, r = divmod(c, A_HEADS)
        o = acc_ref[c] / _denominator(l_ref, c)
        o_ref[r * HEAD_DIM:(r + 1) * HEAD_DIM, blk * tq:(blk + 1) * tq] = o.astype(_BF16)


def _attention_scratch(nchains, dv, tq, tk):
    return [
        pltpu.VMEM((nchains, HEAD_DIM + OFFSET_ROWS, tq), _BF16),
        pltpu.VMEM((max(nchains, FAST_LOOKAHEAD), tk, tq), _F32),
        pltpu.VMEM((nchains, tk, tq), _F32),
        pltpu.VMEM((nchains, 1, tq), _F32),
        pltpu.VMEM((nchains, 8, tq), _F32),
        pltpu.VMEM((nchains, dv, tq), _F32),
    ]


def _attn_a(qTa, kA, vTa, knorm, *, tq):
    n, _, s = qTa.shape
    _, nkv, nck, dv, tk = vTa.shape
    nblk = QUERY_BLOCKS_PER_STEP
    return pl.pallas_call(
        functools.partial(_attn_a_kernel, tq=tq, tk=tk),
        grid=(n, s // (nblk * tq)),
        in_specs=[
            pl.BlockSpec((None, A_Q, nblk * tq), lambda b, i: (b, 0, i)),
            _once_per_row((None, s, nkv * 2 * HEAD_DIM), lambda b, i: (b, 0, 0)),
            _once_per_row((None, nkv, nck, dv, tk), lambda b, i: (b, 0, 0, 0, 0)),
            pl.BlockSpec((None, _KNORM_ROWS, 128), lambda b, i: (b, 0, 0)),
        ],
        out_specs=pl.BlockSpec((None, A_Q, nblk * tq), lambda b, i: (b, 0, i)),
        out_shape=jax.ShapeDtypeStruct((n, A_Q, s), _BF16),
        scratch_shapes=_attention_scratch(nblk * A_HEADS, HEAD_DIM, tq, tk),
        compiler_params=_params(("parallel", "parallel")),
        name="gqa_attention",
    )(qTa, kA, vTa, knorm)


def _attn_b_kernel(qT_ref, k_ref, vT_ref, knorm_ref, lam_ref, gsub_ref, o_ref, w_ref, s_even_ref, s_odd_ref,
                   m_ref, l_ref, acc_ref, *, tq, tk, lam_init):
    qrows = 2 * B_HEADS_PER_STEP
    knorm2 = knorm_ref[pl.ds(_KNORM_B0 + qrows * pl.program_id(1), qrows), 0:1]
    _softmax_value_loop(qT_ref, k_ref, vT_ref, lambda r: knorm2[r:r + 1], lambda r: r, lambda r: r // 2,
                        w_ref, s_even_ref, s_odd_ref, m_ref, l_ref, acc_ref, qrows=qrows, tq=tq, tk=tk)

    lp = lam_ref[...]
    lam = (jnp.exp(jnp.sum(lp[0:1] * lp[1:2], axis=1, keepdims=True))
           - jnp.exp(jnp.sum(lp[2:3] * lp[3:4], axis=1, keepdims=True)) + lam_init)
    for blk in range(acc_ref.shape[0] // qrows):
        for h in range(B_HEADS_PER_STEP):
            c1 = blk * qrows + 2 * h
            o = acc_ref[c1] / _denominator(l_ref, c1) - lam * (acc_ref[c1 + 1] / _denominator(l_ref, c1 + 1))
            ms = jnp.mean(o * o, axis=0, keepdims=True)
            o = ((o * lax.rsqrt(ms + NORM_EPS)) * gsub_ref[...]) * (1.0 - lam_init)
            o_ref[h * B_V_DIM:(h + 1) * B_V_DIM, blk * tq:(blk + 1) * tq] = o.astype(_BF16)


def _attn_b(qTb, kB, vTb, knorm, lam_params, gsub, *, tq, lam_init):
    n, _, s = qTb.shape
    _, _, nck, dv, tk = vTb.shape
    hps = B_HEADS_PER_STEP
    nblk = QUERY_BLOCKS_PER_STEP
    return pl.pallas_call(
        functools.partial(_attn_b_kernel, tq=tq, tk=tk, lam_init=lam_init),
        grid=(n, B_HEADS // hps, s // (nblk * tq)),
        in_specs=[
            pl.BlockSpec((None, hps * 2 * HEAD_DIM, nblk * tq), lambda b, hp, i: (b, hp, i)),
            _once_per_row((None, s, hps * 4 * HEAD_DIM), lambda b, hp, i: (b, 0, hp)),
            _once_per_row((None, hps, nck, dv, tk), lambda b, hp, i: (b, hp, 0, 0, 0)),
            pl.BlockSpec((None, _KNORM_ROWS, 128), lambda b, hp, i: (b, 0, 0)),
            pl.BlockSpec((4, HEAD_DIM), lambda b, hp, i: (0, 0)),
            pl.BlockSpec((B_V_DIM, 1), lambda b, hp, i: (0, 0)),
        ],
        out_specs=pl.BlockSpec((None, hps * B_V_DIM, nblk * tq), lambda b, hp, i: (b, hp, i)),
        out_shape=jax.ShapeDtypeStruct((n, B_V, s), _BF16),
        scratch_shapes=_attention_scratch(nblk * 2 * hps, B_V_DIM, tq, tk),
        compiler_params=_params(("parallel", "parallel", "parallel")),
        name="diff_attention",
    )(qTb, kB, vTb, knorm, lam_params, gsub)


def _mix_and_ffn(x_ref, oTa_ref, oTb_ref, mod_ref, wout_ref, gffn_ref, wgu_ref, wdn_ref, act_ref):
    mixed = _dot_tn(oTa_ref[...], wout_ref[0:A_Q, :]) + _dot_tn(oTb_ref[...], wout_ref[A_Q:, :])
    gt1, sh2, sc2, gt2 = mod_ref[2:3, :], mod_ref[3:4, :], mod_ref[4:5, :], mod_ref[5:6, :]
    x1 = x_ref[...] + gt1 * mixed
    h = ((x1 * _rms_rows(x1)) * gffn_ref[...]) * (1.0 + sc2) + sh2
    hb = h.astype(_BF16)
    for c in range(wgu_ref.shape[0]):
        gu = _dot(hb, wgu_ref[c])
        gate, up = gu[:, :FF_CHUNK], gu[:, FF_CHUNK:]
        act_ref[:, c * FF_CHUNK:(c + 1) * FF_CHUNK] = ((gate * jax.nn.sigmoid(gate)) * up).astype(_BF16)
    return x1 + gt2 * _dot(act_ref[...], wdn_ref[...])


def _ffn_kernel(x_ref, oTa_ref, oTb_ref, mod_ref, wout_ref, gffn_ref, wgu_ref, wdn_ref, y_ref, act_ref):
    y_ref[...] = _mix_and_ffn(x_ref, oTa_ref, oTb_ref, mod_ref, wout_ref, gffn_ref, wgu_ref, wdn_ref, act_ref)


def _ffn_final_kernel(x_ref, oTa_ref, oTb_ref, mod_ref, wout_ref, gffn_ref, wgu_ref, wdn_ref, modf_ref, gfin_ref,
                      y_ref, act_ref):
    x = _mix_and_ffn(x_ref, oTa_ref, oTb_ref, mod_ref, wout_ref, gffn_ref, wgu_ref, wdn_ref, act_ref)
    shift, scale = modf_ref[0:1, :], modf_ref[1:2, :]
    y_ref[...] = ((x * _rms_rows(x)) * gfin_ref[...]) * (1.0 + scale) + shift


def _resident(shape):
    return pl.BlockSpec(shape, lambda b, t: (0,) * len(shape), pipeline_mode=pl.Buffered(1))


def _out_ffn(x, oTa, oTb, mod, w_out, gffn, w_gu, w_dn, final=None, *, ts):
    n, s, d = x.shape
    nff, _, _ = w_gu.shape
    dff = w_dn.shape[0]
    in_specs = [
        pl.BlockSpec((None, ts, d), lambda b, t: (b, t, 0)),
        pl.BlockSpec((None, A_Q, ts), lambda b, t: (b, 0, t)),
        pl.BlockSpec((None, B_V, ts), lambda b, t: (b, 0, t)),
        pl.BlockSpec((None, 6, d), lambda b, t: (b, 0, 0)),
        _resident((A_Q + B_V, d)),
        pl.BlockSpec((1, d), lambda b, t: (0, 0)),
        _resident((nff, d, 2 * FF_CHUNK)),
        _resident((dff, d)),
    ]
    operands = (x, oTa, oTb, mod, w_out, gffn, w_gu, w_dn)
    if final is not None:
        in_specs += [pl.BlockSpec((None, 2, d), lambda b, t: (b, 0, 0)), pl.BlockSpec((1, d), lambda b, t: (0, 0))]
        operands += tuple(final)
    return pl.pallas_call(
        _ffn_kernel if final is None else _ffn_final_kernel,
        grid=(n, s // ts),
        in_specs=in_specs,
        out_specs=pl.BlockSpec((None, ts, d), lambda b, t: (b, t, 0)),
        out_shape=jax.ShapeDtypeStruct((n, s, d), _F32),
        scratch_shapes=[pltpu.VMEM((ts, dff), _BF16)],
        compiler_params=_params(("parallel", "parallel")),
        name="out_proj_swiglu",
    )(*operands)


def _rope_table_t(s):
    def tables(pos, dim, theta):
        inv = 1.0 / (theta ** (jnp.arange(0, dim, 2, dtype=_F32) / dim))
        ang = pos.astype(_F32)[:, None] * inv[None, :]
        return jnp.cos(ang).T, jnp.sin(ang).T

    t = jnp.arange(s)
    cos_r, sin_r = tables(t // GRID_W, HEAD_DIM // 2, AXIAL_THETA)
    cos_c, sin_c = tables(t % GRID_W, HEAD_DIM // 2, AXIAL_THETA)
    cos_p, sin_p = tables(t, PARTIAL_ROT, ROPE_THETA)
    return jnp.concatenate([cos_r, sin_r, cos_c, sin_c, cos_p, sin_p], axis=0)


def _trunks(xs, cs, w_ada, b_ada, g_mix, w_in, g_q_a, g_k_a, lam_q1, lam_k1, lam_q2, lam_k2, g_subln,
            w_out, g_ffn, w_gate_up, w_down, w_ada_final, b_ada_final, g_final, *, ts, tq, tk):
    d = xs[0].shape[-1]
    depth = w_ada.shape[0]
    dff = w_down.shape[1]
    nff = dff // FF_CHUNK

    rows = [c.shape[0] for c in cs]
    c_pad = jnp.zeros((8, d), _F32).at[:sum(rows)].set(jnp.concatenate(cs, axis=0))
    mod_all = _modulation(c_pad, w_ada, b_ada[:, None, :])
    modf_all = _modulation(c_pad, w_ada_final[None], b_ada_final[None, None, :])[0]

    w_inT = jnp.swapaxes(w_in, 1, 2).astype(_BF16)
    w_out_b = w_out.astype(_BF16)
    w_gu = jnp.concatenate([w_gate_up[:, :, :dff].reshape(depth, d, nff, FF_CHUNK),
                            w_gate_up[:, :, dff:].reshape(depth, d, nff, FF_CHUNK)], axis=-1)
    w_gu = jnp.transpose(w_gu, (0, 2, 1, 3)).astype(_BF16)
    w_dn = w_down.astype(_BF16)
    lam_params = jnp.stack([lam_q1, lam_k1, lam_q2, lam_k2], axis=1)

    outs = []
    row0 = 0
    for x, n in zip(xs, rows):
        s = x.shape[1]
        mod = mod_all[:, row0:row0 + n].reshape(depth, n, 6, d)
        modf = modf_all[row0:row0 + n].reshape(n, 2, d)
        row0 += n
        tab = _rope_table_t(s)
        for l in range(depth):
            qTa, kA, vTa, qTb, kB, vTb, knorm = _inproj(
                x, mod[l], g_mix[l][None, :], w_inT[l], g_q_a[l][:, None], g_k_a[l][:, None], tab, ts=ts, tk=tk)
            oTa = _attn_a(qTa, kA, vTa, knorm, tq=tq)
            lam_init = 0.8 - 0.6 * math.exp(-0.3 * l)
            oTb = _attn_b(qTb, kB, vTb, knorm, lam_params[l], g_subln[l][:, None], tq=tq, lam_init=lam_init)
            final = (modf, g_final[None, :]) if l == depth - 1 else None
            x = _out_ffn(x, oTa, oTb, mod[l], w_out_b[l], g_ffn[l][None, :], w_gu[l], w_dn[l], final, ts=ts)
        outs.append(x)
    return tuple(outs)


def _trunk(x, c, *weights, ts, tq, tk):
    return _trunks((x,), (c,), *weights, ts=ts, tq=tq, tk=tk)[0]


def kernel(x_prompt, x_sample, c_prompt, c_sample, w_ada, b_ada, g_mix, w_in, g_q_a, g_k_a, lam_q1, lam_k1, lam_q2, lam_k2, g_subln, w_out, g_ffn, w_gate_up, w_down, w_ada_final, b_ada_final, g_final):
    return _trunks((x_prompt, x_sample), (c_prompt, c_sample), w_ada, b_ada, g_mix, w_in, g_q_a, g_k_a,
                   lam_q1, lam_k1, lam_q2, lam_k2, g_subln, w_out, g_ffn, w_gate_up, w_down,
                   w_ada_final, b_ada_final, g_final, ts=512, tq=256, tk=256)
```

```python
import functools
import math

import jax
import jax.numpy as jnp
from jax import lax
from jax.experimental import pallas as pl
from jax.experimental.pallas import tpu as pltpu

HEAD_DIM = 64
A_HEADS = 8
A_KV_HEADS = 2
A_GROUP = A_HEADS // A_KV_HEADS
B_HEADS = 4
B_V_DIM = 2 * HEAD_DIM
A_Q = A_HEADS * HEAD_DIM
A_KV = A_KV_HEADS * HEAD_DIM
B_QK = B_HEADS * 2 * HEAD_DIM
B_V = B_HEADS * B_V_DIM
IN_WIDTH = A_Q + 2 * A_KV + 2 * B_QK + B_V
GRID_W = 64
AXIAL_THETA = 10000.0
ROPE_THETA = 500000.0
PARTIAL_ROT = HEAD_DIM // 4
NORM_EPS = 1e-6
QK_SCALE = HEAD_DIM ** -0.5 * math.log2(math.e)

_QA0, _KA0, _VA0 = 0, A_Q, A_Q + A_KV
_QB0 = A_Q + 2 * A_KV
_KB0 = _QB0 + B_QK
_VB0 = _KB0 + B_QK

_KNORM_B0, _KNORM_A0, _KNORM_ROWS = 0, 2 * B_HEADS, 16
FAST_SOFTMAX_MAX_BOUND = 48.0
OFFSET_ROWS = 16
B_HEADS_PER_STEP = 2
QUERY_BLOCKS_PER_STEP = 2
FAST_STEPS_PER_TRIP = 256
FAST_LOOKAHEAD = 4

FF_CHUNK = 256
V7X_VMEM_LIMIT_BYTES = 56 * 1024 * 1024

_BF16 = jnp.bfloat16
_F32 = jnp.float32


def _params(semantics):
    return pltpu.CompilerParams(dimension_semantics=semantics, vmem_limit_bytes=V7X_VMEM_LIMIT_BYTES)


def _once_per_row(block_shape, index_map):
    return pl.BlockSpec(block_shape, index_map, pipeline_mode=pl.Buffered(1))


def _dot(a, b):
    return jnp.dot(a, b, preferred_element_type=_F32)


def _dot_nt(a, b):
    return lax.dot_general(a, b, (((1,), (1,)), ((), ())), preferred_element_type=_F32)


def _dot_tn(a, b):
    return lax.dot_general(a, b, (((0,), (0,)), ((), ())), preferred_element_type=_F32)


def _split_bf16(x):
    hi = x.astype(_BF16)
    lo = (x - hi.astype(_F32)).astype(_BF16)
    return hi, lo


def _mod_kernel(c_ref, w_ref, b_ref, o_ref):
    c = c_ref[...]
    c_act = c * jax.nn.sigmoid(c)
    c_hi, c_lo = _split_bf16(c_act)
    w_hi, w_lo = _split_bf16(w_ref[...])
    acc = _dot(c_hi, w_hi) + (_dot(c_lo, w_hi) + _dot(c_hi, w_lo))
    o_ref[...] = acc + b_ref[...]


def _modulation(c_pad, w, b, tn=1024):
    nl, d, n = w.shape
    return pl.pallas_call(
        _mod_kernel,
        grid=(nl, n // tn),
        in_specs=[
            pl.BlockSpec((8, d), lambda l, j: (0, 0)),
            pl.BlockSpec((None, d, tn), lambda l, j: (l, 0, j)),
            pl.BlockSpec((None, 1, tn), lambda l, j: (l, 0, j)),
        ],
        out_specs=pl.BlockSpec((None, 8, tn), lambda l, j: (l, 0, j)),
        out_shape=jax.ShapeDtypeStruct((nl, 8, n), _F32),
        compiler_params=_params(("parallel", "parallel")),
        name="adaln_modulation",
    )(c_pad, w, b)


def _rms_rows(x):
    return lax.rsqrt(jnp.mean(x * x, axis=1, keepdims=True) + NORM_EPS)


def _inproj_kernel(x_ref, mod_ref, gmix_ref, wT_ref, gq_ref, gk_ref, tab_ref,
                   qTa_ref, kA_ref, vTa_ref, qTb_ref, kB_ref, vTb_ref, knorm_ref, *, tk):
    ts = x_ref.shape[0]
    sh1 = mod_ref[0:1, :]
    sc1 = mod_ref[1:2, :]

    @pl.when(pl.program_id(1) == 0)
    def _():
        knorm_ref[...] = jnp.zeros(knorm_ref.shape, _F32)

    def head_norm(x3, g_ref):
        ms = jnp.mean(x3 * x3, axis=1, keepdims=True)
        return (x3 * lax.rsqrt(ms + NORM_EPS)) * g_ref[...][None]

    def max_key_norm2(k3):
        kf = k3.astype(_BF16).astype(_F32)
        return jnp.max(jnp.sum(kf * kf, axis=1), axis=1, keepdims=True)

    one_hot_rows = (lax.broadcasted_iota(jnp.int32, (HEAD_DIM, tk), 0) == 0).astype(_F32)

    def widened_keys_t(k3):
        parts = []
        for hh in range(k3.shape[0]):
            parts += [k3[hh], one_hot_rows]
        return jnp.concatenate(parts, axis=0).T.astype(_BF16)

    for c in range(ts // tk):
        tok = slice(c * tk, (c + 1) * tk)
        x = x_ref[tok, :]
        h = ((x * _rms_rows(x)) * gmix_ref[...]) * (1.0 + sc1) + sh1
        hb = h.astype(_BF16)

        def proj_t(lo, n):
            return _dot_nt(wT_ref[lo:lo + n, :], hb)

        tab = tab_ref[:, tok]
        cos_r, sin_r, cos_c, sin_c = tab[0:16], tab[16:32], tab[32:48], tab[48:64]
        cos_p, sin_p = tab[64:72], tab[72:80]

        def axial(x3):
            r1, r2, c1, c2 = x3[:, 0:16], x3[:, 16:32], x3[:, 32:48], x3[:, 48:64]
            return jnp.concatenate([r1 * cos_r - r2 * sin_r, r1 * sin_r + r2 * cos_r,
                                    c1 * cos_c - c2 * sin_c, c1 * sin_c + c2 * cos_c], axis=1)

        def partial(x3):
            half = PARTIAL_ROT // 2
            x1, x2, rest = x3[:, 0:half], x3[:, half:PARTIAL_ROT], x3[:, PARTIAL_ROT:]
            return jnp.concatenate([x1 * cos_p - x2 * sin_p, x1 * sin_p + x2 * cos_p, rest], axis=1)

        qa = proj_t(_QA0, A_Q).reshape(A_HEADS, HEAD_DIM, tk)
        qa = axial(head_norm(qa, gq_ref)) * QK_SCALE
        qTa_ref[:, tok] = qa.reshape(A_Q, tk).astype(_BF16)

        ka = axial(head_norm(proj_t(_KA0, A_KV).reshape(A_KV_HEADS, HEAD_DIM, tk), gk_ref))
        kA_ref[tok, :] = widened_keys_t(ka)

        va = proj_t(_VA0, A_KV).astype(_BF16)
        for g in range(A_KV_HEADS):
            vTa_ref[g, c] = va[g * HEAD_DIM:(g + 1) * HEAD_DIM]

        qb = proj_t(_QB0, B_QK).reshape(2 * B_HEADS, HEAD_DIM, tk)
        qTb_ref[:, tok] = (partial(qb) * QK_SCALE).reshape(B_QK, tk).astype(_BF16)

        kb = partial(proj_t(_KB0, B_QK).reshape(2 * B_HEADS, HEAD_DIM, tk))
        kB_ref[tok, :] = widened_keys_t(kb)

        vb = proj_t(_VB0, B_V).astype(_BF16)
        for hd in range(B_HEADS):
            vTb_ref[hd, c] = vb[hd * B_V_DIM:(hd + 1) * B_V_DIM]

        lanes = knorm_ref.shape[1]
        b_rows = slice(_KNORM_B0, _KNORM_B0 + 2 * B_HEADS)
        a_rows = slice(_KNORM_A0, _KNORM_A0 + A_KV_HEADS)
        knorm_ref[b_rows, :] = jnp.maximum(knorm_ref[b_rows, :],
                                           jnp.broadcast_to(max_key_norm2(kb), (2 * B_HEADS, lanes)))
        knorm_ref[a_rows, :] = jnp.maximum(knorm_ref[a_rows, :],
                                           jnp.broadcast_to(max_key_norm2(ka), (A_KV_HEADS, lanes)))


def _inproj(x, mod, gmix, w_inT, gq, gk, tab, *, ts, tk):
    n, s, d = x.shape
    nck = s // tk
    cpt = ts // tk
    out_shape = (
        jax.ShapeDtypeStruct((n, A_Q, s), _BF16),
        jax.ShapeDtypeStruct((n, s, 2 * A_KV), _BF16),
        jax.ShapeDtypeStruct((n, A_KV_HEADS, nck, HEAD_DIM, tk), _BF16),
        jax.ShapeDtypeStruct((n, B_QK, s), _BF16),
        jax.ShapeDtypeStruct((n, s, 2 * B_QK), _BF16),
        jax.ShapeDtypeStruct((n, B_HEADS, nck, B_V_DIM, tk), _BF16),
        jax.ShapeDtypeStruct((n, _KNORM_ROWS, 128), _F32),
    )
    return pl.pallas_call(
        functools.partial(_inproj_kernel, tk=tk),
        grid=(n, s // ts),
        in_specs=[
            pl.BlockSpec((None, ts, d), lambda b, t: (b, t, 0)),
            pl.BlockSpec((None, 6, d), lambda b, t: (b, 0, 0)),
            pl.BlockSpec((1, d), lambda b, t: (0, 0)),
            pl.BlockSpec((IN_WIDTH, d), lambda b, t: (0, 0)),
            pl.BlockSpec((HEAD_DIM, 1), lambda b, t: (0, 0)),
            pl.BlockSpec((HEAD_DIM, 1), lambda b, t: (0, 0)),
            pl.BlockSpec((80, ts), lambda b, t: (0, t)),
        ],
        out_specs=(
            pl.BlockSpec((None, A_Q, ts), lambda b, t: (b, 0, t)),
            pl.BlockSpec((None, ts, 2 * A_KV), lambda b, t: (b, t, 0)),
            pl.BlockSpec((None, A_KV_HEADS, cpt, HEAD_DIM, tk), lambda b, t: (b, 0, t, 0, 0)),
            pl.BlockSpec((None, B_QK, ts), lambda b, t: (b, 0, t)),
            pl.BlockSpec((None, ts, 2 * B_QK), lambda b, t: (b, t, 0)),
            pl.BlockSpec((None, B_HEADS, cpt, B_V_DIM, tk), lambda b, t: (b, 0, t, 0, 0)),
            pl.BlockSpec((None, _KNORM_ROWS, 128), lambda b, t: (b, 0, 0)),
        ),
        out_shape=out_shape,
        compiler_params=_params(("parallel", "arbitrary")),
        name="mix_in_projection",
    )(x, mod, gmix, w_inT, gq, gk, tab)


def _softmax_value_loop(qT_ref, k_ref, vT_ref, knorm2_of_row, key_head_of_row, value_head_of_row, w_ref,
                        s_even_ref, s_odd_ref, m_ref, l_ref, acc_ref, *, qrows, tq, tk):
    nck = vT_ref.shape[1]
    nchains = qrows * (qT_ref.shape[-1] // tq)
    knorm2_of_chain = lambda c: knorm2_of_row(c % qrows)
    key_head_of_chain = lambda c: key_head_of_row(c % qrows)
    value_head_of_chain = lambda c: value_head_of_row(c % qrows)
    acc_ref[...] = jnp.zeros(acc_ref.shape, _F32)
    l_ref[...] = jnp.zeros(l_ref.shape, _F32)

    qs = [qT_ref[(c % qrows) * HEAD_DIM:(c % qrows + 1) * HEAD_DIM, (c // qrows) * tq:(c // qrows + 1) * tq]
          for c in range(nchains)]
    bounds = []
    for c in range(nchains):
        qf = qs[c].astype(_F32)
        bounds.append(jnp.sqrt(jnp.sum(qf * qf, axis=0, keepdims=True) * knorm2_of_chain(c)))
    use_bound = jnp.max(jnp.concatenate(bounds, axis=0)) <= FAST_SOFTMAX_MAX_BOUND

    first_row = lax.broadcasted_iota(jnp.int32, (OFFSET_ROWS, tq), 0) == 0
    for c in range(nchains):
        offset = -jnp.minimum(bounds[c], FAST_SOFTMAX_MAX_BOUND)
        offset_rows = jnp.where(first_row, jnp.broadcast_to(offset, (OFFSET_ROWS, tq)), 0.0)
        w_ref[c] = jnp.concatenate([qs[c], offset_rows.astype(_BF16)], axis=0)

    def k_chunk(j, c):
        lane0 = key_head_of_chain(c) * 2 * HEAD_DIM
        return k_ref[pl.ds(pl.multiple_of(j * tk, tk), tk), lane0:lane0 + HEAD_DIM + OFFSET_ROWS]

    def partial_sums(p):
        return jnp.sum(p.reshape(tk // 8, 8, tq), axis=0)

    assert nchains >= FAST_LOOKAHEAD
    for a in range(FAST_LOOKAHEAD):
        s_even_ref[a] = _dot(k_chunk(0, a), w_ref[a])

    @pl.when(use_bound)
    def _():
        def scores(j, step):
            dj, c = divmod(step, nchains)
            jj = j + dj
            if dj and not isinstance(jj, int):
                jj = jnp.where(jj >= nck, jj - nck, jj)
            return _dot(k_chunk(jj, c), w_ref[c])

        chunks_per_trip = min(FAST_STEPS_PER_TRIP // nchains, nck)
        assert nck % chunks_per_trip == 0

        single_trip = chunks_per_trip == nck

        def body(i, carry):
            j0 = i * chunks_per_trip
            pending = [s_even_ref[a] for a in range(FAST_LOOKAHEAD)]
            for u in range(chunks_per_trip):
                for c in range(nchains):
                    ahead = u * nchains + c + FAST_LOOKAHEAD
                    if not (single_trip and ahead >= nck * nchains):
                        pending.append(scores(j0, ahead))
                    p = jnp.exp2(pending.pop(0))
                    l_ref[c] += partial_sums(p)
                    acc_ref[c] += _dot(vT_ref[value_head_of_chain(c), j0 + u], p.astype(_BF16))
            for a, tile in enumerate(pending):
                s_even_ref[a] = tile
            return carry

        if single_trip:
            body(0, 0)
        else:
            lax.fori_loop(0, nck // chunks_per_trip, body, 0)

    @pl.when(jnp.logical_not(use_bound))
    def _():
        m_ref[...] = jnp.full(m_ref.shape, -jnp.inf, _F32)

        def phase(j_cur, j_next, s_cur_ref, s_next_ref):
            for c in range(nchains):
                vc = vT_ref[value_head_of_chain(c), j_cur]
                s = s_cur_ref[c]
                s_next_ref[c] = _dot(k_chunk(j_next, c), w_ref[c])
                m_prev = m_ref[c]
                m_new = jnp.maximum(m_prev, jnp.max(s, axis=0, keepdims=True))
                alpha = jnp.exp2(m_prev - m_new)
                p = jnp.exp2(s - m_new)
                l_ref[c] = alpha * l_ref[c] + partial_sums(p)
                acc_ref[c] = alpha * acc_ref[c] + _dot(vc, p.astype(_BF16))
                m_ref[c] = m_new

        for c in range(FAST_LOOKAHEAD, nchains):
            s_even_ref[c] = _dot(k_chunk(0, c), w_ref[c])

        def body(i, carry):
            j = 2 * i
            phase(j, j + 1, s_even_ref, s_odd_ref)
            phase(j + 1, jnp.where(j + 2 == nck, 0, j + 2), s_odd_ref, s_even_ref)
            return carry

        lax.fori_loop(0, nck // 2, body, 0)


def _denominator(l_ref, c):
    return jnp.sum(l_ref[c], axis=0, keepdims=True)


def _attn_a_kernel(qT_ref, k_ref, vT_ref, knorm_ref, o_ref, w_ref, s_even_ref, s_odd_ref, m_ref, l_ref, acc_ref,
                   *, tq, tk):
    knorm2 = knorm_ref[_KNORM_A0:_KNORM_A0 + A_KV_HEADS, 0:1]
    kv_head = lambda r: r // A_GROUP
    _softmax_value_loop(qT_ref, k_ref, vT_ref, lambda r: knorm2[kv_head(r):kv_head(r) + 1], kv_head, kv_head,
                        w_ref, s_even_ref, s_odd_ref, m_ref, l_ref, acc_ref, qrows=A_HEADS, tq=tq, tk=tk)
    for c in range(acc_ref.shape[0]):
        blk, r = divmod(c, A_HEADS)
        o = acc_ref[c] / _denominator(l_ref, c)
        o_ref[r * HEAD_DIM:(r + 1) * HEAD_DIM, blk * tq:(blk + 1) * tq] = o.astype(_BF16)


def _attention_scratch(nchains, dv, tq, tk):
    return [
        pltpu.VMEM((nchains, HEAD_DIM + OFFSET_ROWS, tq), _BF16),
        pltpu.VMEM((max(nchains, FAST_LOOKAHEAD), tk, tq), _F32),
        pltpu.VMEM((nchains, tk, tq), _F32),
        pltpu.VMEM((nchains, 1, tq), _F32),
        pltpu.VMEM((nchains, 8, tq), _F32),
        pltpu.VMEM((nchains, dv, tq), _F32),
    ]


def _attn_a(qTa, kA, vTa, knorm, *, tq):
    n, _, s = qTa.shape
    _, nkv, nck, dv, tk = vTa.shape
    nblk = QUERY_BLOCKS_PER_STEP
    return pl.pallas_call(
        functools.partial(_attn_a_kernel, tq=tq, tk=tk),
        grid=(n, s // (nblk * tq)),
        in_specs=[
            pl.BlockSpec((None, A_Q, nblk * tq), lambda b, i: (b, 0, i)),
            _once_per_row((None, s, nkv * 2 * HEAD_DIM), lambda b, i: (b, 0, 0)),
            _once_per_row((None, nkv, nck, dv, tk), lambda b, i: (b, 0, 0, 0, 0)),
            pl.BlockSpec((None, _KNORM_ROWS, 128), lambda b, i: (b, 0, 0)),
        ],
        out_specs=pl.BlockSpec((None, A_Q, nblk * tq), lambda b, i: (b, 0, i)),
        out_shape=jax.ShapeDtypeStruct((n, A_Q, s), _BF16),
        scratch_shapes=_attention_scratch(nblk * A_HEADS, HEAD_DIM, tq, tk),
        compiler_params=_params(("parallel", "parallel")),
        name="gqa_attention",
    )(qTa, kA, vTa, knorm)


def _attn_b_kernel(qT_ref, k_ref, vT_ref, knorm_ref, lam_ref, gsub_ref, o_ref, w_ref, s_even_ref, s_odd_ref,
                   m_ref, l_ref, acc_ref, *, tq, tk, lam_init):
    qrows = 2 * B_HEADS_PER_STEP
    knorm2 = knorm_ref[pl.ds(_KNORM_B0 + qrows * pl.program_id(1), qrows), 0:1]
    _softmax_value_loop(qT_ref, k_ref, vT_ref, lambda r: knorm2[r:r + 1], lambda r: r, lambda r: r // 2,
                        w_ref, s_even_ref, s_odd_ref, m_ref, l_ref, acc_ref, qrows=qrows, tq=tq, tk=tk)

    lp = lam_ref[...]
    lam = (jnp.exp(jnp.sum(lp[0:1] * lp[1:2], axis=1, keepdims=True))
           - jnp.exp(jnp.sum(lp[2:3] * lp[3:4], axis=1, keepdims=True)) + lam_init)
    for blk in range(acc_ref.shape[0] // qrows):
        for h in range(B_HEADS_PER_STEP):
            c1 = blk * qrows + 2 * h
            o = acc_ref[c1] / _denominator(l_ref, c1) - lam * (acc_ref[c1 + 1] / _denominator(l_ref, c1 + 1))
            ms = jnp.mean(o * o, axis=0, keepdims=True)
            o = ((o * lax.rsqrt(ms + NORM_EPS)) * gsub_ref[...]) * (1.0 - lam_init)
            o_ref[h * B_V_DIM:(h + 1) * B_V_DIM, blk * tq:(blk + 1) * tq] = o.astype(_BF16)


def _attn_b(qTb, kB, vTb, knorm, lam_params, gsub, *, tq, lam_init):
    n, _, s = qTb.shape
    _, _, nck, dv, tk = vTb.shape
    hps = B_HEADS_PER_STEP
    nblk = QUERY_BLOCKS_PER_STEP
    return pl.pallas_call(
        functools.partial(_attn_b_kernel, tq=tq, tk=tk, lam_init=lam_init),
        grid=(n, B_HEADS // hps, s // (nblk * tq)),
        in_specs=[
            pl.BlockSpec((None, hps * 2 * HEAD_DIM, nblk * tq), lambda b, hp, i: (b, hp, i)),
            _once_per_row((None, s, hps * 4 * HEAD_DIM), lambda b, hp, i: (b, 0, hp)),
            _once_per_row((None, hps, nck, dv, tk), lambda b, hp, i: (b, hp, 0, 0, 0)),
            pl.BlockSpec((None, _KNORM_ROWS, 128), lambda b, hp, i: (b, 0, 0)),
            pl.BlockSpec((4, HEAD_DIM), lambda b, hp, i: (0, 0)),
            pl.BlockSpec((B_V_DIM, 1), lambda b, hp, i: (0, 0)),
        ],
        out_specs=pl.BlockSpec((None, hps * B_V_DIM, nblk * tq), lambda b, hp, i: (b, hp, i)),
        out_shape=jax.ShapeDtypeStruct((n, B_V, s), _BF16),
        scratch_shapes=_attention_scratch(nblk * 2 * hps, B_V_DIM, tq, tk),
        compiler_params=_params(("parallel", "parallel", "parallel")),
        name="diff_attention",
    )(qTb, kB, vTb, knorm, lam_params, gsub)


def _mix_and_ffn(x_ref, oTa_ref, oTb_ref, mod_ref, wout_ref, gffn_ref, wgu_ref, wdn_ref, act_ref):
    mixed = _dot_tn(oTa_ref[...], wout_ref[0:A_Q, :]) + _dot_tn(oTb_ref[...], wout_ref[A_Q:, :])
    gt1, sh2, sc2, gt2 = mod_ref[2:3, :], mod_ref[3:4, :], mod_ref[4:5, :], mod_ref[5:6, :]
    x1 = x_ref[...] + gt1 * mixed
    h = ((x1 * _rms_rows(x1)) * gffn_ref[...]) * (1.0 + sc2) + sh2
    hb = h.astype(_BF16)
    for c in range(wgu_ref.shape[0]):
        gu = _dot(hb, wgu_ref[c])
        gate, up = gu[:, :FF_CHUNK], gu[:, FF_CHUNK:]
        act_ref[:, c * FF_CHUNK:(c + 1) * FF_CHUNK] = ((gate * jax.nn.sigmoid(gate)) * up).astype(_BF16)
    return x1 + gt2 * _dot(act_ref[...], wdn_ref[...])


def _ffn_kernel(x_ref, oTa_ref, oTb_ref, mod_ref, wout_ref, gffn_ref, wgu_ref, wdn_ref, y_ref, act_ref):
    y_ref[...] = _mix_and_ffn(x_ref, oTa_ref, oTb_ref, mod_ref, wout_ref, gffn_ref, wgu_ref, wdn_ref, act_ref)


def _ffn_final_kernel(x_ref, oTa_ref, oTb_ref, mod_ref, wout_ref, gffn_ref, wgu_ref, wdn_ref, modf_ref, gfin_ref,
                      y_ref, act_ref):
    x = _mix_and_ffn(x_ref, oTa_ref, oTb_ref, mod_ref, wout_ref, gffn_ref, wgu_ref, wdn_ref, act_ref)
    shift, scale = modf_ref[0:1, :], modf_ref[1:2, :]
    y_ref[...] = ((x * _rms_rows(x)) * gfin_ref[...]) * (1.0 + scale) + shift


def _resident(shape):
    return pl.BlockSpec(shape, lambda b, t: (0,) * len(shape), pipeline_mode=pl.Buffered(1))


def _out_ffn(x, oTa, oTb, mod, w_out, gffn, w_gu, w_dn, final=None, *, ts):
    n, s, d = x.shape
    nff, _, _ = w_gu.shape
    dff = w_dn.shape[0]
    in_specs = [
        pl.BlockSpec((None, ts, d), lambda b, t: (b, t, 0)),
        pl.BlockSpec((None, A_Q, ts), lambda b, t: (b, 0, t)),
        pl.BlockSpec((None, B_V, ts), lambda b, t: (b, 0, t)),
        pl.BlockSpec((None, 6, d), lambda b, t: (b, 0, 0)),
        _resident((A_Q + B_V, d)),
        pl.BlockSpec((1, d), lambda b, t: (0, 0)),
        _resident((nff, d, 2 * FF_CHUNK)),
        _resident((dff, d)),
    ]
    operands = (x, oTa, oTb, mod, w_out, gffn, w_gu, w_dn)
    if final is not None:
        in_specs += [pl.BlockSpec((None, 2, d), lambda b, t: (b, 0, 0)), pl.BlockSpec((1, d), lambda b, t: (0, 0))]
        operands += tuple(final)
    return pl.pallas_call(
        _ffn_kernel if final is None else _ffn_final_kernel,
        grid=(n, s // ts),
        in_specs=in_specs,
        out_specs=pl.BlockSpec((None, ts, d), lambda b, t: (b, t, 0)),
        out_shape=jax.ShapeDtypeStruct((n, s, d), _F32),
        scratch_shapes=[pltpu.VMEM((ts, dff), _BF16)],
        compiler_params=_params(("parallel", "parallel")),
        name="out_proj_swiglu",
    )(*operands)


def _rope_table_t(s):
    def tables(pos, dim, theta):
        inv = 1.0 / (theta ** (jnp.arange(0, dim, 2, dtype=_F32) / dim))
        ang = pos.astype(_F32)[:, None] * inv[None, :]
        return jnp.cos(ang).T, jnp.sin(ang).T

    t = jnp.arange(s)
    cos_r, sin_r = tables(t // GRID_W, HEAD_DIM // 2, AXIAL_THETA)
    cos_c, sin_c = tables(t % GRID_W, HEAD_DIM // 2, AXIAL_THETA)
    cos_p, sin_p = tables(t, PARTIAL_ROT, ROPE_THETA)
    return jnp.concatenate([cos_r, sin_r, cos_c, sin_c, cos_p, sin_p], axis=0)


def _trunks(xs, cs, w_ada, b_ada, g_mix, w_in, g_q_a, g_k_a, lam_q1, lam_k1, lam_q2, lam_k2, g_subln,
            w_out, g_ffn, w_gate_up, w_down, w_ada_final, b_ada_final, g_final, *, ts, tq, tk):
    d = xs[0].shape[-1]
    depth = w_ada.shape[0]
    dff = w_down.shape[1]
    nff = dff // FF_CHUNK

    rows = [c.shape[0] for c in cs]
    c_pad = jnp.zeros((8, d), _F32).at[:sum(rows)].set(jnp.concatenate(cs, axis=0))
    mod_all = _modulation(c_pad, w_ada, b_ada[:, None, :])
    modf_all = _modulation(c_pad, w_ada_final[None], b_ada_final[None, None, :])[0]

    w_inT = jnp.swapaxes(w_in, 1, 2).astype(_BF16)
    w_out_b = w_out.astype(_BF16)
    w_gu = jnp.concatenate([w_gate_up[:, :, :dff].reshape(depth, d, nff, FF_CHUNK),
                            w_gate_up[:, :, dff:].reshape(depth, d, nff, FF_CHUNK)], axis=-1)
    w_gu = jnp.transpose(w_gu, (0, 2, 1, 3)).astype(_BF16)
    w_dn = w_down.astype(_BF16)
    lam_params = jnp.stack([lam_q1, lam_k1, lam_q2, lam_k2], axis=1)

    outs = []
    row0 = 0
    for x, n in zip(xs, rows):
        s = x.shape[1]
        mod = mod_all[:, row0:row0 + n].reshape(depth, n, 6, d)
        modf = modf_all[row0:row0 + n].reshape(n, 2, d)
        row0 += n
        tab = _rope_table_t(s)
        for l in range(depth):
            qTa, kA, vTa, qTb, kB, vTb, knorm = _inproj(
                x, mod[l], g_mix[l][None, :], w_inT[l], g_q_a[l][:, None], g_k_a[l][:, None], tab, ts=ts, tk=tk)
            oTa = _attn_a(qTa, kA, vTa, knorm, tq=tq)
            lam_init = 0.8 - 0.6 * math.exp(-0.3 * l)
            oTb = _attn_b(qTb, kB, vTb, knorm, lam_params[l], g_subln[l][:, None], tq=tq, lam_init=lam_init)
            final = (modf, g_final[None, :]) if l == depth - 1 else None
            x = _out_ffn(x, oTa, oTb, mod[l], w_out_b[l], g_ffn[l][None, :], w_gu[l], w_dn[l], final, ts=ts)
        outs.append(x)
    return tuple(outs)


def _trunk(x, c, *weights, ts, tq, tk):
    return _trunks((x,), (c,), *weights, ts=ts, tq=tq, tk=tk)[0]


def kernel(x_prompt, x_sample, c_prompt, c_sample, w_ada, b_ada, g_mix, w_in, g_q_a, g_k_a, lam_q1, lam_k1, lam_q2, lam_k2, g_subln, w_out, g_ffn, w_gate_up, w_down, w_ada_final, b_ada_final, g_final):
    return _trunks((x_prompt, x_sample), (c_prompt, c_sample), w_ada, b_ada, g_mix, w_in, g_q_a, g_k_a,
                   lam_q1, lam_k1, lam_q2, lam_k2, g_subln, w_out, g_ffn, w_gate_up, w_down,
                   w_ada_final, b_ada_final, g_final, ts=512, tq=256, tk=256)
```

```python
import functools
import math

import jax
import jax.numpy as jnp
from jax import lax
from jax.experimental import pallas as pl
from jax.experimental.pallas import tpu as pltpu

HEAD_DIM = 64
A_HEADS = 8
A_KV_HEADS = 2
A_GROUP = A_HEADS // A_KV_HEADS
B_HEADS = 4
B_V_DIM = 2 * HEAD_DIM
A_Q = A_HEADS * HEAD_DIM
A_KV = A_KV_HEADS * HEAD_DIM
B_QK = B_HEADS * 2 * HEAD_DIM
B_V = B_HEADS * B_V_DIM
IN_WIDTH = A_Q + 2 * A_KV + 2 * B_QK + B_V
GRID_W = 64
AXIAL_THETA = 10000.0
ROPE_THETA = 500000.0
PARTIAL_ROT = HEAD_DIM // 4
NORM_EPS = 1e-6
QK_SCALE = HEAD_DIM ** -0.5 * math.log2(math.e)

_QA0, _KA0, _VA0 = 0, A_Q, A_Q + A_KV
_QB0 = A_Q + 2 * A_KV
_KB0 = _QB0 + B_QK
_VB0 = _KB0 + B_QK

_KNORM_B0, _KNORM_A0, _KNORM_ROWS = 0, 2 * B_HEADS, 16
FAST_SOFTMAX_MAX_BOUND = 48.0
OFFSET_ROWS = 16
A_QUERY_BLOCKS_PER_STEP = 2
B_HEADS_PER_STEP = 1
B_QUERY_BLOCKS_PER_STEP = 4
FAST_STEPS_PER_TRIP = 256
FAST_LOOKAHEAD = 4

FF_CHUNK = 256
V7X_VMEM_LIMIT_BYTES = 56 * 1024 * 1024

_BF16 = jnp.bfloat16
_F32 = jnp.float32


def _params(semantics):
    return pltpu.CompilerParams(dimension_semantics=semantics, vmem_limit_bytes=V7X_VMEM_LIMIT_BYTES)


def _dot(a, b):
    return jnp.dot(a, b, preferred_element_type=_F32)


def _dot_nt(a, b):
    return lax.dot_general(a, b, (((1,), (1,)), ((), ())), preferred_element_type=_F32)


def _dot_tn(a, b):
    return lax.dot_general(a, b, (((0,), (0,)), ((), ())), preferred_element_type=_F32)


def _split_bf16(x):
    hi = x.astype(_BF16)
    lo = (x - hi.astype(_F32)).astype(_BF16)
    return hi, lo


def _mod_kernel(c_ref, w_ref, b_ref, o_ref):
    c = c_ref[...]
    c_act = c * jax.nn.sigmoid(c)
    c_hi, c_lo = _split_bf16(c_act)
    w_hi, w_lo = _split_bf16(w_ref[...])
    acc = _dot(c_hi, w_hi) + (_dot(c_lo, w_hi) + _dot(c_hi, w_lo))
    o_ref[...] = acc + b_ref[...]


def _modulation(c_pad, w, b, tn=1024):
    nl, d, n = w.shape
    return pl.pallas_call(
        _mod_kernel,
        grid=(nl, n // tn),
        in_specs=[
            pl.BlockSpec((8, d), lambda l, j: (0, 0)),
            pl.BlockSpec((None, d, tn), lambda l, j: (l, 0, j)),
            pl.BlockSpec((None, 1, tn), lambda l, j: (l, 0, j)),
        ],
        out_specs=pl.BlockSpec((None, 8, tn), lambda l, j: (l, 0, j)),
        out_shape=jax.ShapeDtypeStruct((nl, 8, n), _F32),
        compiler_params=_params(("parallel", "parallel")),
        name="adaln_modulation",
    )(c_pad, w, b)


def _rms_rows(x):
    return lax.rsqrt(jnp.mean(x * x, axis=1, keepdims=True) + NORM_EPS)


def _inproj_kernel(x_ref, mod_ref, gmix_ref, wT_ref, gq_ref, gk_ref, tab_ref,
                   qTa_ref, kA_ref, vTa_ref, qTb_ref, kB_ref, vTb_ref, knorm_ref, *, tk):
    ts = x_ref.shape[0]
    sh1 = mod_ref[0:1, :]
    sc1 = mod_ref[1:2, :]

    @pl.when(pl.program_id(1) == 0)
    def _():
        knorm_ref[...] = jnp.zeros(knorm_ref.shape, _F32)

    def head_norm(x3, g_ref):
        ms = jnp.mean(x3 * x3, axis=1, keepdims=True)
        return (x3 * lax.rsqrt(ms + NORM_EPS)) * g_ref[...][None]

    def max_key_norm2(k3):
        kf = k3.astype(_BF16).astype(_F32)
        return jnp.max(jnp.sum(kf * kf, axis=1), axis=1, keepdims=True)

    one_hot_rows = (lax.broadcasted_iota(jnp.int32, (HEAD_DIM, tk), 0) == 0).astype(_F32)

    def widened_keys_t(k3):
        parts = []
        for hh in range(k3.shape[0]):
            parts += [k3[hh], one_hot_rows]
        return jnp.concatenate(parts, axis=0).T.astype(_BF16)

    for c in range(ts // tk):
        tok = slice(c * tk, (c + 1) * tk)
        x = x_ref[tok, :]
        h = ((x * _rms_rows(x)) * gmix_ref[...]) * (1.0 + sc1) + sh1
        hb = h.astype(_BF16)

        def proj_t(lo, n):
            return _dot_nt(wT_ref[lo:lo + n, :], hb)

        tab = tab_ref[:, tok]
        cos_r, sin_r, cos_c, sin_c = tab[0:16], tab[16:32], tab[32:48], tab[48:64]
        cos_p, sin_p = tab[64:72], tab[72:80]

        def axial(x3):
            r1, r2, c1, c2 = x3[:, 0:16], x3[:, 16:32], x3[:, 32:48], x3[:, 48:64]
            return jnp.concatenate([r1 * cos_r - r2 * sin_r, r1 * sin_r + r2 * cos_r,
                                    c1 * cos_c - c2 * sin_c, c1 * sin_c + c2 * cos_c], axis=1)

        def partial(x3):
            half = PARTIAL_ROT // 2
            x1, x2, rest = x3[:, 0:half], x3[:, half:PARTIAL_ROT], x3[:, PARTIAL_ROT:]
            return jnp.concatenate([x1 * cos_p - x2 * sin_p, x1 * sin_p + x2 * cos_p, rest], axis=1)

        qa = proj_t(_QA0, A_Q).reshape(A_HEADS, HEAD_DIM, tk)
        qa = axial(head_norm(qa, gq_ref)) * QK_SCALE
        qTa_ref[:, tok] = qa.reshape(A_Q, tk).astype(_BF16)

        ka = axial(head_norm(proj_t(_KA0, A_KV).reshape(A_KV_HEADS, HEAD_DIM, tk), gk_ref))
        kA_ref[tok, :] = widened_keys_t(ka)

        va = proj_t(_VA0, A_KV).astype(_BF16)
        for g in range(A_KV_HEADS):
            vTa_ref[g, c] = va[g * HEAD_DIM:(g + 1) * HEAD_DIM]

        qb = proj_t(_QB0, B_QK).reshape(2 * B_HEADS, HEAD_DIM, tk)
        qTb_ref[:, tok] = (partial(qb) * QK_SCALE).reshape(B_QK, tk).astype(_BF16)

        kb = partial(proj_t(_KB0, B_QK).reshape(2 * B_HEADS, HEAD_DIM, tk))
        kB_ref[tok, :] = widened_keys_t(kb)

        vb = proj_t(_VB0, B_V).astype(_BF16)
        for hd in range(B_HEADS):
            vTb_ref[hd, c] = vb[hd * B_V_DIM:(hd + 1) * B_V_DIM]

        lanes = knorm_ref.shape[1]
        b_rows = slice(_KNORM_B0, _KNORM_B0 + 2 * B_HEADS)
        a_rows = slice(_KNORM_A0, _KNORM_A0 + A_KV_HEADS)
        knorm_ref[b_rows, :] = jnp.maximum(knorm_ref[b_rows, :],
                                           jnp.broadcast_to(max_key_norm2(kb), (2 * B_HEADS, lanes)))
        knorm_ref[a_rows, :] = jnp.maximum(knorm_ref[a_rows, :],
                                           jnp.broadcast_to(max_key_norm2(ka), (A_KV_HEADS, lanes)))


def _inproj(x, mod, gmix, w_inT, gq, gk, tab, *, ts, tk):
    n, s, d = x.shape
    nck = s // tk
    cpt = ts // tk
    out_shape = (
        jax.ShapeDtypeStruct((n, A_Q, s), _BF16),
        jax.ShapeDtypeStruct((n, s, 2 * A_KV), _BF16),
        jax.ShapeDtypeStruct((n, A_KV_HEADS, nck, HEAD_DIM, tk), _BF16),
        jax.ShapeDtypeStruct((n, B_QK, s), _BF16),
        jax.ShapeDtypeStruct((n, s, 2 * B_QK), _BF16),
        jax.ShapeDtypeStruct((n, B_HEADS, nck, B_V_DIM, tk), _BF16),
        jax.ShapeDtypeStruct((n, _KNORM_ROWS, 128), _F32),
    )
    return pl.pallas_call(
        functools.partial(_inproj_kernel, tk=tk),
        grid=(n, s // ts),
        in_specs=[
            pl.BlockSpec((None, ts, d), lambda b, t: (b, t, 0)),
            pl.BlockSpec((None, 6, d), lambda b, t: (b, 0, 0)),
            pl.BlockSpec((1, d), lambda b, t: (0, 0)),
            pl.BlockSpec((IN_WIDTH, d), lambda b, t: (0, 0)),
            pl.BlockSpec((HEAD_DIM, 1), lambda b, t: (0, 0)),
            pl.BlockSpec((HEAD_DIM, 1), lambda b, t: (0, 0)),
            pl.BlockSpec((80, ts), lambda b, t: (0, t)),
        ],
        out_specs=(
            pl.BlockSpec((None, A_Q, ts), lambda b, t: (b, 0, t)),
            pl.BlockSpec((None, ts, 2 * A_KV), lambda b, t: (b, t, 0)),
            pl.BlockSpec((None, A_KV_HEADS, cpt, HEAD_DIM, tk), lambda b, t: (b, 0, t, 0, 0)),
            pl.BlockSpec((None, B_QK, ts), lambda b, t: (b, 0, t)),
            pl.BlockSpec((None, ts, 2 * B_QK), lambda b, t: (b, t, 0)),
            pl.BlockSpec((None, B_HEADS, cpt, B_V_DIM, tk), lambda b, t: (b, 0, t, 0, 0)),
            pl.BlockSpec((None, _KNORM_ROWS, 128), lambda b, t: (b, 0, 0)),
        ),
        out_shape=out_shape,
        compiler_params=_params(("parallel", "arbitrary")),
        name="mix_in_projection",
    )(x, mod, gmix, w_inT, gq, gk, tab)


def _softmax_value_loop(qT_ref, k_ref, vT_ref, knorm2_of_row, key_head_of_row, value_head_of_row, w_ref,
                        s_even_ref, s_odd_ref, m_ref, l_ref, acc_ref, *, qrows, tq, tk):
    nck = vT_ref.shape[1]
    nchains = qrows * (qT_ref.shape[-1] // tq)
    knorm2_of_chain = lambda c: knorm2_of_row(c % qrows)
    key_head_of_chain = lambda c: key_head_of_row(c % qrows)
    value_head_of_chain = lambda c: value_head_of_row(c % qrows)
    acc_ref[...] = jnp.zeros(acc_ref.shape, _F32)
    l_ref[...] = jnp.zeros(l_ref.shape, _F32)

    qs = [qT_ref[(c % qrows) * HEAD_DIM:(c % qrows + 1) * HEAD_DIM, (c // qrows) * tq:(c // qrows + 1) * tq]
          for c in range(nchains)]
    bounds = []
    for c in range(nchains):
        qf = qs[c].astype(_F32)
        bounds.append(jnp.sqrt(jnp.sum(qf * qf, axis=0, keepdims=True) * knorm2_of_chain(c)))
    use_bound = jnp.max(jnp.concatenate(bounds, axis=0)) <= FAST_SOFTMAX_MAX_BOUND

    first_row = lax.broadcasted_iota(jnp.int32, (OFFSET_ROWS, tq), 0) == 0
    for c in range(nchains):
        offset = -jnp.minimum(bounds[c], FAST_SOFTMAX_MAX_BOUND)
        offset_rows = jnp.where(first_row, jnp.broadcast_to(offset, (OFFSET_ROWS, tq)), 0.0)
        w_ref[c] = jnp.concatenate([qs[c], offset_rows.astype(_BF16)], axis=0)

    def k_chunk(j, c):
        lane0 = key_head_of_chain(c) * 2 * HEAD_DIM
        return k_ref[pl.ds(pl.multiple_of(j * tk, tk), tk), lane0:lane0 + HEAD_DIM + OFFSET_ROWS]

    def partial_sums(p):
        return jnp.sum(p.reshape(tk // 8, 8, tq), axis=0)

    assert nchains >= FAST_LOOKAHEAD
    for a in range(FAST_LOOKAHEAD):
        s_even_ref[a] = _dot(k_chunk(0, a), w_ref[a])

    @pl.when(use_bound)
    def _():
        def scores(j, step):
            dj, c = divmod(step, nchains)
            jj = j + dj
            if dj and not isinstance(jj, int):
                jj = jnp.where(jj >= nck, jj - nck, jj)
            return _dot(k_chunk(jj, c), w_ref[c])

        chunks_per_trip = min(FAST_STEPS_PER_TRIP // nchains, nck)
        assert nck % chunks_per_trip == 0

        single_trip = chunks_per_trip == nck

        def body(i, carry):
            j0 = i * chunks_per_trip
            pending = [s_even_ref[a] for a in range(FAST_LOOKAHEAD)]
            for u in range(chunks_per_trip):
                for c in range(nchains):
                    ahead = u * nchains + c + FAST_LOOKAHEAD
                    if not (single_trip and ahead >= nck * nchains):
                        pending.append(scores(j0, ahead))
                    p = jnp.exp2(pending.pop(0))
                    l_ref[c] += partial_sums(p)
                    acc_ref[c] += _dot(vT_ref[value_head_of_chain(c), j0 + u], p.astype(_BF16))
            for a, tile in enumerate(pending):
                s_even_ref[a] = tile
            return carry

        if single_trip:
            body(0, 0)
        else:
            lax.fori_loop(0, nck // chunks_per_trip, body, 0)

    @pl.when(jnp.logical_not(use_bound))
    def _():
        m_ref[...] = jnp.full(m_ref.shape, -jnp.inf, _F32)

        def phase(j_cur, j_next, s_cur_ref, s_next_ref):
            for c in range(nchains):
                vc = vT_ref[value_head_of_chain(c), j_cur]
                s = s_cur_ref[c]
                s_next_ref[c] = _dot(k_chunk(j_next, c), w_ref[c])
                m_prev = m_ref[c]
                m_new = jnp.maximum(m_prev, jnp.max(s, axis=0, keepdims=True))
                alpha = jnp.exp2(m_prev - m_new)
                p = jnp.exp2(s - m_new)
                l_ref[c] = alpha * l_ref[c] + partial_sums(p)
                acc_ref[c] = alpha * acc_ref[c] + _dot(vc, p.astype(_BF16))
                m_ref[c] = m_new

        for c in range(FAST_LOOKAHEAD, nchains):
            s_even_ref[c] = _dot(k_chunk(0, c), w_ref[c])

        def body(i, carry):
            j = 2 * i
            phase(j, j + 1, s_even_ref, s_odd_ref)
            phase(j + 1, jnp.where(j + 2 == nck, 0, j + 2), s_odd_ref, s_even_ref)
            return carry

        lax.fori_loop(0, nck // 2, body, 0)


def _denominator(l_ref, c):
    return jnp.sum(l_ref[c], axis=0, keepdims=True)


def _attn_a_kernel(qT_ref, k_ref, vT_ref, knorm_ref, o_ref, w_ref, s_even_ref, s_odd_ref, m_ref, l_ref, acc_ref,
                   *, tq, tk):
    knorm2 = knorm_ref[_KNORM_A0:_KNORM_A0 + A_KV_HEADS, 0:1]
    kv_head = lambda r: r // A_GROUP
    _softmax_value_loop(qT_ref, k_ref, vT_ref, lambda r: knorm2[kv_head(r):kv_head(r) + 1], kv_head, kv_head,
                        w_ref, s_even_ref, s_odd_ref, m_ref, l_ref, acc_ref, qrows=A_HEADS, tq=tq, tk=tk)
    for c in range(acc_ref.shape[0]):
        blk, r = divmod(c, A_HEADS)
        o = acc_ref[c] / _denominator(l_ref, c)
        o_ref[r * HEAD_DIM:(r + 1) * HEAD_DIM, blk * tq:(blk + 1) * tq] = o.astype(_BF16)


def _attention_scratch(nchains, dv, tq, tk):
    return [
        pltpu.VMEM((nchains, HEAD_DIM + OFFSET_ROWS, tq), _BF16),
        pltpu.VMEM((max(nchains, FAST_LOOKAHEAD), tk, tq), _F32),
        pltpu.VMEM((nchains, tk, tq), _F32),
        pltpu.VMEM((nchains, 1, tq), _F32),
        pltpu.VMEM((nchains, 8, tq), _F32),
        pltpu.VMEM((nchains, dv, tq), _F32),
    ]


def _attn_a(qTa, kA, vTa, knorm, *, tq):
    n, _, s = qTa.shape
    _, nkv, nck, dv, tk = vTa.shape
    nblk = A_QUERY_BLOCKS_PER_STEP
    return pl.pallas_call(
        functools.partial(_attn_a_kernel, tq=tq, tk=tk),
        grid=(n, s // (nblk * tq)),
        in_specs=[
            pl.BlockSpec((None, A_Q, nblk * tq), lambda b, i: (b, 0, i)),
            pl.BlockSpec((None, s, nkv * 2 * HEAD_DIM), lambda b, i: (b, 0, 0)),
            pl.BlockSpec((None, nkv, nck, dv, tk), lambda b, i: (b, 0, 0, 0, 0)),
            pl.BlockSpec((None, _KNORM_ROWS, 128), lambda b, i: (b, 0, 0)),
        ],
        out_specs=pl.BlockSpec((None, A_Q, nblk * tq), lambda b, i: (b, 0, i)),
        out_shape=jax.ShapeDtypeStruct((n, A_Q, s), _BF16),
        scratch_shapes=_attention_scratch(nblk * A_HEADS, HEAD_DIM, tq, tk),
        compiler_params=_params(("parallel", "parallel")),
        name="gqa_attention",
    )(qTa, kA, vTa, knorm)


def _attn_b_kernel(qT_ref, k_ref, vT_ref, knorm_ref, lam_ref, gsub_ref, o_ref, w_ref, s_even_ref, s_odd_ref,
                   m_ref, l_ref, acc_ref, *, tq, tk, lam_init):
    qrows = 2 * B_HEADS_PER_STEP
    knorm2 = knorm_ref[pl.ds(_KNORM_B0 + qrows * pl.program_id(1), qrows), 0:1]
    _softmax_value_loop(qT_ref, k_ref, vT_ref, lambda r: knorm2[r:r + 1], lambda r: r, lambda r: r // 2,
                        w_ref, s_even_ref, s_odd_ref, m_ref, l_ref, acc_ref, qrows=qrows, tq=tq, tk=tk)

    lp = lam_ref[...]
    lam = (jnp.exp(jnp.sum(lp[0:1] * lp[1:2], axis=1, keepdims=True))
           - jnp.exp(jnp.sum(lp[2:3] * lp[3:4], axis=1, keepdims=True)) + lam_init)
    for blk in range(acc_ref.shape[0] // qrows):
        for h in range(B_HEADS_PER_STEP):
            c1 = blk * qrows + 2 * h
            o = acc_ref[c1] / _denominator(l_ref, c1) - lam * (acc_ref[c1 + 1] / _denominator(l_ref, c1 + 1))
            ms = jnp.mean(o * o, axis=0, keepdims=True)
            o = ((o * lax.rsqrt(ms + NORM_EPS)) * gsub_ref[...]) * (1.0 - lam_init)
            o_ref[h * B_V_DIM:(h + 1) * B_V_DIM, blk * tq:(blk + 1) * tq] = o.astype(_BF16)


def _attn_b(qTb, kB, vTb, knorm, lam_params, gsub, *, tq, lam_init):
    n, _, s = qTb.shape
    _, _, nck, dv, tk = vTb.shape
    hps = B_HEADS_PER_STEP
    nblk = B_QUERY_BLOCKS_PER_STEP
    return pl.pallas_call(
        functools.partial(_attn_b_kernel, tq=tq, tk=tk, lam_init=lam_init),
        grid=(n, B_HEADS // hps, s // (nblk * tq)),
        in_specs=[
            pl.BlockSpec((None, hps * 2 * HEAD_DIM, nblk * tq), lambda b, hp, i: (b, hp, i)),
            pl.BlockSpec((None, s, hps * 4 * HEAD_DIM), lambda b, hp, i: (b, 0, hp)),
            pl.BlockSpec((None, hps, nck, dv, tk), lambda b, hp, i: (b, hp, 0, 0, 0)),
            pl.BlockSpec((None, _KNORM_ROWS, 128), lambda b, hp, i: (b, 0, 0)),
            pl.BlockSpec((4, HEAD_DIM), lambda b, hp, i: (0, 0)),
            pl.BlockSpec((B_V_DIM, 1), lambda b, hp, i: (0, 0)),
        ],
        out_specs=pl.BlockSpec((None, hps * B_V_DIM, nblk * tq), lambda b, hp, i: (b, hp, i)),
        out_shape=jax.ShapeDtypeStruct((n, B_V, s), _BF16),
        scratch_shapes=_attention_scratch(nblk * 2 * hps, B_V_DIM, tq, tk),
        compiler_params=_params(("parallel", "parallel", "parallel")),
        name="diff_attention",
    )(qTb, kB, vTb, knorm, lam_params, gsub)


def _mix_and_ffn(x_ref, oTa_ref, oTb_ref, mod_ref, wout_ref, gffn_ref, wgu_ref, wdn_ref, act_ref):
    mixed = _dot_tn(oTa_ref[...], wout_ref[0:A_Q, :]) + _dot_tn(oTb_ref[...], wout_ref[A_Q:, :])
    gt1, sh2, sc2, gt2 = mod_ref[2:3, :], mod_ref[3:4, :], mod_ref[4:5, :], mod_ref[5:6, :]
    x1 = x_ref[...] + gt1 * mixed
    h = ((x1 * _rms_rows(x1)) * gffn_ref[...]) * (1.0 + sc2) + sh2
    hb = h.astype(_BF16)
    dff = wdn_ref.shape[0]
    for c in range(dff // FF_CHUNK):
        cols = slice(c * FF_CHUNK, (c + 1) * FF_CHUNK)
        gate = _dot(hb, wgu_ref[:, cols])
        up = _dot(hb, wgu_ref[:, dff + c * FF_CHUNK:dff + (c + 1) * FF_CHUNK])
        act_ref[:, cols] = ((gate * jax.nn.sigmoid(gate)) * up).astype(_BF16)
    return x1 + gt2 * _dot(act_ref[...], wdn_ref[...])


def _ffn_kernel(x_ref, oTa_ref, oTb_ref, mod_ref, wout_ref, gffn_ref, wgu_ref, wdn_ref, y_ref, act_ref):
    y_ref[...] = _mix_and_ffn(x_ref, oTa_ref, oTb_ref, mod_ref, wout_ref, gffn_ref, wgu_ref, wdn_ref, act_ref)


def _ffn_final_kernel(x_ref, oTa_ref, oTb_ref, mod_ref, wout_ref, gffn_ref, wgu_ref, wdn_ref, modf_ref, gfin_ref,
                      y_ref, act_ref):
    x = _mix_and_ffn(x_ref, oTa_ref, oTb_ref, mod_ref, wout_ref, gffn_ref, wgu_ref, wdn_ref, act_ref)
    shift, scale = modf_ref[0:1, :], modf_ref[1:2, :]
    y_ref[...] = ((x * _rms_rows(x)) * gfin_ref[...]) * (1.0 + scale) + shift


def _resident(shape):
    return pl.BlockSpec(shape, lambda b, t: (0,) * len(shape), pipeline_mode=pl.Buffered(1))


def _out_ffn(x, oTa, oTb, mod, w_out, gffn, w_gu, w_dn, final=None, *, ts):
    n, s, d = x.shape
    dff = w_dn.shape[0]
    assert dff % FF_CHUNK == 0 and w_gu.shape == (d, 2 * dff)
    in_specs = [
        pl.BlockSpec((None, ts, d), lambda b, t: (b, t, 0)),
        pl.BlockSpec((None, A_Q, ts), lambda b, t: (b, 0, t)),
        pl.BlockSpec((None, B_V, ts), lambda b, t: (b, 0, t)),
        pl.BlockSpec((None, 6, d), lambda b, t: (b, 0, 0)),
        _resident((A_Q + B_V, d)),
        pl.BlockSpec((1, d), lambda b, t: (0, 0)),
        _resident((d, 2 * dff)),
        _resident((dff, d)),
    ]
    operands = (x, oTa, oTb, mod, w_out, gffn, w_gu, w_dn)
    if final is not None:
        in_specs += [pl.BlockSpec((None, 2, d), lambda b, t: (b, 0, 0)), pl.BlockSpec((1, d), lambda b, t: (0, 0))]
        operands += tuple(final)
    return pl.pallas_call(
        _ffn_kernel if final is None else _ffn_final_kernel,
        grid=(n, s // ts),
        in_specs=in_specs,
        out_specs=pl.BlockSpec((None, ts, d), lambda b, t: (b, t, 0)),
        out_shape=jax.ShapeDtypeStruct((n, s, d), _F32),
        scratch_shapes=[pltpu.VMEM((ts, dff), _BF16)],
        compiler_params=_params(("parallel", "parallel")),
        name="out_proj_swiglu",
    )(*operands)


def _rope_table_t(s):
    def tables(pos, dim, theta):
        inv = 1.0 / (theta ** (jnp.arange(0, dim, 2, dtype=_F32) / dim))
        ang = pos.astype(_F32)[:, None] * inv[None, :]
        return jnp.cos(ang).T, jnp.sin(ang).T

    t = jnp.arange(s)
    cos_r, sin_r = tables(t // GRID_W, HEAD_DIM // 2, AXIAL_THETA)
    cos_c, sin_c = tables(t % GRID_W, HEAD_DIM // 2, AXIAL_THETA)
    cos_p, sin_p = tables(t, PARTIAL_ROT, ROPE_THETA)
    return jnp.concatenate([cos_r, sin_r, cos_c, sin_c, cos_p, sin_p], axis=0)


def _trunks(xs, cs, w_ada, b_ada, g_mix, w_in, g_q_a, g_k_a, lam_q1, lam_k1, lam_q2, lam_k2, g_subln,
            w_out, g_ffn, w_gate_up, w_down, w_ada_final, b_ada_final, g_final, *, ts, tq, tk):
    d = xs[0].shape[-1]
    depth = w_ada.shape[0]

    rows = [c.shape[0] for c in cs]
    c_pad = jnp.zeros((8, d), _F32).at[:sum(rows)].set(jnp.concatenate(cs, axis=0))
    mod_all = _modulation(c_pad, w_ada, b_ada[:, None, :])
    modf_all = _modulation(c_pad, w_ada_final[None], b_ada_final[None, None, :])[0]

    w_inT = jnp.swapaxes(w_in, 1, 2).astype(_BF16)
    w_out_b = w_out.astype(_BF16)
    w_gu = w_gate_up.astype(_BF16)
    w_dn = w_down.astype(_BF16)
    lam_params = jnp.stack([lam_q1, lam_k1, lam_q2, lam_k2], axis=1)

    outs = []
    row0 = 0
    for x, n in zip(xs, rows):
        s = x.shape[1]
        mod = mod_all[:, row0:row0 + n].reshape(depth, n, 6, d)
        modf = modf_all[row0:row0 + n].reshape(n, 2, d)
        row0 += n
        tab = _rope_table_t(s)
        for l in range(depth):
            qTa, kA, vTa, qTb, kB, vTb, knorm = _inproj(
                x, mod[l], g_mix[l][None, :], w_inT[l], g_q_a[l][:, None], g_k_a[l][:, None], tab, ts=ts, tk=tk)
            oTa = _attn_a(qTa, kA, vTa, knorm, tq=tq)
            lam_init = 0.8 - 0.6 * math.exp(-0.3 * l)
            oTb = _attn_b(qTb, kB, vTb, knorm, lam_params[l], g_subln[l][:, None], tq=tq, lam_init=lam_init)
            final = (modf, g_final[None, :]) if l == depth - 1 else None
            x = _out_ffn(x, oTa, oTb, mod[l], w_out_b[l], g_ffn[l][None, :], w_gu[l], w_dn[l], final, ts=ts)
        outs.append(x)
    return tuple(outs)


def _trunk(x, c, *weights, ts, tq, tk):
    return _trunks((x,), (c,), *weights, ts=ts, tq=tq, tk=tk)[0]


def kernel(x_prompt, x_sample, c_prompt, c_sample, w_ada, b_ada, g_mix, w_in, g_q_a, g_k_a, lam_q1, lam_k1, lam_q2, lam_k2, g_subln, w_out, g_ffn, w_gate_up, w_down, w_ada_final, b_ada_final, g_final):
    return _trunks((x_prompt, x_sample), (c_prompt, c_sample), w_ada, b_ada, g_mix, w_in, g_q_a, g_k_a,
                   lam_q1, lam_k1, lam_q2, lam_k2, g_subln, w_out, g_ffn, w_gate_up, w_down,
                   w_ada_final, b_ada_final, g_final, ts=512, tq=256, tk=256)
```

```python
import functools
import math

import jax
import jax.numpy as jnp
from jax import lax
from jax.experimental import pallas as pl
from jax.experimental.pallas import tpu as pltpu

HEAD_DIM = 64
A_HEADS = 8
A_KV_HEADS = 2
A_GROUP = A_HEADS // A_KV_HEADS
B_HEADS = 4
B_V_DIM = 2 * HEAD_DIM
A_Q = A_HEADS * HEAD_DIM
A_KV = A_KV_HEADS * HEAD_DIM
B_QK = B_HEADS * 2 * HEAD_DIM
B_V = B_HEADS * B_V_DIM
IN_WIDTH = A_Q + 2 * A_KV + 2 * B_QK + B_V
GRID_W = 64
AXIAL_THETA = 10000.0
ROPE_THETA = 500000.0
PARTIAL_ROT = HEAD_DIM // 4
NORM_EPS = 1e-6
QK_SCALE = HEAD_DIM ** -0.5 * math.log2(math.e)

_QA0, _KA0, _VA0 = 0, A_Q, A_Q + A_KV
_QB0 = A_Q + 2 * A_KV
_KB0 = _QB0 + B_QK
_VB0 = _KB0 + B_QK

_KNORM_B0, _KNORM_A0, _KNORM_ROWS = 0, 2 * B_HEADS, 16
FAST_SOFTMAX_MAX_BOUND = 48.0
OFFSET_ROWS = 16
A_QUERY_BLOCKS_PER_STEP = 2
B_HEADS_PER_STEP = 1
B_QUERY_BLOCKS_PER_STEP = 4
FAST_STEPS_PER_TRIP = 256
FAST_LOOKAHEAD = 4

FF_CHUNK = 256
V7X_VMEM_LIMIT_BYTES = 56 * 1024 * 1024

_BF16 = jnp.bfloat16
_F32 = jnp.float32


def _params(semantics):
    return pltpu.CompilerParams(dimension_semantics=semantics, vmem_limit_bytes=V7X_VMEM_LIMIT_BYTES)


def _dot(a, b):
    return jnp.dot(a, b, preferred_element_type=_F32)


def _dot_nt(a, b):
    return lax.dot_general(a, b, (((1,), (1,)), ((), ())), preferred_element_type=_F32)


def _dot_tn(a, b):
    return lax.dot_general(a, b, (((0,), (0,)), ((), ())), preferred_element_type=_F32)


def _split_bf16(x):
    hi = x.astype(_BF16)
    lo = (x - hi.astype(_F32)).astype(_BF16)
    return hi, lo


def _mod_kernel(c_ref, w_ref, b_ref, o_ref):
    c = c_ref[...]
    c_act = c * jax.nn.sigmoid(c)
    c_hi, c_lo = _split_bf16(c_act)
    w_hi, w_lo = _split_bf16(w_ref[...])
    acc = _dot(c_hi, w_hi) + (_dot(c_lo, w_hi) + _dot(c_hi, w_lo))
    o_ref[...] = acc + b_ref[...]


def _modulation(c_pad, w, b, tn=1024):
    nl, d, n = w.shape
    return pl.pallas_call(
        _mod_kernel,
        grid=(nl, n // tn),
        in_specs=[
            pl.BlockSpec((8, d), lambda l, j: (0, 0)),
            pl.BlockSpec((None, d, tn), lambda l, j: (l, 0, j)),
            pl.BlockSpec((None, 1, tn), lambda l, j: (l, 0, j)),
        ],
        out_specs=pl.BlockSpec((None, 8, tn), lambda l, j: (l, 0, j)),
        out_shape=jax.ShapeDtypeStruct((nl, 8, n), _F32),
        compiler_params=_params(("parallel", "parallel")),
        name="adaln_modulation",
    )(c_pad, w, b)


def _rms_rows(x):
    return lax.rsqrt(jnp.mean(x * x, axis=1, keepdims=True) + NORM_EPS)


def _inproj_kernel(x_ref, mod_ref, gmix_ref, wT_ref, gq_ref, gk_ref, tab_ref,
                   qTa_ref, kA_ref, vTa_ref, qTb_ref, kB_ref, vTb_ref, knorm_ref, *, tk):
    ts = x_ref.shape[0]
    sh1 = mod_ref[0:1, :]
    sc1 = mod_ref[1:2, :]

    @pl.when(pl.program_id(1) == 0)
    def _():
        knorm_ref[...] = jnp.zeros(knorm_ref.shape, _F32)

    def head_norm(x3, g_ref):
        ms = jnp.mean(x3 * x3, axis=1, keepdims=True)
        return (x3 * lax.rsqrt(ms + NORM_EPS)) * g_ref[...][None]

    def max_key_norm2(k3):
        kf = k3.astype(_BF16).astype(_F32)
        return jnp.max(jnp.sum(kf * kf, axis=1), axis=1, keepdims=True)

    one_hot_rows = (lax.broadcasted_iota(jnp.int32, (HEAD_DIM, tk), 0) == 0).astype(_F32)

    def widened_keys_t(k3):
        parts = []
        for hh in range(k3.shape[0]):
            parts += [k3[hh], one_hot_rows]
        return jnp.concatenate(parts, axis=0).T.astype(_BF16)

    for c in range(ts // tk):
        tok = slice(c * tk, (c + 1) * tk)
        x = x_ref[tok, :]
        h = ((x * _rms_rows(x)) * gmix_ref[...]) * (1.0 + sc1) + sh1
        hbT = h.T.astype(_BF16)

        def proj_t(lo, n):
            return _dot(wT_ref[lo:lo + n, :], hbT)

        tab = tab_ref[:, tok]
        cos_r, sin_r, cos_c, sin_c = tab[0:16], tab[16:32], tab[32:48], tab[48:64]
        cos_p, sin_p = tab[64:72], tab[72:80]

        def axial(x3):
            r1, r2, c1, c2 = x3[:, 0:16], x3[:, 16:32], x3[:, 32:48], x3[:, 48:64]
            return jnp.concatenate([r1 * cos_r - r2 * sin_r, r1 * sin_r + r2 * cos_r,
                                    c1 * cos_c - c2 * sin_c, c1 * sin_c + c2 * cos_c], axis=1)

        def partial(x3):
            half = PARTIAL_ROT // 2
            x1, x2, rest = x3[:, 0:half], x3[:, half:PARTIAL_ROT], x3[:, PARTIAL_ROT:]
            return jnp.concatenate([x1 * cos_p - x2 * sin_p, x1 * sin_p + x2 * cos_p, rest], axis=1)

        qa = proj_t(_QA0, A_Q).reshape(A_HEADS, HEAD_DIM, tk)
        qa = axial(head_norm(qa, gq_ref)) * QK_SCALE
        qTa_ref[:, tok] = qa.reshape(A_Q, tk).astype(_BF16)

        ka = axial(head_norm(proj_t(_KA0, A_KV).reshape(A_KV_HEADS, HEAD_DIM, tk), gk_ref))
        kA_ref[tok, :] = widened_keys_t(ka)

        va = proj_t(_VA0, A_KV).astype(_BF16)
        for g in range(A_KV_HEADS):
            vTa_ref[g, c] = va[g * HEAD_DIM:(g + 1) * HEAD_DIM]

        qb = proj_t(_QB0, B_QK).reshape(2 * B_HEADS, HEAD_DIM, tk)
        qTb_ref[:, tok] = (partial(qb) * QK_SCALE).reshape(B_QK, tk).astype(_BF16)

        kb = partial(proj_t(_KB0, B_QK).reshape(2 * B_HEADS, HEAD_DIM, tk))
        kB_ref[tok, :] = widened_keys_t(kb)

        vb = proj_t(_VB0, B_V).astype(_BF16)
        for hd in range(B_HEADS):
            vTb_ref[hd, c] = vb[hd * B_V_DIM:(hd + 1) * B_V_DIM]

        lanes = knorm_ref.shape[1]
        b_rows = slice(_KNORM_B0, _KNORM_B0 + 2 * B_HEADS)
        a_rows = slice(_KNORM_A0, _KNORM_A0 + A_KV_HEADS)
        knorm_ref[b_rows, :] = jnp.maximum(knorm_ref[b_rows, :],
                                           jnp.broadcast_to(max_key_norm2(kb), (2 * B_HEADS, lanes)))
        knorm_ref[a_rows, :] = jnp.maximum(knorm_ref[a_rows, :],
                                           jnp.broadcast_to(max_key_norm2(ka), (A_KV_HEADS, lanes)))


def _inproj(x, mod, gmix, w_inT, gq, gk, tab, *, ts, tk):
    n, s, d = x.shape
    nck = s // tk
    cpt = ts // tk
    out_shape = (
        jax.ShapeDtypeStruct((n, A_Q, s), _BF16),
        jax.ShapeDtypeStruct((n, s, 2 * A_KV), _BF16),
        jax.ShapeDtypeStruct((n, A_KV_HEADS, nck, HEAD_DIM, tk), _BF16),
        jax.ShapeDtypeStruct((n, B_QK, s), _BF16),
        jax.ShapeDtypeStruct((n, s, 2 * B_QK), _BF16),
        jax.ShapeDtypeStruct((n, B_HEADS, nck, B_V_DIM, tk), _BF16),
        jax.ShapeDtypeStruct((n, _KNORM_ROWS, 128), _F32),
    )
    return pl.pallas_call(
        functools.partial(_inproj_kernel, tk=tk),
        grid=(n, s // ts),
        in_specs=[
            pl.BlockSpec((None, ts, d), lambda b, t: (b, t, 0)),
            pl.BlockSpec((None, 6, d), lambda b, t: (b, 0, 0)),
            pl.BlockSpec((1, d), lambda b, t: (0, 0)),
            pl.BlockSpec((IN_WIDTH, d), lambda b, t: (0, 0)),
            pl.BlockSpec((HEAD_DIM, 1), lambda b, t: (0, 0)),
            pl.BlockSpec((HEAD_DIM, 1), lambda b, t: (0, 0)),
            pl.BlockSpec((80, ts), lambda b, t: (0, t)),
        ],
        out_specs=(
            pl.BlockSpec((None, A_Q, ts), lambda b, t: (b, 0, t)),
            pl.BlockSpec((None, ts, 2 * A_KV), lambda b, t: (b, t, 0)),
            pl.BlockSpec((None, A_KV_HEADS, cpt, HEAD_DIM, tk), lambda b, t: (b, 0, t, 0, 0)),
            pl.BlockSpec((None, B_QK, ts), lambda b, t: (b, 0, t)),
            pl.BlockSpec((None, ts, 2 * B_QK), lambda b, t: (b, t, 0)),
            pl.BlockSpec((None, B_HEADS, cpt, B_V_DIM, tk), lambda b, t: (b, 0, t, 0, 0)),
            pl.BlockSpec((None, _KNORM_ROWS, 128), lambda b, t: (b, 0, 0)),
        ),
        out_shape=out_shape,
        compiler_params=_params(("parallel", "arbitrary")),
        name="mix_in_projection",
    )(x, mod, gmix, w_inT, gq, gk, tab)


def _softmax_value_loop(qT_ref, k_ref, vT_ref, knorm2_of_row, key_head_of_row, value_head_of_row, w_ref,
                        s_even_ref, s_odd_ref, m_ref, l_ref, acc_ref, *, qrows, tq, tk):
    nck = vT_ref.shape[1]
    nchains = qrows * (qT_ref.shape[-1] // tq)
    knorm2_of_chain = lambda c: knorm2_of_row(c % qrows)
    key_head_of_chain = lambda c: key_head_of_row(c % qrows)
    value_head_of_chain = lambda c: value_head_of_row(c % qrows)
    acc_ref[...] = jnp.zeros(acc_ref.shape, _F32)
    l_ref[...] = jnp.zeros(l_ref.shape, _F32)

    qs = [qT_ref[(c % qrows) * HEAD_DIM:(c % qrows + 1) * HEAD_DIM, (c // qrows) * tq:(c // qrows + 1) * tq]
          for c in range(nchains)]
    bounds = []
    for c in range(nchains):
        qf = qs[c].astype(_F32)
        bounds.append(jnp.sqrt(jnp.sum(qf * qf, axis=0, keepdims=True) * knorm2_of_chain(c)))
    use_bound = jnp.max(jnp.concatenate(bounds, axis=0)) <= FAST_SOFTMAX_MAX_BOUND

    first_row = lax.broadcasted_iota(jnp.int32, (OFFSET_ROWS, tq), 0) == 0
    for c in range(nchains):
        offset = -jnp.minimum(bounds[c], FAST_SOFTMAX_MAX_BOUND)
        offset_rows = jnp.where(first_row, jnp.broadcast_to(offset, (OFFSET_ROWS, tq)), 0.0)
        w_ref[c] = jnp.concatenate([qs[c], offset_rows.astype(_BF16)], axis=0)

    def k_chunk(j, c):
        lane0 = key_head_of_chain(c) * 2 * HEAD_DIM
        return k_ref[pl.ds(pl.multiple_of(j * tk, tk), tk), lane0:lane0 + HEAD_DIM + OFFSET_ROWS]

    def partial_sums(p):
        return jnp.sum(p.reshape(tk // 8, 8, tq), axis=0)

    assert nchains >= FAST_LOOKAHEAD
    for a in range(FAST_LOOKAHEAD):
        s_even_ref[a] = _dot(k_chunk(0, a), w_ref[a])

    @pl.when(use_bound)
    def _():
        def scores(j, step):
            dj, c = divmod(step, nchains)
            jj = j + dj
            if dj and not isinstance(jj, int):
                jj = jnp.where(jj >= nck, jj - nck, jj)
            return _dot(k_chunk(jj, c), w_ref[c])

        chunks_per_trip = min(FAST_STEPS_PER_TRIP // nchains, nck)
        assert nck % chunks_per_trip == 0

        single_trip = chunks_per_trip == nck

        def body(i, carry):
            j0 = i * chunks_per_trip
            pending = [s_even_ref[a] for a in range(FAST_LOOKAHEAD)]
            for u in range(chunks_per_trip):
                for c in range(nchains):
                    ahead = u * nchains + c + FAST_LOOKAHEAD
                    if not (single_trip and ahead >= nck * nchains):
                        pending.append(scores(j0, ahead))
                    p = jnp.exp2(pending.pop(0))
                    l_ref[c] += partial_sums(p)
                    acc_ref[c] += _dot(vT_ref[value_head_of_chain(c), j0 + u], p.astype(_BF16))
            for a, tile in enumerate(pending):
                s_even_ref[a] = tile
            return carry

        if single_trip:
            body(0, 0)
        else:
            lax.fori_loop(0, nck // chunks_per_trip, body, 0)

    @pl.when(jnp.logical_not(use_bound))
    def _():
        m_ref[...] = jnp.full(m_ref.shape, -jnp.inf, _F32)

        def phase(j_cur, j_next, s_cur_ref, s_next_ref):
            for c in range(nchains):
                vc = vT_ref[value_head_of_chain(c), j_cur]
                s = s_cur_ref[c]
                s_next_ref[c] = _dot(k_chunk(j_next, c), w_ref[c])
                m_prev = m_ref[c]
                m_new = jnp.maximum(m_prev, jnp.max(s, axis=0, keepdims=True))
                alpha = jnp.exp2(m_prev - m_new)
                p = jnp.exp2(s - m_new)
                l_ref[c] = alpha * l_ref[c] + partial_sums(p)
                acc_ref[c] = alpha * acc_ref[c] + _dot(vc, p.astype(_BF16))
                m_ref[c] = m_new

        for c in range(FAST_LOOKAHEAD, nchains):
            s_even_ref[c] = _dot(k_chunk(0, c), w_ref[c])

        def body(i, carry):
            j = 2 * i
            phase(j, j + 1, s_even_ref, s_odd_ref)
            phase(j + 1, jnp.where(j + 2 == nck, 0, j + 2), s_odd_ref, s_even_ref)
            return carry

        lax.fori_loop(0, nck // 2, body, 0)


def _denominator(l_ref, c):
    return jnp.sum(l_ref[c], axis=0, keepdims=True)


def _attn_a_kernel(qT_ref, k_ref, vT_ref, knorm_ref, o_ref, w_ref, s_even_ref, s_odd_ref, m_ref, l_ref, acc_ref,
                   *, tq, tk):
    knorm2 = knorm_ref[_KNORM_A0:_KNORM_A0 + A_KV_HEADS, 0:1]
    kv_head = lambda r: r // A_GROUP
    _softmax_value_loop(qT_ref, k_ref, vT_ref, lambda r: knorm2[kv_head(r):kv_head(r) + 1], kv_head, kv_head,
                        w_ref, s_even_ref, s_odd_ref, m_ref, l_ref, acc_ref, qrows=A_HEADS, tq=tq, tk=tk)
    for c in range(acc_ref.shape[0]):
        blk, r = divmod(c, A_HEADS)
        o = acc_ref[c] / _denominator(l_ref, c)
        o_ref[r * HEAD_DIM:(r + 1) * HEAD_DIM, blk * tq:(blk + 1) * tq] = o.astype(_BF16)


def _attention_scratch(nchains, dv, tq, tk):
    return [
        pltpu.VMEM((nchains, HEAD_DIM + OFFSET_ROWS, tq), _BF16),
        pltpu.VMEM((max(nchains, FAST_LOOKAHEAD), tk, tq), _F32),
        pltpu.VMEM((nchains, tk, tq), _F32),
        pltpu.VMEM((nchains, 1, tq), _F32),
        pltpu.VMEM((nchains, 8, tq), _F32),
        pltpu.VMEM((nchains, dv, tq), _F32),
    ]


def _attn_a(qTa, kA, vTa, knorm, *, tq):
    n, _, s = qTa.shape
    _, nkv, nck, dv, tk = vTa.shape
    nblk = A_QUERY_BLOCKS_PER_STEP
    return pl.pallas_call(
        functools.partial(_attn_a_kernel, tq=tq, tk=tk),
        grid=(n, s // (nblk * tq)),
        in_specs=[
            pl.BlockSpec((None, A_Q, nblk * tq), lambda b, i: (b, 0, i)),
            pl.BlockSpec((None, s, nkv * 2 * HEAD_DIM), lambda b, i: (b, 0, 0)),
            pl.BlockSpec((None, nkv, nck, dv, tk), lambda b, i: (b, 0, 0, 0, 0)),
            pl.BlockSpec((None, _KNORM_ROWS, 128), lambda b, i: (b, 0, 0)),
        ],
        out_specs=pl.BlockSpec((None, A_Q, nblk * tq), lambda b, i: (b, 0, i)),
        out_shape=jax.ShapeDtypeStruct((n, A_Q, s), _BF16),
        scratch_shapes=_attention_scratch(nblk * A_HEADS, HEAD_DIM, tq, tk),
        compiler_params=_params(("parallel", "parallel")),
        name="gqa_attention",
    )(qTa, kA, vTa, knorm)


def _attn_b_kernel(qT_ref, k_ref, vT_ref, knorm_ref, lam_ref, gsub_ref, o_ref, w_ref, s_even_ref, s_odd_ref,
                   m_ref, l_ref, acc_ref, *, tq, tk, lam_init):
    qrows = 2 * B_HEADS_PER_STEP
    knorm2 = knorm_ref[pl.ds(_KNORM_B0 + qrows * pl.program_id(1), qrows), 0:1]
    _softmax_value_loop(qT_ref, k_ref, vT_ref, lambda r: knorm2[r:r + 1], lambda r: r, lambda r: r // 2,
                        w_ref, s_even_ref, s_odd_ref, m_ref, l_ref, acc_ref, qrows=qrows, tq=tq, tk=tk)

    lp = lam_ref[...]
    lam = (jnp.exp(jnp.sum(lp[0:1] * lp[1:2], axis=1, keepdims=True))
           - jnp.exp(jnp.sum(lp[2:3] * lp[3:4], axis=1, keepdims=True)) + lam_init)
    for blk in range(acc_ref.shape[0] // qrows):
        for h in range(B_HEADS_PER_STEP):
            c1 = blk * qrows + 2 * h
            o = acc_ref[c1] / _denominator(l_ref, c1) - lam * (acc_ref[c1 + 1] / _denominator(l_ref, c1 + 1))
            ms = jnp.mean(o * o, axis=0, keepdims=True)
            o = ((o * lax.rsqrt(ms + NORM_EPS)) * gsub_ref[...]) * (1.0 - lam_init)
            o_ref[h * B_V_DIM:(h + 1) * B_V_DIM, blk * tq:(blk + 1) * tq] = o.astype(_BF16)


def _attn_b(qTb, kB, vTb, knorm, lam_params, gsub, *, tq, lam_init):
    n, _, s = qTb.shape
    _, _, nck, dv, tk = vTb.shape
    hps = B_HEADS_PER_STEP
    nblk = B_QUERY_BLOCKS_PER_STEP
    return pl.pallas_call(
        functools.partial(_attn_b_kernel, tq=tq, tk=tk, lam_init=lam_init),
        grid=(n, B_HEADS // hps, s // (nblk * tq)),
        in_specs=[
            pl.BlockSpec((None, hps * 2 * HEAD_DIM, nblk * tq), lambda b, hp, i: (b, hp, i)),
            pl.BlockSpec((None, s, hps * 4 * HEAD_DIM), lambda b, hp, i: (b, 0, hp)),
            pl.BlockSpec((None, hps, nck, dv, tk), lambda b, hp, i: (b, hp, 0, 0, 0)),
            pl.BlockSpec((None, _KNORM_ROWS, 128), lambda b, hp, i: (b, 0, 0)),
            pl.BlockSpec((4, HEAD_DIM), lambda b, hp, i: (0, 0)),
            pl.BlockSpec((B_V_DIM, 1), lambda b, hp, i: (0, 0)),
        ],
        out_specs=pl.BlockSpec((None, hps * B_V_DIM, nblk * tq), lambda b, hp, i: (b, hp, i)),
        out_shape=jax.ShapeDtypeStruct((n, B_V, s), _BF16),
        scratch_shapes=_attention_scratch(nblk * 2 * hps, B_V_DIM, tq, tk),
        compiler_params=_params(("parallel", "parallel", "parallel")),
        name="diff_attention",
    )(qTb, kB, vTb, knorm, lam_params, gsub)


def _mix_and_ffn(x_ref, oTa_ref, oTb_ref, mod_ref, wout_ref, gffn_ref, wgu_ref, wdn_ref, act_ref):
    mixed = _dot_tn(oTa_ref[...], wout_ref[0:A_Q, :]) + _dot_tn(oTb_ref[...], wout_ref[A_Q:, :])
    gt1, sh2, sc2, gt2 = mod_ref[2:3, :], mod_ref[3:4, :], mod_ref[4:5, :], mod_ref[5:6, :]
    x1 = x_ref[...] + gt1 * mixed
    h = ((x1 * _rms_rows(x1)) * gffn_ref[...]) * (1.0 + sc2) + sh2
    hb = h.astype(_BF16)
    dff = wdn_ref.shape[0]
    for c in range(dff // FF_CHUNK):
        cols = slice(c * FF_CHUNK, (c + 1) * FF_CHUNK)
        gate = _dot(hb, wgu_ref[:, cols])
        up = _dot(hb, wgu_ref[:, dff + c * FF_CHUNK:dff + (c + 1) * FF_CHUNK])
        act_ref[:, cols] = ((gate * jax.nn.sigmoid(gate)) * up).astype(_BF16)
    return x1 + gt2 * _dot(act_ref[...], wdn_ref[...])


def _ffn_kernel(x_ref, oTa_ref, oTb_ref, mod_ref, wout_ref, gffn_ref, wgu_ref, wdn_ref, y_ref, act_ref):
    y_ref[...] = _mix_and_ffn(x_ref, oTa_ref, oTb_ref, mod_ref, wout_ref, gffn_ref, wgu_ref, wdn_ref, act_ref)


def _ffn_final_kernel(x_ref, oTa_ref, oTb_ref, mod_ref, wout_ref, gffn_ref, wgu_ref, wdn_ref, modf_ref, gfin_ref,
                      y_ref, act_ref):
    x = _mix_and_ffn(x_ref, oTa_ref, oTb_ref, mod_ref, wout_ref, gffn_ref, wgu_ref, wdn_ref, act_ref)
    shift, scale = modf_ref[0:1, :], modf_ref[1:2, :]
    y_ref[...] = ((x * _rms_rows(x)) * gfin_ref[...]) * (1.0 + scale) + shift


def _resident(shape):
    return pl.BlockSpec(shape, lambda b, t: (0,) * len(shape), pipeline_mode=pl.Buffered(1))


def _out_ffn(x, oTa, oTb, mod, w_out, gffn, w_gu, w_dn, final=None, *, ts):
    n, s, d = x.shape
    dff = w_dn.shape[0]
    assert dff % FF_CHUNK == 0 and w_gu.shape == (d, 2 * dff)
    in_specs = [
        pl.BlockSpec((None, ts, d), lambda b, t: (b, t, 0)),
        pl.BlockSpec((None, A_Q, ts), lambda b, t: (b, 0, t)),
        pl.BlockSpec((None, B_V, ts), lambda b, t: (b, 0, t)),
        pl.BlockSpec((None, 6, d), lambda b, t: (b, 0, 0)),
        _resident((A_Q + B_V, d)),
        pl.BlockSpec((1, d), lambda b, t: (0, 0)),
        _resident((d, 2 * dff)),
        _resident((dff, d)),
    ]
    operands = (x, oTa, oTb, mod, w_out, gffn, w_gu, w_dn)
    if final is not None:
        in_specs += [pl.BlockSpec((None, 2, d), lambda b, t: (b, 0, 0)), pl.BlockSpec((1, d), lambda b, t: (0, 0))]
        operands += tuple(final)
    return pl.pallas_call(
        _ffn_kernel if final is None else _ffn_final_kernel,
        grid=(n, s // ts),
        in_specs=in_specs,
        out_specs=pl.BlockSpec((None, ts, d), lambda b, t: (b, t, 0)),
        out_shape=jax.ShapeDtypeStruct((n, s, d), _F32),
        scratch_shapes=[pltpu.VMEM((ts, dff), _BF16)],
        compiler_params=_params(("parallel", "parallel")),
        name="out_proj_swiglu",
    )(*operands)


def _rope_table_t(s):
    def tables(pos, dim, theta):
        inv = 1.0 / (theta ** (jnp.arange(0, dim, 2, dtype=_F32) / dim))
        ang = pos.astype(_F32)[:, None] * inv[None, :]
        return jnp.cos(ang).T, jnp.sin(ang).T

    t = jnp.arange(s)
    cos_r, sin_r = tables(t // GRID_W, HEAD_DIM // 2, AXIAL_THETA)
    cos_c, sin_c = tables(t % GRID_W, HEAD_DIM // 2, AXIAL_THETA)
    cos_p, sin_p = tables(t, PARTIAL_ROT, ROPE_THETA)
    return jnp.concatenate([cos_r, sin_r, cos_c, sin_c, cos_p, sin_p], axis=0)


def _trunks(xs, cs, w_ada, b_ada, g_mix, w_in, g_q_a, g_k_a, lam_q1, lam_k1, lam_q2, lam_k2, g_subln,
            w_out, g_ffn, w_gate_up, w_down, w_ada_final, b_ada_final, g_final, *, ts, tq, tk):
    d = xs[0].shape[-1]
    depth = w_ada.shape[0]

    rows = [c.shape[0] for c in cs]
    c_pad = jnp.zeros((8, d), _F32).at[:sum(rows)].set(jnp.concatenate(cs, axis=0))
    mod_all = _modulation(c_pad, w_ada, b_ada[:, None, :])
    modf_all = _modulation(c_pad, w_ada_final[None], b_ada_final[None, None, :])[0]

    w_inT = jnp.swapaxes(w_in, 1, 2).astype(_BF16)
    w_out_b = w_out.astype(_BF16)
    w_gu = w_gate_up.astype(_BF16)
    w_dn = w_down.astype(_BF16)
    lam_params = jnp.stack([lam_q1, lam_k1, lam_q2, lam_k2], axis=1)

    outs = []
    row0 = 0
    for x, n in zip(xs, rows):
        s = x.shape[1]
        mod = mod_all[:, row0:row0 + n].reshape(depth, n, 6, d)
        modf = modf_all[row0:row0 + n].reshape(n, 2, d)
        row0 += n
        tab = _rope_table_t(s)
        for l in range(depth):
            qTa, kA, vTa, qTb, kB, vTb, knorm = _inproj(
                x, mod[l], g_mix[l][None, :], w_inT[l], g_q_a[l][:, None], g_k_a[l][:, None], tab, ts=ts, tk=tk)
            oTa = _attn_a(qTa, kA, vTa, knorm, tq=tq)
            lam_init = 0.8 - 0.6 * math.exp(-0.3 * l)
            oTb = _attn_b(qTb, kB, vTb, knorm, lam_params[l], g_subln[l][:, None], tq=tq, lam_init=lam_init)
            final = (modf, g_final[None, :]) if l == depth - 1 else None
            x = _out_ffn(x, oTa, oTb, mod[l], w_out_b[l], g_ffn[l][None, :], w_gu[l], w_dn[l], final, ts=ts)
        outs.append(x)
    return tuple(outs)


def _trunk(x, c, *weights, ts, tq, tk):
    return _trunks((x,), (c,), *weights, ts=ts, tq=tq, tk=tk)[0]


def kernel(x_prompt, x_sample, c_prompt, c_sample, w_ada, b_ada, g_mix, w_in, g_q_a, g_k_a, lam_q1, lam_k1, lam_q2, lam_k2, g_subln, w_out, g_ffn, w_gate_up, w_down, w_ada_final, b_ada_final, g_final):
    return _trunks((x_prompt, x_sample), (c_prompt, c_sample), w_ada, b_ada, g_mix, w_in, g_q_a, g_k_a,
                   lam_q1, lam_k1, lam_q2, lam_k2, g_subln, w_out, g_ffn, w_gate_up, w_down,
                   w_ada_final, b_ada_final, g_final, ts=512, tq=256, tk=256)
```

```python
import functools
import math

import jax
import jax.numpy as jnp
from jax import lax
from jax.experimental import pallas as pl
from jax.experimental.pallas import tpu as pltpu

HEAD_DIM = 64
A_HEADS = 8
A_KV_HEADS = 2
A_GROUP = A_HEADS // A_KV_HEADS
B_HEADS = 4
B_V_DIM = 2 * HEAD_DIM
A_Q = A_HEADS * HEAD_DIM
A_KV = A_KV_HEADS * HEAD_DIM
B_QK = B_HEADS * 2 * HEAD_DIM
B_V = B_HEADS * B_V_DIM
IN_WIDTH = A_Q + 2 * A_KV + 2 * B_QK + B_V
GRID_W = 64
AXIAL_THETA = 10000.0
ROPE_THETA = 500000.0
PARTIAL_ROT = HEAD_DIM // 4
NORM_EPS = 1e-6
QK_SCALE = HEAD_DIM ** -0.5 * math.log2(math.e)

_QA0, _KA0, _VA0 = 0, A_Q, A_Q + A_KV
_QB0 = A_Q + 2 * A_KV
_KB0 = _QB0 + B_QK
_VB0 = _KB0 + B_QK

_KNORM_B0, _KNORM_A0, _KNORM_ROWS = 0, 2 * B_HEADS, 16
FAST_SOFTMAX_MAX_BOUND = 48.0
OFFSET_ROWS = 16
A_QUERY_BLOCKS_PER_STEP = 2
B_HEADS_PER_STEP = 1
B_QUERY_BLOCKS_PER_STEP = 8
FAST_STEPS_PER_TRIP = 256
FAST_LOOKAHEAD = 4

FF_CHUNK = 256
V7X_VMEM_LIMIT_BYTES = 56 * 1024 * 1024

_BF16 = jnp.bfloat16
_F32 = jnp.float32


def _params(semantics):
    return pltpu.CompilerParams(dimension_semantics=semantics, vmem_limit_bytes=V7X_VMEM_LIMIT_BYTES)


def _dot(a, b):
    return jnp.dot(a, b, preferred_element_type=_F32)


def _dot_nt(a, b):
    return lax.dot_general(a, b, (((1,), (1,)), ((), ())), preferred_element_type=_F32)


def _dot_tn(a, b):
    return lax.dot_general(a, b, (((0,), (0,)), ((), ())), preferred_element_type=_F32)


def _split_bf16(x):
    hi = x.astype(_BF16)
    lo = (x - hi.astype(_F32)).astype(_BF16)
    return hi, lo


def _mod_kernel(c_ref, w_ref, b_ref, o_ref):
    c = c_ref[...]
    c_act = c * jax.nn.sigmoid(c)
    c_hi, c_lo = _split_bf16(c_act)
    w_hi, w_lo = _split_bf16(w_ref[...])
    acc = _dot(c_hi, w_hi) + (_dot(c_lo, w_hi) + _dot(c_hi, w_lo))
    o_ref[...] = acc + b_ref[...]


def _modulation(c_pad, w, b, tn=1024):
    nl, d, n = w.shape
    return pl.pallas_call(
        _mod_kernel,
        grid=(nl, n // tn),
        in_specs=[
            pl.BlockSpec((8, d), lambda l, j: (0, 0)),
            pl.BlockSpec((None, d, tn), lambda l, j: (l, 0, j)),
            pl.BlockSpec((None, 1, tn), lambda l, j: (l, 0, j)),
        ],
        out_specs=pl.BlockSpec((None, 8, tn), lambda l, j: (l, 0, j)),
        out_shape=jax.ShapeDtypeStruct((nl, 8, n), _F32),
        compiler_params=_params(("parallel", "parallel")),
        name="adaln_modulation",
    )(c_pad, w, b)


def _rms_rows(x):
    return lax.rsqrt(jnp.mean(x * x, axis=1, keepdims=True) + NORM_EPS)


def _inproj_kernel(x_ref, mod_ref, gmix_ref, wT_ref, gq_ref, gk_ref, tab_ref,
                   qTa_ref, kA_ref, vTa_ref, qTb_ref, kB_ref, vTb_ref, knorm_ref, *, tk):
    ts = x_ref.shape[0]
    sh1 = mod_ref[0:1, :]
    sc1 = mod_ref[1:2, :]

    @pl.when(pl.program_id(1) == 0)
    def _():
        knorm_ref[...] = jnp.zeros(knorm_ref.shape, _F32)

    def head_norm(x3, g_ref):
        ms = jnp.mean(x3 * x3, axis=1, keepdims=True)
        return (x3 * lax.rsqrt(ms + NORM_EPS)) * g_ref[...][None]

    def max_key_norm2(k3):
        kf = k3.astype(_BF16).astype(_F32)
        return jnp.max(jnp.sum(kf * kf, axis=1), axis=1, keepdims=True)

    one_hot_rows = (lax.broadcasted_iota(jnp.int32, (HEAD_DIM, tk), 0) == 0).astype(_F32)

    def widened_keys_t(k3):
        parts = []
        for hh in range(k3.shape[0]):
            parts += [k3[hh], one_hot_rows]
        return jnp.concatenate(parts, axis=0).T.astype(_BF16)

    for c in range(ts // tk):
        tok = slice(c * tk, (c + 1) * tk)
        x = x_ref[tok, :]
        h = ((x * _rms_rows(x)) * gmix_ref[...]) * (1.0 + sc1) + sh1
        hbT = h.T.astype(_BF16)

        def proj_t(lo, n):
            return _dot(wT_ref[lo:lo + n, :], hbT)

        tab = tab_ref[:, tok]
        cos_r, sin_r, cos_c, sin_c = tab[0:16], tab[16:32], tab[32:48], tab[48:64]
        cos_p, sin_p = tab[64:72], tab[72:80]

        def axial(x3):
            r1, r2, c1, c2 = x3[:, 0:16], x3[:, 16:32], x3[:, 32:48], x3[:, 48:64]
            return jnp.concatenate([r1 * cos_r - r2 * sin_r, r1 * sin_r + r2 * cos_r,
                                    c1 * cos_c - c2 * sin_c, c1 * sin_c + c2 * cos_c], axis=1)

        def partial(x3):
            half = PARTIAL_ROT // 2
            x1, x2, rest = x3[:, 0:half], x3[:, half:PARTIAL_ROT], x3[:, PARTIAL_ROT:]
            return jnp.concatenate([x1 * cos_p - x2 * sin_p, x1 * sin_p + x2 * cos_p, rest], axis=1)

        qa = proj_t(_QA0, A_Q).reshape(A_HEADS, HEAD_DIM, tk)
        qa = axial(head_norm(qa, gq_ref)) * QK_SCALE
        qTa_ref[:, tok] = qa.reshape(A_Q, tk).astype(_BF16)

        ka = axial(head_norm(proj_t(_KA0, A_KV).reshape(A_KV_HEADS, HEAD_DIM, tk), gk_ref))
        kA_ref[tok, :] = widened_keys_t(ka)

        va = proj_t(_VA0, A_KV).astype(_BF16)
        for g in range(A_KV_HEADS):
            vTa_ref[g, c] = va[g * HEAD_DIM:(g + 1) * HEAD_DIM]

        qb = proj_t(_QB0, B_QK).reshape(2 * B_HEADS, HEAD_DIM, tk)
        qTb_ref[:, tok] = (partial(qb) * QK_SCALE).reshape(B_QK, tk).astype(_BF16)

        kb = partial(proj_t(_KB0, B_QK).reshape(2 * B_HEADS, HEAD_DIM, tk))
        kB_ref[tok, :] = widened_keys_t(kb)

        vb = proj_t(_VB0, B_V).astype(_BF16)
        for hd in range(B_HEADS):
            vTb_ref[hd, c] = vb[hd * B_V_DIM:(hd + 1) * B_V_DIM]

        lanes = knorm_ref.shape[1]
        b_rows = slice(_KNORM_B0, _KNORM_B0 + 2 * B_HEADS)
        a_rows = slice(_KNORM_A0, _KNORM_A0 + A_KV_HEADS)
        knorm_ref[b_rows, :] = jnp.maximum(knorm_ref[b_rows, :],
                                           jnp.broadcast_to(max_key_norm2(kb), (2 * B_HEADS, lanes)))
        knorm_ref[a_rows, :] = jnp.maximum(knorm_ref[a_rows, :],
                                           jnp.broadcast_to(max_key_norm2(ka), (A_KV_HEADS, lanes)))


def _inproj(x, mod, gmix, w_inT, gq, gk, tab, *, ts, tk):
    n, s, d = x.shape
    nck = s // tk
    cpt = ts // tk
    out_shape = (
        jax.ShapeDtypeStruct((n, A_Q, s), _BF16),
        jax.ShapeDtypeStruct((n, s, 2 * A_KV), _BF16),
        jax.ShapeDtypeStruct((n, A_KV_HEADS, nck, HEAD_DIM, tk), _BF16),
        jax.ShapeDtypeStruct((n, B_QK, s), _BF16),
        jax.ShapeDtypeStruct((n, s, 2 * B_QK), _BF16),
        jax.ShapeDtypeStruct((n, B_HEADS, nck, B_V_DIM, tk), _BF16),
        jax.ShapeDtypeStruct((n, _KNORM_ROWS, 128), _F32),
    )
    return pl.pallas_call(
        functools.partial(_inproj_kernel, tk=tk),
        grid=(n, s // ts),
        in_specs=[
            pl.BlockSpec((None, ts, d), lambda b, t: (b, t, 0)),
            pl.BlockSpec((None, 6, d), lambda b, t: (b, 0, 0)),
            pl.BlockSpec((1, d), lambda b, t: (0, 0)),
            pl.BlockSpec((IN_WIDTH, d), lambda b, t: (0, 0)),
            pl.BlockSpec((HEAD_DIM, 1), lambda b, t: (0, 0)),
            pl.BlockSpec((HEAD_DIM, 1), lambda b, t: (0, 0)),
            pl.BlockSpec((80, ts), lambda b, t: (0, t)),
        ],
        out_specs=(
            pl.BlockSpec((None, A_Q, ts), lambda b, t: (b, 0, t)),
            pl.BlockSpec((None, ts, 2 * A_KV), lambda b, t: (b, t, 0)),
            pl.BlockSpec((None, A_KV_HEADS, cpt, HEAD_DIM, tk), lambda b, t: (b, 0, t, 0, 0)),
            pl.BlockSpec((None, B_QK, ts), lambda b, t: (b, 0, t)),
            pl.BlockSpec((None, ts, 2 * B_QK), lambda b, t: (b, t, 0)),
            pl.BlockSpec((None, B_HEADS, cpt, B_V_DIM, tk), lambda b, t: (b, 0, t, 0, 0)),
            pl.BlockSpec((None, _KNORM_ROWS, 128), lambda b, t: (b, 0, 0)),
        ),
        out_shape=out_shape,
        compiler_params=_params(("parallel", "arbitrary")),
        name="mix_in_projection",
    )(x, mod, gmix, w_inT, gq, gk, tab)


def _softmax_value_loop(qT_ref, k_ref, vT_ref, knorm2_of_row, key_head_of_row, value_head_of_row, w_ref,
                        s_even_ref, s_odd_ref, m_ref, l_ref, acc_ref, *, qrows, tq, tk):
    nck = vT_ref.shape[1]
    nchains = qrows * (qT_ref.shape[-1] // tq)
    knorm2_of_chain = lambda c: knorm2_of_row(c % qrows)
    key_head_of_chain = lambda c: key_head_of_row(c % qrows)
    value_head_of_chain = lambda c: value_head_of_row(c % qrows)
    acc_ref[...] = jnp.zeros(acc_ref.shape, _F32)
    l_ref[...] = jnp.zeros(l_ref.shape, _F32)

    qs = [qT_ref[(c % qrows) * HEAD_DIM:(c % qrows + 1) * HEAD_DIM, (c // qrows) * tq:(c // qrows + 1) * tq]
          for c in range(nchains)]
    bounds = []
    for c in range(nchains):
        qf = qs[c].astype(_F32)
        bounds.append(jnp.sqrt(jnp.sum(qf * qf, axis=0, keepdims=True) * knorm2_of_chain(c)))
    use_bound = jnp.max(jnp.concatenate(bounds, axis=0)) <= FAST_SOFTMAX_MAX_BOUND

    first_row = lax.broadcasted_iota(jnp.int32, (OFFSET_ROWS, tq), 0) == 0
    for c in range(nchains):
        offset = -jnp.minimum(bounds[c], FAST_SOFTMAX_MAX_BOUND)
        offset_rows = jnp.where(first_row, jnp.broadcast_to(offset, (OFFSET_ROWS, tq)), 0.0)
        w_ref[c] = jnp.concatenate([qs[c], offset_rows.astype(_BF16)], axis=0)

    def k_chunk(j, c):
        lane0 = key_head_of_chain(c) * 2 * HEAD_DIM
        return k_ref[pl.ds(pl.multiple_of(j * tk, tk), tk), lane0:lane0 + HEAD_DIM + OFFSET_ROWS]

    def partial_sums(p):
        return jnp.sum(p.reshape(tk // 8, 8, tq), axis=0)

    assert nchains >= FAST_LOOKAHEAD
    for a in range(FAST_LOOKAHEAD):
        s_even_ref[a] = _dot(k_chunk(0, a), w_ref[a])

    @pl.when(use_bound)
    def _():
        def scores(j, step):
            dj, c = divmod(step, nchains)
            jj = j + dj
            if dj and not isinstance(jj, int):
                jj = jnp.where(jj >= nck, jj - nck, jj)
            return _dot(k_chunk(jj, c), w_ref[c])

        chunks_per_trip = min(FAST_STEPS_PER_TRIP // nchains, nck)
        assert nck % chunks_per_trip == 0

        single_trip = chunks_per_trip == nck

        def body(i, carry):
            j0 = i * chunks_per_trip
            pending = [s_even_ref[a] for a in range(FAST_LOOKAHEAD)]
            for u in range(chunks_per_trip):
                for c in range(nchains):
                    ahead = u * nchains + c + FAST_LOOKAHEAD
                    if not (single_trip and ahead >= nck * nchains):
                        pending.append(scores(j0, ahead))
                    p = jnp.exp2(pending.pop(0))
                    l_ref[c] += partial_sums(p)
                    acc_ref[c] += _dot(vT_ref[value_head_of_chain(c), j0 + u], p.astype(_BF16))
            for a, tile in enumerate(pending):
                s_even_ref[a] = tile
            return carry

        if single_trip:
            body(0, 0)
        else:
            lax.fori_loop(0, nck // chunks_per_trip, body, 0)

    @pl.when(jnp.logical_not(use_bound))
    def _():
        m_ref[...] = jnp.full(m_ref.shape, -jnp.inf, _F32)

        def phase(j_cur, j_next, s_cur_ref, s_next_ref):
            for c in range(nchains):
                vc = vT_ref[value_head_of_chain(c), j_cur]
                s = s_cur_ref[c]
                s_next_ref[c] = _dot(k_chunk(j_next, c), w_ref[c])
                m_prev = m_ref[c]
                m_new = jnp.maximum(m_prev, jnp.max(s, axis=0, keepdims=True))
                alpha = jnp.exp2(m_prev - m_new)
                p = jnp.exp2(s - m_new)
                l_ref[c] = alpha * l_ref[c] + partial_sums(p)
                acc_ref[c] = alpha * acc_ref[c] + _dot(vc, p.astype(_BF16))
                m_ref[c] = m_new

        for c in range(FAST_LOOKAHEAD, nchains):
            s_even_ref[c] = _dot(k_chunk(0, c), w_ref[c])

        def body(i, carry):
            j = 2 * i
            phase(j, j + 1, s_even_ref, s_odd_ref)
            phase(j + 1, jnp.where(j + 2 == nck, 0, j + 2), s_odd_ref, s_even_ref)
            return carry

        lax.fori_loop(0, nck // 2, body, 0)


def _denominator(l_ref, c):
    return jnp.sum(l_ref[c], axis=0, keepdims=True)


def _attn_a_kernel(qT_ref, k_ref, vT_ref, knorm_ref, o_ref, w_ref, s_even_ref, s_odd_ref, m_ref, l_ref, acc_ref,
                   *, tq, tk):
    knorm2 = knorm_ref[_KNORM_A0:_KNORM_A0 + A_KV_HEADS, 0:1]
    kv_head = lambda r: r // A_GROUP
    _softmax_value_loop(qT_ref, k_ref, vT_ref, lambda r: knorm2[kv_head(r):kv_head(r) + 1], kv_head, kv_head,
                        w_ref, s_even_ref, s_odd_ref, m_ref, l_ref, acc_ref, qrows=A_HEADS, tq=tq, tk=tk)
    for c in range(acc_ref.shape[0]):
        blk, r = divmod(c, A_HEADS)
        o = acc_ref[c] / _denominator(l_ref, c)
        o_ref[r * HEAD_DIM:(r + 1) * HEAD_DIM, blk * tq:(blk + 1) * tq] = o.astype(_BF16)


def _attention_scratch(nchains, dv, tq, tk):
    return [
        pltpu.VMEM((nchains, HEAD_DIM + OFFSET_ROWS, tq), _BF16),
        pltpu.VMEM((max(nchains, FAST_LOOKAHEAD), tk, tq), _F32),
        pltpu.VMEM((nchains, tk, tq), _F32),
        pltpu.VMEM((nchains, 1, tq), _F32),
        pltpu.VMEM((nchains, 8, tq), _F32),
        pltpu.VMEM((nchains, dv, tq), _F32),
    ]


def _attn_a(qTa, kA, vTa, knorm, *, tq):
    n, _, s = qTa.shape
    _, nkv, nck, dv, tk = vTa.shape
    nblk = A_QUERY_BLOCKS_PER_STEP
    return pl.pallas_call(
        functools.partial(_attn_a_kernel, tq=tq, tk=tk),
        grid=(n, s // (nblk * tq)),
        in_specs=[
            pl.BlockSpec((None, A_Q, nblk * tq), lambda b, i: (b, 0, i)),
            pl.BlockSpec((None, s, nkv * 2 * HEAD_DIM), lambda b, i: (b, 0, 0)),
            pl.BlockSpec((None, nkv, nck, dv, tk), lambda b, i: (b, 0, 0, 0, 0)),
            pl.BlockSpec((None, _KNORM_ROWS, 128), lambda b, i: (b, 0, 0)),
        ],
        out_specs=pl.BlockSpec((None, A_Q, nblk * tq), lambda b, i: (b, 0, i)),
        out_shape=jax.ShapeDtypeStruct((n, A_Q, s), _BF16),
        scratch_shapes=_attention_scratch(nblk * A_HEADS, HEAD_DIM, tq, tk),
        compiler_params=_params(("parallel", "parallel")),
        name="gqa_attention",
    )(qTa, kA, vTa, knorm)


def _attn_b_kernel(qT_ref, k_ref, vT_ref, knorm_ref, lam_ref, gsub_ref, o_ref, w_ref, s_even_ref, s_odd_ref,
                   m_ref, l_ref, acc_ref, *, tq, tk, lam_init):
    qrows = 2 * B_HEADS_PER_STEP
    knorm2 = knorm_ref[pl.ds(_KNORM_B0 + qrows * pl.program_id(1), qrows), 0:1]
    _softmax_value_loop(qT_ref, k_ref, vT_ref, lambda r: knorm2[r:r + 1], lambda r: r, lambda r: r // 2,
                        w_ref, s_even_ref, s_odd_ref, m_ref, l_ref, acc_ref, qrows=qrows, tq=tq, tk=tk)

    lp = lam_ref[...]
    lam = (jnp.exp(jnp.sum(lp[0:1] * lp[1:2], axis=1, keepdims=True))
           - jnp.exp(jnp.sum(lp[2:3] * lp[3:4], axis=1, keepdims=True)) + lam_init)
    for blk in range(acc_ref.shape[0] // qrows):
        for h in range(B_HEADS_PER_STEP):
            c1 = blk * qrows + 2 * h
            o = acc_ref[c1] / _denominator(l_ref, c1) - lam * (acc_ref[c1 + 1] / _denominator(l_ref, c1 + 1))
            ms = jnp.mean(o * o, axis=0, keepdims=True)
            o = ((o * lax.rsqrt(ms + NORM_EPS)) * gsub_ref[...]) * (1.0 - lam_init)
            o_ref[h * B_V_DIM:(h + 1) * B_V_DIM, blk * tq:(blk + 1) * tq] = o.astype(_BF16)


def _attn_b(qTb, kB, vTb, knorm, lam_params, gsub, *, tq, lam_init):
    n, _, s = qTb.shape
    _, _, nck, dv, tk = vTb.shape
    hps = B_HEADS_PER_STEP
    nblk = B_QUERY_BLOCKS_PER_STEP
    return pl.pallas_call(
        functools.partial(_attn_b_kernel, tq=tq, tk=tk, lam_init=lam_init),
        grid=(n, B_HEADS // hps, s // (nblk * tq)),
        in_specs=[
            pl.BlockSpec((None, hps * 2 * HEAD_DIM, nblk * tq), lambda b, hp, i: (b, hp, i)),
            pl.BlockSpec((None, s, hps * 4 * HEAD_DIM), lambda b, hp, i: (b, 0, hp)),
            pl.BlockSpec((None, hps, nck, dv, tk), lambda b, hp, i: (b, hp, 0, 0, 0)),
            pl.BlockSpec((None, _KNORM_ROWS, 128), lambda b, hp, i: (b, 0, 0)),
            pl.BlockSpec((4, HEAD_DIM), lambda b, hp, i: (0, 0)),
            pl.BlockSpec((B_V_DIM, 1), lambda b, hp, i: (0, 0)),
        ],
        out_specs=pl.BlockSpec((None, hps * B_V_DIM, nblk * tq), lambda b, hp, i: (b, hp, i)),
        out_shape=jax.ShapeDtypeStruct((n, B_V, s), _BF16),
        scratch_shapes=_attention_scratch(nblk * 2 * hps, B_V_DIM, tq, tk),
        compiler_params=_params(("parallel", "parallel", "parallel")),
        name="diff_attention",
    )(qTb, kB, vTb, knorm, lam_params, gsub)


def _mix_and_ffn(x_ref, oTa_ref, oTb_ref, mod_ref, wout_ref, gffn_ref, wgu_ref, wdn_ref, act_ref):
    mixed = _dot_tn(oTa_ref[...], wout_ref[0:A_Q, :]) + _dot_tn(oTb_ref[...], wout_ref[A_Q:, :])
    gt1, sh2, sc2, gt2 = mod_ref[2:3, :], mod_ref[3:4, :], mod_ref[4:5, :], mod_ref[5:6, :]
    x1 = x_ref[...] + gt1 * mixed
    h = ((x1 * _rms_rows(x1)) * gffn_ref[...]) * (1.0 + sc2) + sh2
    hb = h.astype(_BF16)
    dff = wdn_ref.shape[0]
    for c in range(dff // FF_CHUNK):
        cols = slice(c * FF_CHUNK, (c + 1) * FF_CHUNK)
        gate = _dot(hb, wgu_ref[:, cols])
        up = _dot(hb, wgu_ref[:, dff + c * FF_CHUNK:dff + (c + 1) * FF_CHUNK])
        act_ref[:, cols] = ((gate * jax.nn.sigmoid(gate)) * up).astype(_BF16)
    return x1 + gt2 * _dot(act_ref[...], wdn_ref[...])


def _ffn_kernel(x_ref, oTa_ref, oTb_ref, mod_ref, wout_ref, gffn_ref, wgu_ref, wdn_ref, y_ref, act_ref):
    y_ref[...] = _mix_and_ffn(x_ref, oTa_ref, oTb_ref, mod_ref, wout_ref, gffn_ref, wgu_ref, wdn_ref, act_ref)


def _ffn_final_kernel(x_ref, oTa_ref, oTb_ref, mod_ref, wout_ref, gffn_ref, wgu_ref, wdn_ref, modf_ref, gfin_ref,
                      y_ref, act_ref):
    x = _mix_and_ffn(x_ref, oTa_ref, oTb_ref, mod_ref, wout_ref, gffn_ref, wgu_ref, wdn_ref, act_ref)
    shift, scale = modf_ref[0:1, :], modf_ref[1:2, :]
    y_ref[...] = ((x * _rms_rows(x)) * gfin_ref[...]) * (1.0 + scale) + shift


def _resident(shape):
    return pl.BlockSpec(shape, lambda b, t: (0,) * len(shape), pipeline_mode=pl.Buffered(1))


def _out_ffn(x, oTa, oTb, mod, w_out, gffn, w_gu, w_dn, final=None, *, ts):
    n, s, d = x.shape
    dff = w_dn.shape[0]
    assert dff % FF_CHUNK == 0 and w_gu.shape == (d, 2 * dff)
    in_specs = [
        pl.BlockSpec((None, ts, d), lambda b, t: (b, t, 0)),
        pl.BlockSpec((None, A_Q, ts), lambda b, t: (b, 0, t)),
        pl.BlockSpec((None, B_V, ts), lambda b, t: (b, 0, t)),
        pl.BlockSpec((None, 6, d), lambda b, t: (b, 0, 0)),
        _resident((A_Q + B_V, d)),
        pl.BlockSpec((1, d), lambda b, t: (0, 0)),
        _resident((d, 2 * dff)),
        _resident((dff, d)),
    ]
    operands = (x, oTa, oTb, mod, w_out, gffn, w_gu, w_dn)
    if final is not None:
        in_specs += [pl.BlockSpec((None, 2, d), lambda b, t: (b, 0, 0)), pl.BlockSpec((1, d), lambda b, t: (0, 0))]
        operands += tuple(final)
    return pl.pallas_call(
        _ffn_kernel if final is None else _ffn_final_kernel,
        grid=(n, s // ts),
        in_specs=in_specs,
        out_specs=pl.BlockSpec((None, ts, d), lambda b, t: (b, t, 0)),
        out_shape=jax.ShapeDtypeStruct((n, s, d), _F32),
        scratch_shapes=[pltpu.VMEM((ts, dff), _BF16)],
        compiler_params=_params(("parallel", "parallel")),
        name="out_proj_swiglu",
    )(*operands)


def _rope_table_t(s):
    def tables(pos, dim, theta):
        inv = 1.0 / (theta ** (jnp.arange(0, dim, 2, dtype=_F32) / dim))
        ang = pos.astype(_F32)[:, None] * inv[None, :]
        return jnp.cos(ang).T, jnp.sin(ang).T

    t = jnp.arange(s)
    cos_r, sin_r = tables(t // GRID_W, HEAD_DIM // 2, AXIAL_THETA)
    cos_c, sin_c = tables(t % GRID_W, HEAD_DIM // 2, AXIAL_THETA)
    cos_p, sin_p = tables(t, PARTIAL_ROT, ROPE_THETA)
    return jnp.concatenate([cos_r, sin_r, cos_c, sin_c, cos_p, sin_p], axis=0)


def _trunks(xs, cs, w_ada, b_ada, g_mix, w_in, g_q_a, g_k_a, lam_q1, lam_k1, lam_q2, lam_k2, g_subln,
            w_out, g_ffn, w_gate_up, w_down, w_ada_final, b_ada_final, g_final, *, ts, tq, tk):
    d = xs[0].shape[-1]
    depth = w_ada.shape[0]

    rows = [c.shape[0] for c in cs]
    c_pad = jnp.zeros((8, d), _F32).at[:sum(rows)].set(jnp.concatenate(cs, axis=0))
    mod_all = _modulation(c_pad, w_ada, b_ada[:, None, :])
    modf_all = _modulation(c_pad, w_ada_final[None], b_ada_final[None, None, :])[0]

    w_inT = jnp.swapaxes(w_in, 1, 2).astype(_BF16)
    w_out_b = w_out.astype(_BF16)
    w_gu = w_gate_up.astype(_BF16)
    w_dn = w_down.astype(_BF16)
    lam_params = jnp.stack([lam_q1, lam_k1, lam_q2, lam_k2], axis=1)

    outs = []
    row0 = 0
    for x, n in zip(xs, rows):
        s = x.shape[1]
        mod = mod_all[:, row0:row0 + n].reshape(depth, n, 6, d)
        modf = modf_all[row0:row0 + n].reshape(n, 2, d)
        row0 += n
        tab = _rope_table_t(s)
        for l in range(depth):
            qTa, kA, vTa, qTb, kB, vTb, knorm = _inproj(
                x, mod[l], g_mix[l][None, :], w_inT[l], g_q_a[l][:, None], g_k_a[l][:, None], tab, ts=ts, tk=tk)
            oTa = _attn_a(qTa, kA, vTa, knorm, tq=tq)
            lam_init = 0.8 - 0.6 * math.exp(-0.3 * l)
            oTb = _attn_b(qTb, kB, vTb, knorm, lam_params[l], g_subln[l][:, None], tq=tq, lam_init=lam_init)
            final = (modf, g_final[None, :]) if l == depth - 1 else None
            x = _out_ffn(x, oTa, oTb, mod[l], w_out_b[l], g_ffn[l][None, :], w_gu[l], w_dn[l], final, ts=ts)
        outs.append(x)
    return tuple(outs)


def _trunk(x, c, *weights, ts, tq, tk):
    return _trunks((x,), (c,), *weights, ts=ts, tq=tq, tk=tk)[0]


def kernel(x_prompt, x_sample, c_prompt, c_sample, w_ada, b_ada, g_mix, w_in, g_q_a, g_k_a, lam_q1, lam_k1, lam_q2, lam_k2, g_subln, w_out, g_ffn, w_gate_up, w_down, w_ada_final, b_ada_final, g_final):
    return _trunks((x_prompt, x_sample), (c_prompt, c_sample), w_ada, b_ada, g_mix, w_in, g_q_a, g_k_a,
                   lam_q1, lam_k1, lam_q2, lam_k2, g_subln, w_out, g_ffn, w_gate_up, w_down,
                   w_ada_final, b_ada_final, g_final, ts=512, tq=256, tk=256)
```

```python
import functools
import math

import jax
import jax.numpy as jnp
from jax import lax
from jax.experimental import pallas as pl
from jax.experimental.pallas import tpu as pltpu

HEAD_DIM = 64
A_HEADS = 8
A_KV_HEADS = 2
A_GROUP = A_HEADS // A_KV_HEADS
B_HEADS = 4
B_V_DIM = 2 * HEAD_DIM
A_Q = A_HEADS * HEAD_DIM
A_KV = A_KV_HEADS * HEAD_DIM
B_QK = B_HEADS * 2 * HEAD_DIM
B_V = B_HEADS * B_V_DIM
IN_WIDTH = A_Q + 2 * A_KV + 2 * B_QK + B_V
GRID_W = 64
AXIAL_THETA = 10000.0
ROPE_THETA = 500000.0
PARTIAL_ROT = HEAD_DIM // 4
NORM_EPS = 1e-6
QK_SCALE = HEAD_DIM ** -0.5 * math.log2(math.e)

_QA0, _KA0, _VA0 = 0, A_Q, A_Q + A_KV
_QB0 = A_Q + 2 * A_KV
_KB0 = _QB0 + B_QK
_VB0 = _KB0 + B_QK

_KNORM_B0, _KNORM_A0, _KNORM_ROWS = 0, 2 * B_HEADS, 16
FAST_SOFTMAX_MAX_BOUND = 48.0
OFFSET_ROWS = 16
A_QUERY_BLOCKS_PER_STEP = 4
B_HEADS_PER_STEP = 1
B_QUERY_BLOCKS_PER_STEP = 16
FAST_STEPS_PER_TRIP = 256
FAST_LOOKAHEAD = 4

FF_CHUNK = 256
V7X_VMEM_LIMIT_BYTES = 56 * 1024 * 1024

_BF16 = jnp.bfloat16
_F32 = jnp.float32


def _params(semantics):
    return pltpu.CompilerParams(dimension_semantics=semantics, vmem_limit_bytes=V7X_VMEM_LIMIT_BYTES)


def _dot(a, b):
    return jnp.dot(a, b, preferred_element_type=_F32)


def _dot_nt(a, b):
    return lax.dot_general(a, b, (((1,), (1,)), ((), ())), preferred_element_type=_F32)


def _dot_tn(a, b):
    return lax.dot_general(a, b, (((0,), (0,)), ((), ())), preferred_element_type=_F32)


def _split_bf16(x):
    hi = x.astype(_BF16)
    lo = (x - hi.astype(_F32)).astype(_BF16)
    return hi, lo


def _mod_kernel(c_ref, w_ref, b_ref, o_ref):
    c = c_ref[...]
    c_act = c * jax.nn.sigmoid(c)
    c_hi, c_lo = _split_bf16(c_act)
    w_hi, w_lo = _split_bf16(w_ref[...])
    acc = _dot(c_hi, w_hi) + (_dot(c_lo, w_hi) + _dot(c_hi, w_lo))
    o_ref[...] = acc + b_ref[...]


def _modulation(c_pad, w, b, tn=1024):
    nl, d, n = w.shape
    return pl.pallas_call(
        _mod_kernel,
        grid=(nl, n // tn),
        in_specs=[
            pl.BlockSpec((8, d), lambda l, j: (0, 0)),
            pl.BlockSpec((None, d, tn), lambda l, j: (l, 0, j)),
            pl.BlockSpec((None, 1, tn), lambda l, j: (l, 0, j)),
        ],
        out_specs=pl.BlockSpec((None, 8, tn), lambda l, j: (l, 0, j)),
        out_shape=jax.ShapeDtypeStruct((nl, 8, n), _F32),
        compiler_params=_params(("parallel", "parallel")),
        name="adaln_modulation",
    )(c_pad, w, b)


def _rms_rows(x):
    return lax.rsqrt(jnp.mean(x * x, axis=1, keepdims=True) + NORM_EPS)


def _inproj_kernel(x_ref, mod_ref, gmix_ref, wT_ref, gq_ref, gk_ref, tab_ref,
                   qTa_ref, kA_ref, vTa_ref, qTb_ref, kB_ref, vTb_ref, knorm_ref, *, tk):
    ts = x_ref.shape[0]
    sh1 = mod_ref[0:1, :]
    sc1 = mod_ref[1:2, :]

    @pl.when(pl.program_id(1) == 0)
    def _():
        knorm_ref[...] = jnp.zeros(knorm_ref.shape, _F32)

    def head_norm(x3, g_ref):
        ms = jnp.mean(x3 * x3, axis=1, keepdims=True)
        return (x3 * lax.rsqrt(ms + NORM_EPS)) * g_ref[...][None]

    def max_key_norm2(k3):
        kf = k3.astype(_BF16).astype(_F32)
        return jnp.max(jnp.sum(kf * kf, axis=1), axis=1, keepdims=True)

    one_hot_rows = (lax.broadcasted_iota(jnp.int32, (HEAD_DIM, tk), 0) == 0).astype(_F32)

    def widened_keys_t(k3):
        parts = []
        for hh in range(k3.shape[0]):
            parts += [k3[hh], one_hot_rows]
        return jnp.concatenate(parts, axis=0).T.astype(_BF16)

    for c in range(ts // tk):
        tok = slice(c * tk, (c + 1) * tk)
        x = x_ref[tok, :]
        h = ((x * _rms_rows(x)) * gmix_ref[...]) * (1.0 + sc1) + sh1
        hbT = h.T.astype(_BF16)

        def proj_t(lo, n):
            return _dot(wT_ref[lo:lo + n, :], hbT)

        tab = tab_ref[:, tok]
        cos_r, sin_r, cos_c, sin_c = tab[0:16], tab[16:32], tab[32:48], tab[48:64]
        cos_p, sin_p = tab[64:72], tab[72:80]

        def axial(x3):
            r1, r2, c1, c2 = x3[:, 0:16], x3[:, 16:32], x3[:, 32:48], x3[:, 48:64]
            return jnp.concatenate([r1 * cos_r - r2 * sin_r, r1 * sin_r + r2 * cos_r,
                                    c1 * cos_c - c2 * sin_c, c1 * sin_c + c2 * cos_c], axis=1)

        def partial(x3):
            half = PARTIAL_ROT // 2
            x1, x2, rest = x3[:, 0:half], x3[:, half:PARTIAL_ROT], x3[:, PARTIAL_ROT:]
            return jnp.concatenate([x1 * cos_p - x2 * sin_p, x1 * sin_p + x2 * cos_p, rest], axis=1)

        qa = proj_t(_QA0, A_Q).reshape(A_HEADS, HEAD_DIM, tk)
        qa = axial(head_norm(qa, gq_ref)) * QK_SCALE
        qTa_ref[:, tok] = qa.reshape(A_Q, tk).astype(_BF16)

        ka = axial(head_norm(proj_t(_KA0, A_KV).reshape(A_KV_HEADS, HEAD_DIM, tk), gk_ref))
        kA_ref[tok, :] = widened_keys_t(ka)

        va = proj_t(_VA0, A_KV).astype(_BF16)
        for g in range(A_KV_HEADS):
            vTa_ref[g, c] = va[g * HEAD_DIM:(g + 1) * HEAD_DIM]

        qb = proj_t(_QB0, B_QK).reshape(2 * B_HEADS, HEAD_DIM, tk)
        qTb_ref[:, tok] = (partial(qb) * QK_SCALE).reshape(B_QK, tk).astype(_BF16)

        kb = partial(proj_t(_KB0, B_QK).reshape(2 * B_HEADS, HEAD_DIM, tk))
        kB_ref[tok, :] = widened_keys_t(kb)

        vb = proj_t(_VB0, B_V).astype(_BF16)
        for hd in range(B_HEADS):
            vTb_ref[hd, c] = vb[hd * B_V_DIM:(hd + 1) * B_V_DIM]

        lanes = knorm_ref.shape[1]
        b_rows = slice(_KNORM_B0, _KNORM_B0 + 2 * B_HEADS)
        a_rows = slice(_KNORM_A0, _KNORM_A0 + A_KV_HEADS)
        knorm_ref[b_rows, :] = jnp.maximum(knorm_ref[b_rows, :],
                                           jnp.broadcast_to(max_key_norm2(kb), (2 * B_HEADS, lanes)))
        knorm_ref[a_rows, :] = jnp.maximum(knorm_ref[a_rows, :],
                                           jnp.broadcast_to(max_key_norm2(ka), (A_KV_HEADS, lanes)))


def _inproj(x, mod, gmix, w_inT, gq, gk, tab, *, ts, tk):
    n, s, d = x.shape
    nck = s // tk
    cpt = ts // tk
    out_shape = (
        jax.ShapeDtypeStruct((n, A_Q, s), _BF16),
        jax.ShapeDtypeStruct((n, s, 2 * A_KV), _BF16),
        jax.ShapeDtypeStruct((n, A_KV_HEADS, nck, HEAD_DIM, tk), _BF16),
        jax.ShapeDtypeStruct((n, B_QK, s), _BF16),
        jax.ShapeDtypeStruct((n, s, 2 * B_QK), _BF16),
        jax.ShapeDtypeStruct((n, B_HEADS, nck, B_V_DIM, tk), _BF16),
        jax.ShapeDtypeStruct((n, _KNORM_ROWS, 128), _F32),
    )
    return pl.pallas_call(
        functools.partial(_inproj_kernel, tk=tk),
        grid=(n, s // ts),
        in_specs=[
            pl.BlockSpec((None, ts, d), lambda b, t: (b, t, 0)),
            pl.BlockSpec((None, 6, d), lambda b, t: (b, 0, 0)),
            pl.BlockSpec((1, d), lambda b, t: (0, 0)),
            pl.BlockSpec((IN_WIDTH, d), lambda b, t: (0, 0)),
            pl.BlockSpec((HEAD_DIM, 1), lambda b, t: (0, 0)),
            pl.BlockSpec((HEAD_DIM, 1), lambda b, t: (0, 0)),
            pl.BlockSpec((80, ts), lambda b, t: (0, t)),
        ],
        out_specs=(
            pl.BlockSpec((None, A_Q, ts), lambda b, t: (b, 0, t)),
            pl.BlockSpec((None, ts, 2 * A_KV), lambda b, t: (b, t, 0)),
            pl.BlockSpec((None, A_KV_HEADS, cpt, HEAD_DIM, tk), lambda b, t: (b, 0, t, 0, 0)),
            pl.BlockSpec((None, B_QK, ts), lambda b, t: (b, 0, t)),
            pl.BlockSpec((None, ts, 2 * B_QK), lambda b, t: (b, t, 0)),
            pl.BlockSpec((None, B_HEADS, cpt, B_V_DIM, tk), lambda b, t: (b, 0, t, 0, 0)),
            pl.BlockSpec((None, _KNORM_ROWS, 128), lambda b, t: (b, 0, 0)),
        ),
        out_shape=out_shape,
        compiler_params=_params(("parallel", "arbitrary")),
        name="mix_in_projection",
    )(x, mod, gmix, w_inT, gq, gk, tab)


def _softmax_value_loop(qT_ref, k_ref, vT_ref, knorm2_of_row, key_head_of_row, value_head_of_row, w_ref,
                        s_even_ref, s_odd_ref, m_ref, l_ref, acc_ref, *, qrows, tq, tk):
    nck = vT_ref.shape[1]
    nchains = qrows * (qT_ref.shape[-1] // tq)
    knorm2_of_chain = lambda c: knorm2_of_row(c % qrows)
    key_head_of_chain = lambda c: key_head_of_row(c % qrows)
    value_head_of_chain = lambda c: value_head_of_row(c % qrows)
    acc_ref[...] = jnp.zeros(acc_ref.shape, _F32)
    l_ref[...] = jnp.zeros(l_ref.shape, _F32)

    qs = [qT_ref[(c % qrows) * HEAD_DIM:(c % qrows + 1) * HEAD_DIM, (c // qrows) * tq:(c // qrows + 1) * tq]
          for c in range(nchains)]
    bounds = []
    for c in range(nchains):
        qf = qs[c].astype(_F32)
        bounds.append(jnp.sqrt(jnp.sum(qf * qf, axis=0, keepdims=True) * knorm2_of_chain(c)))
    use_bound = jnp.max(jnp.concatenate(bounds, axis=0)) <= FAST_SOFTMAX_MAX_BOUND

    first_row = lax.broadcasted_iota(jnp.int32, (OFFSET_ROWS, tq), 0) == 0
    for c in range(nchains):
        offset = -jnp.minimum(bounds[c], FAST_SOFTMAX_MAX_BOUND)
        offset_rows = jnp.where(first_row, jnp.broadcast_to(offset, (OFFSET_ROWS, tq)), 0.0)
        w_ref[c] = jnp.concatenate([qs[c], offset_rows.astype(_BF16)], axis=0)

    def k_chunk(j, c):
        lane0 = key_head_of_chain(c) * 2 * HEAD_DIM
        return k_ref[pl.ds(pl.multiple_of(j * tk, tk), tk), lane0:lane0 + HEAD_DIM + OFFSET_ROWS]

    def partial_sums(p):
        return jnp.sum(p.reshape(tk // 8, 8, tq), axis=0)

    assert nchains >= FAST_LOOKAHEAD
    for a in range(FAST_LOOKAHEAD):
        s_even_ref[a] = _dot(k_chunk(0, a), w_ref[a])

    @pl.when(use_bound)
    def _():
        def scores(j, step):
            dj, c = divmod(step, nchains)
            jj = j + dj
            if dj and not isinstance(jj, int):
                jj = jnp.where(jj >= nck, jj - nck, jj)
            return _dot(k_chunk(jj, c), w_ref[c])

        chunks_per_trip = min(FAST_STEPS_PER_TRIP // nchains, nck)
        assert nck % chunks_per_trip == 0

        single_trip = chunks_per_trip == nck

        def body(i, carry):
            j0 = i * chunks_per_trip
            pending = [s_even_ref[a] for a in range(FAST_LOOKAHEAD)]
            for u in range(chunks_per_trip):
                for c in range(nchains):
                    ahead = u * nchains + c + FAST_LOOKAHEAD
                    if not (single_trip and ahead >= nck * nchains):
                        pending.append(scores(j0, ahead))
                    p = jnp.exp2(pending.pop(0))
                    l_ref[c] += partial_sums(p)
                    acc_ref[c] += _dot(vT_ref[value_head_of_chain(c), j0 + u], p.astype(_BF16))
            for a, tile in enumerate(pending):
                s_even_ref[a] = tile
            return carry

        if single_trip:
            body(0, 0)
        else:
            lax.fori_loop(0, nck // chunks_per_trip, body, 0)

    @pl.when(jnp.logical_not(use_bound))
    def _():
        m_ref[...] = jnp.full(m_ref.shape, -jnp.inf, _F32)

        def phase(j_cur, j_next, s_cur_ref, s_next_ref):
            for c in range(nchains):
                vc = vT_ref[value_head_of_chain(c), j_cur]
                s = s_cur_ref[c]
                s_next_ref[c] = _dot(k_chunk(j_next, c), w_ref[c])
                m_prev = m_ref[c]
                m_new = jnp.maximum(m_prev, jnp.max(s, axis=0, keepdims=True))
                alpha = jnp.exp2(m_prev - m_new)
                p = jnp.exp2(s - m_new)
                l_ref[c] = alpha * l_ref[c] + partial_sums(p)
                acc_ref[c] = alpha * acc_ref[c] + _dot(vc, p.astype(_BF16))
                m_ref[c] = m_new

        for c in range(FAST_LOOKAHEAD, nchains):
            s_even_ref[c] = _dot(k_chunk(0, c), w_ref[c])

        def body(i, carry):
            j = 2 * i
            phase(j, j + 1, s_even_ref, s_odd_ref)
            phase(j + 1, jnp.where(j + 2 == nck, 0, j + 2), s_odd_ref, s_even_ref)
            return carry

        lax.fori_loop(0, nck // 2, body, 0)


def _denominator(l_ref, c):
    return jnp.sum(l_ref[c], axis=0, keepdims=True)


def _attn_a_kernel(qT_ref, k_ref, vT_ref, knorm_ref, o_ref, w_ref, s_even_ref, s_odd_ref, m_ref, l_ref, acc_ref,
                   *, tq, tk):
    knorm2 = knorm_ref[_KNORM_A0:_KNORM_A0 + A_KV_HEADS, 0:1]
    kv_head = lambda r: r // A_GROUP
    _softmax_value_loop(qT_ref, k_ref, vT_ref, lambda r: knorm2[kv_head(r):kv_head(r) + 1], kv_head, kv_head,
                        w_ref, s_even_ref, s_odd_ref, m_ref, l_ref, acc_ref, qrows=A_HEADS, tq=tq, tk=tk)
    for c in range(acc_ref.shape[0]):
        blk, r = divmod(c, A_HEADS)
        o = acc_ref[c] / _denominator(l_ref, c)
        o_ref[r * HEAD_DIM:(r + 1) * HEAD_DIM, blk * tq:(blk + 1) * tq] = o.astype(_BF16)


def _attention_scratch(nchains, dv, tq, tk):
    return [
        pltpu.VMEM((nchains, HEAD_DIM + OFFSET_ROWS, tq), _BF16),
        pltpu.VMEM((max(nchains, FAST_LOOKAHEAD), tk, tq), _F32),
        pltpu.VMEM((nchains, tk, tq), _F32),
        pltpu.VMEM((nchains, 1, tq), _F32),
        pltpu.VMEM((nchains, 8, tq), _F32),
        pltpu.VMEM((nchains, dv, tq), _F32),
    ]


def _attn_a(qTa, kA, vTa, knorm, *, tq):
    n, _, s = qTa.shape
    _, nkv, nck, dv, tk = vTa.shape
    nblk = A_QUERY_BLOCKS_PER_STEP
    return pl.pallas_call(
        functools.partial(_attn_a_kernel, tq=tq, tk=tk),
        grid=(n, s // (nblk * tq)),
        in_specs=[
            pl.BlockSpec((None, A_Q, nblk * tq), lambda b, i: (b, 0, i)),
            pl.BlockSpec((None, s, nkv * 2 * HEAD_DIM), lambda b, i: (b, 0, 0)),
            pl.BlockSpec((None, nkv, nck, dv, tk), lambda b, i: (b, 0, 0, 0, 0)),
            pl.BlockSpec((None, _KNORM_ROWS, 128), lambda b, i: (b, 0, 0)),
        ],
        out_specs=pl.BlockSpec((None, A_Q, nblk * tq), lambda b, i: (b, 0, i)),
        out_shape=jax.ShapeDtypeStruct((n, A_Q, s), _BF16),
        scratch_shapes=_attention_scratch(nblk * A_HEADS, HEAD_DIM, tq, tk),
        compiler_params=_params(("parallel", "parallel")),
        name="gqa_attention",
    )(qTa, kA, vTa, knorm)


def _attn_b_kernel(qT_ref, k_ref, vT_ref, knorm_ref, lam_ref, gsub_ref, o_ref, w_ref, s_even_ref, s_odd_ref,
                   m_ref, l_ref, acc_ref, *, tq, tk, lam_init):
    qrows = 2 * B_HEADS_PER_STEP
    knorm2 = knorm_ref[pl.ds(_KNORM_B0 + qrows * pl.program_id(1), qrows), 0:1]
    _softmax_value_loop(qT_ref, k_ref, vT_ref, lambda r: knorm2[r:r + 1], lambda r: r, lambda r: r // 2,
                        w_ref, s_even_ref, s_odd_ref, m_ref, l_ref, acc_ref, qrows=qrows, tq=tq, tk=tk)

    lp = lam_ref[...]
    lam = (jnp.exp(jnp.sum(lp[0:1] * lp[1:2], axis=1, keepdims=True))
           - jnp.exp(jnp.sum(lp[2:3] * lp[3:4], axis=1, keepdims=True)) + lam_init)
    for blk in range(acc_ref.shape[0] // qrows):
        for h in range(B_HEADS_PER_STEP):
            c1 = blk * qrows + 2 * h
            o = acc_ref[c1] / _denominator(l_ref, c1) - lam * (acc_ref[c1 + 1] / _denominator(l_ref, c1 + 1))
            ms = jnp.mean(o * o, axis=0, keepdims=True)
            o = ((o * lax.rsqrt(ms + NORM_EPS)) * gsub_ref[...]) * (1.0 - lam_init)
            o_ref[h * B_V_DIM:(h + 1) * B_V_DIM, blk * tq:(blk + 1) * tq] = o.astype(_BF16)


def _attn_b(qTb, kB, vTb, knorm, lam_params, gsub, *, tq, lam_init):
    n, _, s = qTb.shape
    _, _, nck, dv, tk = vTb.shape
    hps = B_HEADS_PER_STEP
    nblk = B_QUERY_BLOCKS_PER_STEP
    return pl.pallas_call(
        functools.partial(_attn_b_kernel, tq=tq, tk=tk, lam_init=lam_init),
        grid=(n, B_HEADS // hps, s // (nblk * tq)),
        in_specs=[
            pl.BlockSpec((None, hps * 2 * HEAD_DIM, nblk * tq), lambda b, hp, i: (b, hp, i)),
            pl.BlockSpec((None, s, hps * 4 * HEAD_DIM), lambda b, hp, i: (b, 0, hp)),
            pl.BlockSpec((None, hps, nck, dv, tk), lambda b, hp, i: (b, hp, 0, 0, 0)),
            pl.BlockSpec((None, _KNORM_ROWS, 128), lambda b, hp, i: (b, 0, 0)),
            pl.BlockSpec((4, HEAD_DIM), lambda b, hp, i: (0, 0)),
            pl.BlockSpec((B_V_DIM, 1), lambda b, hp, i: (0, 0)),
        ],
        out_specs=pl.BlockSpec((None, hps * B_V_DIM, nblk * tq), lambda b, hp, i: (b, hp, i)),
        out_shape=jax.ShapeDtypeStruct((n, B_V, s), _BF16),
        scratch_shapes=_attention_scratch(nblk * 2 * hps, B_V_DIM, tq, tk),
        compiler_params=_params(("parallel", "parallel", "parallel")),
        name="diff_attention",
    )(qTb, kB, vTb, knorm, lam_params, gsub)


def _mix_and_ffn(x_ref, oTa_ref, oTb_ref, mod_ref, wout_ref, gffn_ref, wgu_ref, wdn_ref, act_ref):
    mixed = _dot_tn(oTa_ref[...], wout_ref[0:A_Q, :]) + _dot_tn(oTb_ref[...], wout_ref[A_Q:, :])
    gt1, sh2, sc2, gt2 = mod_ref[2:3, :], mod_ref[3:4, :], mod_ref[4:5, :], mod_ref[5:6, :]
    x1 = x_ref[...] + gt1 * mixed
    h = ((x1 * _rms_rows(x1)) * gffn_ref[...]) * (1.0 + sc2) + sh2
    hb = h.astype(_BF16)
    dff = wdn_ref.shape[0]
    for c in range(dff // FF_CHUNK):
        cols = slice(c * FF_CHUNK, (c + 1) * FF_CHUNK)
        gate = _dot(hb, wgu_ref[:, cols])
        up = _dot(hb, wgu_ref[:, dff + c * FF_CHUNK:dff + (c + 1) * FF_CHUNK])
        act_ref[:, cols] = ((gate * jax.nn.sigmoid(gate)) * up).astype(_BF16)
    return x1 + gt2 * _dot(act_ref[...], wdn_ref[...])


def _ffn_kernel(x_ref, oTa_ref, oTb_ref, mod_ref, wout_ref, gffn_ref, wgu_ref, wdn_ref, y_ref, act_ref):
    y_ref[...] = _mix_and_ffn(x_ref, oTa_ref, oTb_ref, mod_ref, wout_ref, gffn_ref, wgu_ref, wdn_ref, act_ref)


def _ffn_final_kernel(x_ref, oTa_ref, oTb_ref, mod_ref, wout_ref, gffn_ref, wgu_ref, wdn_ref, modf_ref, gfin_ref,
                      y_ref, act_ref):
    x = _mix_and_ffn(x_ref, oTa_ref, oTb_ref, mod_ref, wout_ref, gffn_ref, wgu_ref, wdn_ref, act_ref)
    shift, scale = modf_ref[0:1, :], modf_ref[1:2, :]
    y_ref[...] = ((x * _rms_rows(x)) * gfin_ref[...]) * (1.0 + scale) + shift


def _resident(shape):
    return pl.BlockSpec(shape, lambda b, t: (0,) * len(shape), pipeline_mode=pl.Buffered(1))


def _out_ffn(x, oTa, oTb, mod, w_out, gffn, w_gu, w_dn, final=None, *, ts):
    n, s, d = x.shape
    dff = w_dn.shape[0]
    assert dff % FF_CHUNK == 0 and w_gu.shape == (d, 2 * dff)
    in_specs = [
        pl.BlockSpec((None, ts, d), lambda b, t: (b, t, 0)),
        pl.BlockSpec((None, A_Q, ts), lambda b, t: (b, 0, t)),
        pl.BlockSpec((None, B_V, ts), lambda b, t: (b, 0, t)),
        pl.BlockSpec((None, 6, d), lambda b, t: (b, 0, 0)),
        _resident((A_Q + B_V, d)),
        pl.BlockSpec((1, d), lambda b, t: (0, 0)),
        _resident((d, 2 * dff)),
        _resident((dff, d)),
    ]
    operands = (x, oTa, oTb, mod, w_out, gffn, w_gu, w_dn)
    if final is not None:
        in_specs += [pl.BlockSpec((None, 2, d), lambda b, t: (b, 0, 0)), pl.BlockSpec((1, d), lambda b, t: (0, 0))]
        operands += tuple(final)
    return pl.pallas_call(
        _ffn_kernel if final is None else _ffn_final_kernel,
        grid=(n, s // ts),
        in_specs=in_specs,
        out_specs=pl.BlockSpec((None, ts, d), lambda b, t: (b, t, 0)),
        out_shape=jax.ShapeDtypeStruct((n, s, d), _F32),
        scratch_shapes=[pltpu.VMEM((ts, dff), _BF16)],
        compiler_params=_params(("parallel", "parallel")),
        name="out_proj_swiglu",
    )(*operands)


def _rope_table_t(s):
    def tables(pos, dim, theta):
        inv = 1.0 / (theta ** (jnp.arange(0, dim, 2, dtype=_F32) / dim))
        ang = pos.astype(_F32)[:, None] * inv[None, :]
        return jnp.cos(ang).T, jnp.sin(ang).T

    t = jnp.arange(s)
    cos_r, sin_r = tables(t // GRID_W, HEAD_DIM // 2, AXIAL_THETA)
    cos_c, sin_c = tables(t % GRID_W, HEAD_DIM // 2, AXIAL_THETA)
    cos_p, sin_p = tables(t, PARTIAL_ROT, ROPE_THETA)
    return jnp.concatenate([cos_r, sin_r, cos_c, sin_c, cos_p, sin_p], axis=0)


def _trunks(xs, cs, w_ada, b_ada, g_mix, w_in, g_q_a, g_k_a, lam_q1, lam_k1, lam_q2, lam_k2, g_subln,
            w_out, g_ffn, w_gate_up, w_down, w_ada_final, b_ada_final, g_final, *, ts, tq, tk):
    d = xs[0].shape[-1]
    depth = w_ada.shape[0]

    rows = [c.shape[0] for c in cs]
    c_pad = jnp.zeros((8, d), _F32).at[:sum(rows)].set(jnp.concatenate(cs, axis=0))
    mod_all = _modulation(c_pad, w_ada, b_ada[:, None, :])
    modf_all = _modulation(c_pad, w_ada_final[None], b_ada_final[None, None, :])[0]

    w_inT = jnp.swapaxes(w_in, 1, 2).astype(_BF16)
    w_out_b = w_out.astype(_BF16)
    w_gu = w_gate_up.astype(_BF16)
    w_dn = w_down.astype(_BF16)
    lam_params = jnp.stack([lam_q1, lam_k1, lam_q2, lam_k2], axis=1)

    outs = []
    row0 = 0
    for x, n in zip(xs, rows):
        s = x.shape[1]
        mod = mod_all[:, row0:row0 + n].reshape(depth, n, 6, d)
        modf = modf_all[row0:row0 + n].reshape(n, 2, d)
        row0 += n
        tab = _rope_table_t(s)
        for l in range(depth):
            qTa, kA, vTa, qTb, kB, vTb, knorm = _inproj(
                x, mod[l], g_mix[l][None, :], w_inT[l], g_q_a[l][:, None], g_k_a[l][:, None], tab, ts=ts, tk=tk)
            oTa = _attn_a(qTa, kA, vTa, knorm, tq=tq)
            lam_init = 0.8 - 0.6 * math.exp(-0.3 * l)
            oTb = _attn_b(qTb, kB, vTb, knorm, lam_params[l], g_subln[l][:, None], tq=tq, lam_init=lam_init)
            final = (modf, g_final[None, :]) if l == depth - 1 else None
            x = _out_ffn(x, oTa, oTb, mod[l], w_out_b[l], g_ffn[l][None, :], w_gu[l], w_dn[l], final, ts=ts)
        outs.append(x)
    return tuple(outs)


def _trunk(x, c, *weights, ts, tq, tk):
    return _trunks((x,), (c,), *weights, ts=ts, tq=tq, tk=tk)[0]


def kernel(x_prompt, x_sample, c_prompt, c_sample, w_ada, b_ada, g_mix, w_in, g_q_a, g_k_a, lam_q1, lam_k1, lam_q2, lam_k2, g_subln, w_out, g_ffn, w_gate_up, w_down, w_ada_final, b_ada_final, g_final):
    return _trunks((x_prompt, x_sample), (c_prompt, c_sample), w_ada, b_ada, g_mix, w_in, g_q_a, g_k_a,
                   lam_q1, lam_k1, lam_q2, lam_k2, g_subln, w_out, g_ffn, w_gate_up, w_down,
                   w_ada_final, b_ada_final, g_final, ts=512, tq=256, tk=256)
```

```python
import functools
import math

import jax
import jax.numpy as jnp
from jax import lax
from jax.experimental import pallas as pl
from jax.experimental.pallas import tpu as pltpu

HEAD_DIM = 64
A_HEADS = 8
A_KV_HEADS = 2
A_GROUP = A_HEADS // A_KV_HEADS
B_HEADS = 4
B_V_DIM = 2 * HEAD_DIM
A_Q = A_HEADS * HEAD_DIM
A_KV = A_KV_HEADS * HEAD_DIM
B_QK = B_HEADS * 2 * HEAD_DIM
B_V = B_HEADS * B_V_DIM
IN_WIDTH = A_Q + 2 * A_KV + 2 * B_QK + B_V
GRID_W = 64
AXIAL_THETA = 10000.0
ROPE_THETA = 500000.0
PARTIAL_ROT = HEAD_DIM // 4
NORM_EPS = 1e-6
QK_SCALE = HEAD_DIM ** -0.5 * math.log2(math.e)

_QA0, _KA0, _VA0 = 0, A_Q, A_Q + A_KV
_QB0 = A_Q + 2 * A_KV
_KB0 = _QB0 + B_QK
_VB0 = _KB0 + B_QK

_KNORM_B0, _KNORM_A0, _KNORM_ROWS = 0, 2 * B_HEADS, 16
FAST_SOFTMAX_MAX_BOUND = 48.0
OFFSET_ROWS = 16
A_QUERY_BLOCKS_PER_STEP = 2
B_HEADS_PER_STEP = 1
B_QUERY_BLOCKS_PER_STEP = 8
FAST_STEPS_PER_TRIP = 256
FAST_LOOKAHEAD = 4

FFN_NORM_SUBTILE = 128
FF_CHUNK = 256
V7X_VMEM_LIMIT_BYTES = 56 * 1024 * 1024

_BF16 = jnp.bfloat16
_F32 = jnp.float32


def _params(semantics):
    return pltpu.CompilerParams(dimension_semantics=semantics, vmem_limit_bytes=V7X_VMEM_LIMIT_BYTES)


def _dot(a, b):
    return jnp.dot(a, b, preferred_element_type=_F32)


def _dot_nt(a, b):
    return lax.dot_general(a, b, (((1,), (1,)), ((), ())), preferred_element_type=_F32)


def _dot_tn(a, b):
    return lax.dot_general(a, b, (((0,), (0,)), ((), ())), preferred_element_type=_F32)


def _split_bf16(x):
    hi = x.astype(_BF16)
    lo = (x - hi.astype(_F32)).astype(_BF16)
    return hi, lo


def _mod_kernel(c_ref, w_ref, b_ref, o_ref):
    c = c_ref[...]
    c_act = c * jax.nn.sigmoid(c)
    c_hi, c_lo = _split_bf16(c_act)
    w_hi, w_lo = _split_bf16(w_ref[...])
    acc = _dot(c_hi, w_hi) + (_dot(c_lo, w_hi) + _dot(c_hi, w_lo))
    o_ref[...] = acc + b_ref[...]


def _modulation(c_pad, w, b, tn=1024):
    nl, d, n = w.shape
    return pl.pallas_call(
        _mod_kernel,
        grid=(nl, n // tn),
        in_specs=[
            pl.BlockSpec((8, d), lambda l, j: (0, 0)),
            pl.BlockSpec((None, d, tn), lambda l, j: (l, 0, j)),
            pl.BlockSpec((None, 1, tn), lambda l, j: (l, 0, j)),
        ],
        out_specs=pl.BlockSpec((None, 8, tn), lambda l, j: (l, 0, j)),
        out_shape=jax.ShapeDtypeStruct((nl, 8, n), _F32),
        compiler_params=_params(("parallel", "parallel")),
        name="adaln_modulation",
    )(c_pad, w, b)


def _rms_rows(x):
    return lax.rsqrt(jnp.mean(x * x, axis=1, keepdims=True) + NORM_EPS)


def _inproj_kernel(x_ref, mod_ref, gmix_ref, wT_ref, gq_ref, gk_ref, tab_ref,
                   qTa_ref, kA_ref, vTa_ref, qTb_ref, kB_ref, vTb_ref, knorm_ref, *, tk):
    ts = x_ref.shape[0]
    sh1 = mod_ref[0:1, :]
    sc1 = mod_ref[1:2, :]

    @pl.when(pl.program_id(1) == 0)
    def _():
        knorm_ref[...] = jnp.zeros(knorm_ref.shape, _F32)

    def head_norm(x3, g_ref):
        ms = jnp.mean(x3 * x3, axis=1, keepdims=True)
        return (x3 * lax.rsqrt(ms + NORM_EPS)) * g_ref[...][None]

    def max_key_norm2(k3):
        kf = k3.astype(_BF16).astype(_F32)
        return jnp.max(jnp.sum(kf * kf, axis=1), axis=1, keepdims=True)

    one_hot_rows = (lax.broadcasted_iota(jnp.int32, (HEAD_DIM, tk), 0) == 0).astype(_F32)

    def widened_keys_t(k3):
        parts = []
        for hh in range(k3.shape[0]):
            parts += [k3[hh], one_hot_rows]
        return jnp.concatenate(parts, axis=0).T.astype(_BF16)

    for c in range(ts // tk):
        tok = slice(c * tk, (c + 1) * tk)
        x = x_ref[tok, :]
        h = ((x * _rms_rows(x)) * gmix_ref[...]) * (1.0 + sc1) + sh1
        hbT = h.T.astype(_BF16)

        def proj_t(lo, n):
            return _dot(wT_ref[lo:lo + n, :], hbT)

        tab = tab_ref[:, tok]
        cos_r, sin_r, cos_c, sin_c = tab[0:16], tab[16:32], tab[32:48], tab[48:64]
        cos_p, sin_p = tab[64:72], tab[72:80]

        def axial(x3):
            r1, r2, c1, c2 = x3[:, 0:16], x3[:, 16:32], x3[:, 32:48], x3[:, 48:64]
            return jnp.concatenate([r1 * cos_r - r2 * sin_r, r1 * sin_r + r2 * cos_r,
                                    c1 * cos_c - c2 * sin_c, c1 * sin_c + c2 * cos_c], axis=1)

        def partial(x3):
            half = PARTIAL_ROT // 2
            x1, x2, rest = x3[:, 0:half], x3[:, half:PARTIAL_ROT], x3[:, PARTIAL_ROT:]
            return jnp.concatenate([x1 * cos_p - x2 * sin_p, x1 * sin_p + x2 * cos_p, rest], axis=1)

        qa = proj_t(_QA0, A_Q).reshape(A_HEADS, HEAD_DIM, tk)
        qa = axial(head_norm(qa, gq_ref)) * QK_SCALE
        qTa_ref[:, tok] = qa.reshape(A_Q, tk).astype(_BF16)

        ka = axial(head_norm(proj_t(_KA0, A_KV).reshape(A_KV_HEADS, HEAD_DIM, tk), gk_ref))
        kA_ref[tok, :] = widened_keys_t(ka)

        va = proj_t(_VA0, A_KV).astype(_BF16)
        for g in range(A_KV_HEADS):
            vTa_ref[g, c] = va[g * HEAD_DIM:(g + 1) * HEAD_DIM]

        qb = proj_t(_QB0, B_QK).reshape(2 * B_HEADS, HEAD_DIM, tk)
        qTb_ref[:, tok] = (partial(qb) * QK_SCALE).reshape(B_QK, tk).astype(_BF16)

        kb = partial(proj_t(_KB0, B_QK).reshape(2 * B_HEADS, HEAD_DIM, tk))
        kB_ref[tok, :] = widened_keys_t(kb)

        vb = proj_t(_VB0, B_V).astype(_BF16)
        for hd in range(B_HEADS):
            vTb_ref[hd, c] = vb[hd * B_V_DIM:(hd + 1) * B_V_DIM]

        lanes = knorm_ref.shape[1]
        b_rows = slice(_KNORM_B0, _KNORM_B0 + 2 * B_HEADS)
        a_rows = slice(_KNORM_A0, _KNORM_A0 + A_KV_HEADS)
        knorm_ref[b_rows, :] = jnp.maximum(knorm_ref[b_rows, :],
                                           jnp.broadcast_to(max_key_norm2(kb), (2 * B_HEADS, lanes)))
        knorm_ref[a_rows, :] = jnp.maximum(knorm_ref[a_rows, :],
                                           jnp.broadcast_to(max_key_norm2(ka), (A_KV_HEADS, lanes)))


def _inproj(x, mod, gmix, w_inT, gq, gk, tab, *, ts, tk):
    n, s, d = x.shape
    nck = s // tk
    cpt = ts // tk
    out_shape = (
        jax.ShapeDtypeStruct((n, A_Q, s), _BF16),
        jax.ShapeDtypeStruct((n, s, 2 * A_KV), _BF16),
        jax.ShapeDtypeStruct((n, A_KV_HEADS, nck, HEAD_DIM, tk), _BF16),
        jax.ShapeDtypeStruct((n, B_QK, s), _BF16),
        jax.ShapeDtypeStruct((n, s, 2 * B_QK), _BF16),
        jax.ShapeDtypeStruct((n, B_HEADS, nck, B_V_DIM, tk), _BF16),
        jax.ShapeDtypeStruct((n, _KNORM_ROWS, 128), _F32),
    )
    return pl.pallas_call(
        functools.partial(_inproj_kernel, tk=tk),
        grid=(n, s // ts),
        in_specs=[
            pl.BlockSpec((None, ts, d), lambda b, t: (b, t, 0)),
            pl.BlockSpec((None, 6, d), lambda b, t: (b, 0, 0)),
            pl.BlockSpec((1, d), lambda b, t: (0, 0)),
            pl.BlockSpec((IN_WIDTH, d), lambda b, t: (0, 0)),
            pl.BlockSpec((HEAD_DIM, 1), lambda b, t: (0, 0)),
            pl.BlockSpec((HEAD_DIM, 1), lambda b, t: (0, 0)),
            pl.BlockSpec((80, ts), lambda b, t: (0, t)),
        ],
        out_specs=(
            pl.BlockSpec((None, A_Q, ts), lambda b, t: (b, 0, t)),
            pl.BlockSpec((None, ts, 2 * A_KV), lambda b, t: (b, t, 0)),
            pl.BlockSpec((None, A_KV_HEADS, cpt, HEAD_DIM, tk), lambda b, t: (b, 0, t, 0, 0)),
            pl.BlockSpec((None, B_QK, ts), lambda b, t: (b, 0, t)),
            pl.BlockSpec((None, ts, 2 * B_QK), lambda b, t: (b, t, 0)),
            pl.BlockSpec((None, B_HEADS, cpt, B_V_DIM, tk), lambda b, t: (b, 0, t, 0, 0)),
            pl.BlockSpec((None, _KNORM_ROWS, 128), lambda b, t: (b, 0, 0)),
        ),
        out_shape=out_shape,
        compiler_params=_params(("parallel", "arbitrary")),
        name="mix_in_projection",
    )(x, mod, gmix, w_inT, gq, gk, tab)


def _softmax_value_loop(qT_ref, k_ref, vT_ref, knorm2_of_row, key_head_of_row, value_head_of_row, w_ref,
                        s_even_ref, s_odd_ref, m_ref, l_ref, acc_ref, *, qrows, tq, tk):
    nck = vT_ref.shape[1]
    nchains = qrows * (qT_ref.shape[-1] // tq)
    knorm2_of_chain = lambda c: knorm2_of_row(c % qrows)
    key_head_of_chain = lambda c: key_head_of_row(c % qrows)
    value_head_of_chain = lambda c: value_head_of_row(c % qrows)
    acc_ref[...] = jnp.zeros(acc_ref.shape, _F32)
    l_ref[...] = jnp.zeros(l_ref.shape, _F32)

    qs = [qT_ref[(c % qrows) * HEAD_DIM:(c % qrows + 1) * HEAD_DIM, (c // qrows) * tq:(c // qrows + 1) * tq]
          for c in range(nchains)]
    bounds = []
    for c in range(nchains):
        qf = qs[c].astype(_F32)
        bounds.append(jnp.sqrt(jnp.sum(qf * qf, axis=0, keepdims=True) * knorm2_of_chain(c)))
    use_bound = jnp.max(jnp.concatenate(bounds, axis=0)) <= FAST_SOFTMAX_MAX_BOUND

    first_row = lax.broadcasted_iota(jnp.int32, (OFFSET_ROWS, tq), 0) == 0
    for c in range(nchains):
        offset = -jnp.minimum(bounds[c], FAST_SOFTMAX_MAX_BOUND)
        offset_rows = jnp.where(first_row, jnp.broadcast_to(offset, (OFFSET_ROWS, tq)), 0.0)
        w_ref[c] = jnp.concatenate([qs[c], offset_rows.astype(_BF16)], axis=0)

    def k_chunk(j, c):
        lane0 = key_head_of_chain(c) * 2 * HEAD_DIM
        return k_ref[pl.ds(pl.multiple_of(j * tk, tk), tk), lane0:lane0 + HEAD_DIM + OFFSET_ROWS]

    def partial_sums(p):
        return jnp.sum(p.reshape(tk // 8, 8, tq), axis=0)

    assert nchains >= FAST_LOOKAHEAD
    for a in range(FAST_LOOKAHEAD):
        s_even_ref[a] = _dot(k_chunk(0, a), w_ref[a])

    @pl.when(use_bound)
    def _():
        def scores(j, step):
            dj, c = divmod(step, nchains)
            jj = j + dj
            if dj and not isinstance(jj, int):
                jj = jnp.where(jj >= nck, jj - nck, jj)
            return _dot(k_chunk(jj, c), w_ref[c])

        chunks_per_trip = min(FAST_STEPS_PER_TRIP // nchains, nck)
        assert nck % chunks_per_trip == 0

        single_trip = chunks_per_trip == nck

        def body(i, carry):
            j0 = i * chunks_per_trip
            pending = [s_even_ref[a] for a in range(FAST_LOOKAHEAD)]
            for u in range(chunks_per_trip):
                for c in range(nchains):
                    ahead = u * nchains + c + FAST_LOOKAHEAD
                    if not (single_trip and ahead >= nck * nchains):
                        pending.append(scores(j0, ahead))
                    p = jnp.exp2(pending.pop(0))
                    l_ref[c] += partial_sums(p)
                    acc_ref[c] += _dot(vT_ref[value_head_of_chain(c), j0 + u], p.astype(_BF16))
            for a, tile in enumerate(pending):
                s_even_ref[a] = tile
            return carry

        if single_trip:
            body(0, 0)
        else:
            lax.fori_loop(0, nck // chunks_per_trip, body, 0)

    @pl.when(jnp.logical_not(use_bound))
    def _():
        m_ref[...] = jnp.full(m_ref.shape, -jnp.inf, _F32)

        def phase(j_cur, j_next, s_cur_ref, s_next_ref):
            for c in range(nchains):
                vc = vT_ref[value_head_of_chain(c), j_cur]
                s = s_cur_ref[c]
                s_next_ref[c] = _dot(k_chunk(j_next, c), w_ref[c])
                m_prev = m_ref[c]
                m_new = jnp.maximum(m_prev, jnp.max(s, axis=0, keepdims=True))
                alpha = jnp.exp2(m_prev - m_new)
                p = jnp.exp2(s - m_new)
                l_ref[c] = alpha * l_ref[c] + partial_sums(p)
                acc_ref[c] = alpha * acc_ref[c] + _dot(vc, p.astype(_BF16))
                m_ref[c] = m_new

        for c in range(FAST_LOOKAHEAD, nchains):
            s_even_ref[c] = _dot(k_chunk(0, c), w_ref[c])

        def body(i, carry):
            j = 2 * i
            phase(j, j + 1, s_even_ref, s_odd_ref)
            phase(j + 1, jnp.where(j + 2 == nck, 0, j + 2), s_odd_ref, s_even_ref)
            return carry

        lax.fori_loop(0, nck // 2, body, 0)


def _denominator(l_ref, c):
    return jnp.sum(l_ref[c], axis=0, keepdims=True)


def _attn_a_kernel(qT_ref, k_ref, vT_ref, knorm_ref, o_ref, w_ref, s_even_ref, s_odd_ref, m_ref, l_ref, acc_ref,
                   *, tq, tk):
    knorm2 = knorm_ref[_KNORM_A0:_KNORM_A0 + A_KV_HEADS, 0:1]
    kv_head = lambda r: r // A_GROUP
    _softmax_value_loop(qT_ref, k_ref, vT_ref, lambda r: knorm2[kv_head(r):kv_head(r) + 1], kv_head, kv_head,
                        w_ref, s_even_ref, s_odd_ref, m_ref, l_ref, acc_ref, qrows=A_HEADS, tq=tq, tk=tk)
    for c in range(acc_ref.shape[0]):
        blk, r = divmod(c, A_HEADS)
        o = acc_ref[c] / _denominator(l_ref, c)
        o_ref[r * HEAD_DIM:(r + 1) * HEAD_DIM, blk * tq:(blk + 1) * tq] = o.astype(_BF16)


def _attention_scratch(nchains, dv, tq, tk):
    return [
        pltpu.VMEM((nchains, HEAD_DIM + OFFSET_ROWS, tq), _BF16),
        pltpu.VMEM((max(nchains, FAST_LOOKAHEAD), tk, tq), _F32),
        pltpu.VMEM((nchains, tk, tq), _F32),
        pltpu.VMEM((nchains, 1, tq), _F32),
        pltpu.VMEM((nchains, 8, tq), _F32),
        pltpu.VMEM((nchains, dv, tq), _F32),
    ]


def _attn_a(qTa, kA, vTa, knorm, *, tq):
    n, _, s = qTa.shape
    _, nkv, nck, dv, tk = vTa.shape
    nblk = A_QUERY_BLOCKS_PER_STEP
    return pl.pallas_call(
        functools.partial(_attn_a_kernel, tq=tq, tk=tk),
        grid=(n, s // (nblk * tq)),
        in_specs=[
            pl.BlockSpec((None, A_Q, nblk * tq), lambda b, i: (b, 0, i)),
            pl.BlockSpec((None, s, nkv * 2 * HEAD_DIM), lambda b, i: (b, 0, 0)),
            pl.BlockSpec((None, nkv, nck, dv, tk), lambda b, i: (b, 0, 0, 0, 0)),
            pl.BlockSpec((None, _KNORM_ROWS, 128), lambda b, i: (b, 0, 0)),
        ],
        out_specs=pl.BlockSpec((None, A_Q, nblk * tq), lambda b, i: (b, 0, i)),
        out_shape=jax.ShapeDtypeStruct((n, A_Q, s), _BF16),
        scratch_shapes=_attention_scratch(nblk * A_HEADS, HEAD_DIM, tq, tk),
        compiler_params=_params(("parallel", "parallel")),
        name="gqa_attention",
    )(qTa, kA, vTa, knorm)


def _attn_b_kernel(qT_ref, k_ref, vT_ref, knorm_ref, lam_ref, gsub_ref, o_ref, w_ref, s_even_ref, s_odd_ref,
                   m_ref, l_ref, acc_ref, *, tq, tk, lam_init):
    qrows = 2 * B_HEADS_PER_STEP
    knorm2 = knorm_ref[pl.ds(_KNORM_B0 + qrows * pl.program_id(1), qrows), 0:1]
    _softmax_value_loop(qT_ref, k_ref, vT_ref, lambda r: knorm2[r:r + 1], lambda r: r, lambda r: r // 2,
                        w_ref, s_even_ref, s_odd_ref, m_ref, l_ref, acc_ref, qrows=qrows, tq=tq, tk=tk)

    lp = lam_ref[...]
    lam = (jnp.exp(jnp.sum(lp[0:1] * lp[1:2], axis=1, keepdims=True))
           - jnp.exp(jnp.sum(lp[2:3] * lp[3:4], axis=1, keepdims=True)) + lam_init)
    for blk in range(acc_ref.shape[0] // qrows):
        for h in range(B_HEADS_PER_STEP):
            c1 = blk * qrows + 2 * h
            o = acc_ref[c1] / _denominator(l_ref, c1) - lam * (acc_ref[c1 + 1] / _denominator(l_ref, c1 + 1))
            ms = jnp.mean(o * o, axis=0, keepdims=True)
            o = ((o * lax.rsqrt(ms + NORM_EPS)) * gsub_ref[...]) * (1.0 - lam_init)
            o_ref[h * B_V_DIM:(h + 1) * B_V_DIM, blk * tq:(blk + 1) * tq] = o.astype(_BF16)


def _attn_b(qTb, kB, vTb, knorm, lam_params, gsub, *, tq, lam_init):
    n, _, s = qTb.shape
    _, _, nck, dv, tk = vTb.shape
    hps = B_HEADS_PER_STEP
    nblk = B_QUERY_BLOCKS_PER_STEP
    return pl.pallas_call(
        functools.partial(_attn_b_kernel, tq=tq, tk=tk, lam_init=lam_init),
        grid=(n, B_HEADS // hps, s // (nblk * tq)),
        in_specs=[
            pl.BlockSpec((None, hps * 2 * HEAD_DIM, nblk * tq), lambda b, hp, i: (b, hp, i)),
            pl.BlockSpec((None, s, hps * 4 * HEAD_DIM), lambda b, hp, i: (b, 0, hp)),
            pl.BlockSpec((None, hps, nck, dv, tk), lambda b, hp, i: (b, hp, 0, 0, 0)),
            pl.BlockSpec((None, _KNORM_ROWS, 128), lambda b, hp, i: (b, 0, 0)),
            pl.BlockSpec((4, HEAD_DIM), lambda b, hp, i: (0, 0)),
            pl.BlockSpec((B_V_DIM, 1), lambda b, hp, i: (0, 0)),
        ],
        out_specs=pl.BlockSpec((None, hps * B_V_DIM, nblk * tq), lambda b, hp, i: (b, hp, i)),
        out_shape=jax.ShapeDtypeStruct((n, B_V, s), _BF16),
        scratch_shapes=_attention_scratch(nblk * 2 * hps, B_V_DIM, tq, tk),
        compiler_params=_params(("parallel", "parallel", "parallel")),
        name="diff_attention",
    )(qTb, kB, vTb, knorm, lam_params, gsub)


def _mix_and_ffn(x_ref, oTa_ref, oTb_ref, mod_ref, wout_ref, gffn_ref, wgu_ref, wdn_ref, act_ref):
    gt1, sh2, sc2, gt2 = mod_ref[2:3, :], mod_ref[3:4, :], mod_ref[4:5, :], mod_ref[5:6, :]
    ts = x_ref.shape[0]
    sub = min(ts, FFN_NORM_SUBTILE)
    x1_parts, hb_parts = [], []
    for r in range(ts // sub):
        tok = slice(r * sub, (r + 1) * sub)
        mixed = _dot_tn(oTa_ref[:, tok], wout_ref[0:A_Q, :]) + _dot_tn(oTb_ref[:, tok], wout_ref[A_Q:, :])
        x1_r = x_ref[tok, :] + gt1 * mixed
        h = ((x1_r * _rms_rows(x1_r)) * gffn_ref[...]) * (1.0 + sc2) + sh2
        x1_parts.append(x1_r)
        hb_parts.append(h.astype(_BF16))
    x1 = jnp.concatenate(x1_parts, axis=0)
    hb = jnp.concatenate(hb_parts, axis=0)
    dff = wdn_ref.shape[0]
    for c in range(dff // FF_CHUNK):
        cols = slice(c * FF_CHUNK, (c + 1) * FF_CHUNK)
        gate = _dot(hb, wgu_ref[:, cols])
        up = _dot(hb, wgu_ref[:, dff + c * FF_CHUNK:dff + (c + 1) * FF_CHUNK])
        act_ref[:, cols] = ((gate * jax.nn.sigmoid(gate)) * up).astype(_BF16)
    return x1 + gt2 * _dot(act_ref[...], wdn_ref[...])


def _ffn_kernel(x_ref, oTa_ref, oTb_ref, mod_ref, wout_ref, gffn_ref, wgu_ref, wdn_ref, y_ref, act_ref):
    y_ref[...] = _mix_and_ffn(x_ref, oTa_ref, oTb_ref, mod_ref, wout_ref, gffn_ref, wgu_ref, wdn_ref, act_ref)


def _ffn_final_kernel(x_ref, oTa_ref, oTb_ref, mod_ref, wout_ref, gffn_ref, wgu_ref, wdn_ref, modf_ref, gfin_ref,
                      y_ref, act_ref):
    x = _mix_and_ffn(x_ref, oTa_ref, oTb_ref, mod_ref, wout_ref, gffn_ref, wgu_ref, wdn_ref, act_ref)
    shift, scale = modf_ref[0:1, :], modf_ref[1:2, :]
    y_ref[...] = ((x * _rms_rows(x)) * gfin_ref[...]) * (1.0 + scale) + shift


def _resident(shape):
    return pl.BlockSpec(shape, lambda b, t: (0,) * len(shape), pipeline_mode=pl.Buffered(1))


def _out_ffn(x, oTa, oTb, mod, w_out, gffn, w_gu, w_dn, final=None, *, ts):
    n, s, d = x.shape
    dff = w_dn.shape[0]
    assert dff % FF_CHUNK == 0 and w_gu.shape == (d, 2 * dff)
    in_specs = [
        pl.BlockSpec((None, ts, d), lambda b, t: (b, t, 0)),
        pl.BlockSpec((None, A_Q, ts), lambda b, t: (b, 0, t)),
        pl.BlockSpec((None, B_V, ts), lambda b, t: (b, 0, t)),
        pl.BlockSpec((None, 6, d), lambda b, t: (b, 0, 0)),
        _resident((A_Q + B_V, d)),
        pl.BlockSpec((1, d), lambda b, t: (0, 0)),
        _resident((d, 2 * dff)),
        _resident((dff, d)),
    ]
    operands = (x, oTa, oTb, mod, w_out, gffn, w_gu, w_dn)
    if final is not None:
        in_specs += [pl.BlockSpec((None, 2, d), lambda b, t: (b, 0, 0)), pl.BlockSpec((1, d), lambda b, t: (0, 0))]
        operands += tuple(final)
    return pl.pallas_call(
        _ffn_kernel if final is None else _ffn_final_kernel,
        grid=(n, s // ts),
        in_specs=in_specs,
        out_specs=pl.BlockSpec((None, ts, d), lambda b, t: (b, t, 0)),
        out_shape=jax.ShapeDtypeStruct((n, s, d), _F32),
        scratch_shapes=[pltpu.VMEM((ts, dff), _BF16)],
        compiler_params=_params(("parallel", "parallel")),
        name="out_proj_swiglu",
    )(*operands)


def _rope_table_t(s):
    def tables(pos, dim, theta):
        inv = 1.0 / (theta ** (jnp.arange(0, dim, 2, dtype=_F32) / dim))
        ang = pos.astype(_F32)[:, None] * inv[None, :]
        return jnp.cos(ang).T, jnp.sin(ang).T

    t = jnp.arange(s)
    cos_r, sin_r = tables(t // GRID_W, HEAD_DIM // 2, AXIAL_THETA)
    cos_c, sin_c = tables(t % GRID_W, HEAD_DIM // 2, AXIAL_THETA)
    cos_p, sin_p = tables(t, PARTIAL_ROT, ROPE_THETA)
    return jnp.concatenate([cos_r, sin_r, cos_c, sin_c, cos_p, sin_p], axis=0)


def _trunks(xs, cs, w_ada, b_ada, g_mix, w_in, g_q_a, g_k_a, lam_q1, lam_k1, lam_q2, lam_k2, g_subln,
            w_out, g_ffn, w_gate_up, w_down, w_ada_final, b_ada_final, g_final, *, ts, tq, tk):
    d = xs[0].shape[-1]
    depth = w_ada.shape[0]

    rows = [c.shape[0] for c in cs]
    c_pad = jnp.zeros((8, d), _F32).at[:sum(rows)].set(jnp.concatenate(cs, axis=0))
    mod_all = _modulation(c_pad, w_ada, b_ada[:, None, :])
    modf_all = _modulation(c_pad, w_ada_final[None], b_ada_final[None, None, :])[0]

    w_inT = jnp.swapaxes(w_in, 1, 2).astype(_BF16)
    w_out_b = w_out.astype(_BF16)
    w_gu = w_gate_up.astype(_BF16)
    w_dn = w_down.astype(_BF16)
    lam_params = jnp.stack([lam_q1, lam_k1, lam_q2, lam_k2], axis=1)

    outs = []
    row0 = 0
    for x, n in zip(xs, rows):
        s = x.shape[1]
        mod = mod_all[:, row0:row0 + n].reshape(depth, n, 6, d)
        modf = modf_all[row0:row0 + n].reshape(n, 2, d)
        row0 += n
        tab = _rope_table_t(s)
        for l in range(depth):
            qTa, kA, vTa, qTb, kB, vTb, knorm = _inproj(
                x, mod[l], g_mix[l][None, :], w_inT[l], g_q_a[l][:, None], g_k_a[l][:, None], tab, ts=ts, tk=tk)
            oTa = _attn_a(qTa, kA, vTa, knorm, tq=tq)
            lam_init = 0.8 - 0.6 * math.exp(-0.3 * l)
            oTb = _attn_b(qTb, kB, vTb, knorm, lam_params[l], g_subln[l][:, None], tq=tq, lam_init=lam_init)
            final = (modf, g_final[None, :]) if l == depth - 1 else None
            x = _out_ffn(x, oTa, oTb, mod[l], w_out_b[l], g_ffn[l][None, :], w_gu[l], w_dn[l], final, ts=ts)
        outs.append(x)
    return tuple(outs)


def _trunk(x, c, *weights, ts, tq, tk):
    return _trunks((x,), (c,), *weights, ts=ts, tq=tq, tk=tk)[0]


def kernel(x_prompt, x_sample, c_prompt, c_sample, w_ada, b_ada, g_mix, w_in, g_q_a, g_k_a, lam_q1, lam_k1, lam_q2, lam_k2, g_subln, w_out, g_ffn, w_gate_up, w_down, w_ada_final, b_ada_final, g_final):
    return _trunks((x_prompt, x_sample), (c_prompt, c_sample), w_ada, b_ada, g_mix, w_in, g_q_a, g_k_a,
                   lam_q1, lam_k1, lam_q2, lam_k2, g_subln, w_out, g_ffn, w_gate_up, w_down,
                   w_ada_final, b_ada_final, g_final, ts=512, tq=256, tk=256)
```
